```python
import math
import jax, jax.numpy as jnp
from jax import lax
import numpy as np

D_MODEL = 1024
BATCH = 16
SEQ = 2048
DEPTH = 2

SSM_GROUP = 16
SSM_GROUPS = D_MODEL // 64
SSM_WIDTH = SSM_GROUPS * SSM_GROUP
SSM_STATE = 64
DIFF_HEADS = D_MODEL // 256
DIFF_HEAD_DIM = 64
DIFF_WIDTH = DIFF_HEADS * 2 * DIFF_HEAD_DIM
SWA_Q_HEADS = D_MODEL // 256
SWA_KV_HEADS = 2
SWA_GROUP = SWA_Q_HEADS // SWA_KV_HEADS
SWA_HEAD_DIM = 64
SWA_WIDTH = SWA_Q_HEADS * SWA_HEAD_DIM
SWA_KV_WIDTH = SWA_KV_HEADS * SWA_HEAD_DIM
WINDOW = 128
Q_BLOCK = 128
D_MIX = SSM_WIDTH + DIFF_WIDTH + SWA_WIDTH
IN_COLS = 2 * SSM_WIDTH + 4 * DIFF_WIDTH + 2 * SWA_WIDTH + 2 * SWA_KV_WIDTH
N_ATTN_HEADS = DIFF_HEADS + SWA_Q_HEADS
RMS_EPS = 1e-6

kernel_name = "hybrid_s5_diffattn_swa_adaln"


def rms_norm(x, gain):
    xf = x.astype(jnp.float32)
    y = xf * lax.rsqrt(jnp.mean(xf * xf, axis=-1, keepdims=True) + RMS_EPS)
    return (y * gain.astype(jnp.float32)).astype(x.dtype)


def alibi_slopes():
    i = jnp.arange(1, N_ATTN_HEADS + 1, dtype=jnp.float32)
    return jnp.exp2(-i * (8.0 / N_ATTN_HEADS))


def split_cols(proj):
    sizes = [SSM_WIDTH, SSM_WIDTH,
             DIFF_WIDTH, DIFF_WIDTH, DIFF_WIDTH, DIFF_WIDTH,
             SWA_WIDTH, SWA_KV_WIDTH, SWA_KV_WIDTH, SWA_WIDTH]
    out, start = [], 0
    for n in sizes:
        out.append(proj[..., start:start + n])
        start += n
    return out


def s5_mixer(u, lam_re, lam_im, log_step, b_re, b_im, c_re, c_im, d_skip, w_glu, b_glu):
    bsz, seq, _ = u.shape
    ug = u.reshape(bsz, seq, SSM_GROUPS, SSM_GROUP).astype(jnp.float32)
    step = jnp.exp(log_step.astype(jnp.float32))[:, None]
    lr = lam_re.astype(jnp.float32)
    li = lam_im.astype(jnp.float32)
    mag = jnp.exp(lr * step)
    ang = li * step
    ab_re = mag * jnp.cos(ang)
    ab_im = mag * jnp.sin(ang)
    den = lr * lr + li * li
    f_re = ((ab_re - 1.0) * lr + ab_im * li) / den
    f_im = (ab_im * lr - (ab_re - 1.0) * li) / den
    br = b_re.astype(jnp.float32)
    bi = b_im.astype(jnp.float32)
    bb_re = f_re[..., None] * br - f_im[..., None] * bi
    bb_im = f_re[..., None] * bi + f_im[..., None] * br
    bu_re = jnp.einsum('bsgh,gph->bsgp', ug, bb_re)
    bu_im = jnp.einsum('bsgh,gph->bsgp', ug, bb_im)
    a_re = jnp.broadcast_to(ab_re, bu_re.shape)
    a_im = jnp.broadcast_to(ab_im, bu_im.shape)

    def combine(e1, e2):
        a1r, a1i, b1r, b1i = e1
        a2r, a2i, b2r, b2i = e2
        return (a2r * a1r - a2i * a1i,
                a2r * a1i + a2i * a1r,
                a2r * b1r - a2i * b1i + b2r,
                a2r * b1i + a2i * b1r + b2i)

    _, _, st_re, st_im = lax.associative_scan(combine, (a_re, a_im, bu_re, bu_im), axis=1)
    y = (jnp.einsum('bsgp,ghp->bsgh', st_re, c_re.astype(jnp.float32))
         - jnp.einsum('bsgp,ghp->bsgh', st_im, c_im.astype(jnp.float32))
         + d_skip.astype(jnp.float32).reshape(SSM_GROUPS, SSM_GROUP) * ug)
    y = jax.nn.gelu(y.reshape(bsz, seq, SSM_WIDTH))
    y = y * jax.nn.sigmoid(y @ w_glu.astype(jnp.float32) + b_glu.astype(jnp.float32))
    return y.astype(u.dtype)


def diff_attention(q, k, v, slopes, lam, subln_gain, lam_init):
    bsz, seq = q.shape[0], q.shape[1]
    nb = seq // Q_BLOCK
    scale = DIFF_HEAD_DIM ** -0.5
    k1, k2 = k[:, :, :, 0], k[:, :, :, 1]

    def to_blocks(t):
        return t.reshape(bsz, nb, Q_BLOCK, DIFF_HEADS, DIFF_HEAD_DIM).transpose(1, 0, 2, 3, 4)

    q1b, q2b = to_blocks(q[:, :, :, 0]), to_blocks(q[:, :, :, 1])
    key_pos = jnp.arange(seq)

    def block(args):
        q1_blk, q2_blk, blk = args
        t = blk * Q_BLOCK + jnp.arange(Q_BLOCK)
        dist = t[:, None] - key_pos[None, :]
        mask = dist >= 0
        bias = -slopes[:, None, None] * dist.astype(jnp.float32)[None]

        def probs(qb, kk):
            s = jnp.einsum('bqhd,bkhd->bhqk', qb, kk).astype(jnp.float32) * scale + bias
            return jax.nn.softmax(jnp.where(mask, s, -jnp.inf), axis=-1)

        p = probs(q1_blk, k1) - lam * probs(q2_blk, k2)
        return jnp.einsum('bhqk,bkhd->bqhd', p.astype(v.dtype), v)

    o = lax.map(block, (q1b, q2b, jnp.arange(nb)))
    o = o.transpose(1, 0, 2, 3, 4).reshape(bsz, seq, DIFF_HEADS, 2 * DIFF_HEAD_DIM)
    o = rms_norm(o, subln_gain) * (1.0 - lam_init)
    return o.reshape(bsz, seq, DIFF_WIDTH)


def sliding_window_attention(q, k, v, slopes, sinks):
    bsz, seq = q.shape[0], q.shape[1]
    nb = seq // WINDOW
    scale = SWA_HEAD_DIM ** -0.5
    qb = q.reshape(bsz, nb, WINDOW, SWA_KV_HEADS, SWA_GROUP, SWA_HEAD_DIM)

    def band(t):
        cur = t.reshape(bsz, nb, WINDOW, SWA_KV_HEADS, SWA_HEAD_DIM)
        prev = jnp.concatenate([jnp.zeros_like(cur[:, :1]), cur[:, :-1]], axis=1)
        return jnp.concatenate([prev, cur], axis=2)

    kk, vv = band(k), band(v)
    t_loc = WINDOW + jnp.arange(WINDOW)
    s_loc = jnp.arange(2 * WINDOW)
    dist = t_loc[:, None] - s_loc[None, :]
    valid = (jnp.arange(nb)[:, None] * WINDOW - WINDOW + s_loc[None, :]) >= 0
    mask = ((dist >= 0) & (dist < WINDOW))[None] & valid[:, None, :]
    sl = slopes.reshape(SWA_KV_HEADS, SWA_GROUP)
    bias = -sl[:, :, None, None] * dist.astype(jnp.float32)
    s = jnp.einsum('bnqkgd,bnskd->bnkgqs', qb, kk).astype(jnp.float32) * scale + bias[None, None]
    s = jnp.where(mask[None, :, None, None], s, -jnp.inf)
    sink = sinks.astype(jnp.float32).reshape(SWA_KV_HEADS, SWA_GROUP)[None, None, :, :, None, None]
    m = jnp.maximum(jnp.max(s, axis=-1, keepdims=True), sink)
    e = jnp.exp(s - m)
    p = e / (jnp.sum(e, axis=-1, keepdims=True) + jnp.exp(sink - m))
    o = jnp.einsum('bnkgqs,bnskd->bnqkgd', p.astype(v.dtype), vv)
    return o.reshape(bsz, seq, SWA_WIDTH)


def hybrid_layer(x, cond, layer_idx, norm_gain, ada_w, ada_b, w_in, w_out,
                 ssm_lam_re, ssm_lam_im, ssm_log_step, ssm_b_re, ssm_b_im,
                 ssm_c_re, ssm_c_im, ssm_d, glu_w, glu_b,
                 diff_lq1, diff_lk1, diff_lq2, diff_lk2, diff_subln, swa_sinks):
    bsz, seq, _ = x.shape
    mod = cond @ ada_w + ada_b
    shift, scl, gate = jnp.split(mod, 3, axis=-1)
    h = rms_norm(x, norm_gain) * (1.0 + scl[:, None, :]) + shift[:, None, :]
    proj = h @ w_in
    (u, z_ssm, qd, kd, vd, z_diff, qs, ks, vs, z_swa) = split_cols(proj)

    slopes = alibi_slopes()
    swa_slopes = slopes[:SWA_Q_HEADS]
    diff_slopes = slopes[SWA_Q_HEADS:]

    y_ssm = s5_mixer(u, ssm_lam_re, ssm_lam_im, ssm_log_step, ssm_b_re, ssm_b_im,
                     ssm_c_re, ssm_c_im, ssm_d, glu_w, glu_b) * jax.nn.silu(z_ssm)

    lam_init = 0.8 - 0.6 * math.exp(-0.3 * layer_idx)
    lam = (jnp.exp(jnp.sum(diff_lq1.astype(jnp.float32) * diff_lk1.astype(jnp.float32)))
           - jnp.exp(jnp.sum(diff_lq2.astype(jnp.float32) * diff_lk2.astype(jnp.float32)))
           + lam_init)
    qd = qd.reshape(bsz, seq, DIFF_HEADS, 2, DIFF_HEAD_DIM)
    kd = kd.reshape(bsz, seq, DIFF_HEADS, 2, DIFF_HEAD_DIM)
    vd = vd.reshape(bsz, seq, DIFF_HEADS, 2 * DIFF_HEAD_DIM)
    y_diff = diff_attention(qd, kd, vd, diff_slopes, lam, diff_subln, lam_init) * jax.nn.silu(z_diff)

    qs = qs.reshape(bsz, seq, SWA_Q_HEADS, SWA_HEAD_DIM)
    ks = ks.reshape(bsz, seq, SWA_KV_HEADS, SWA_HEAD_DIM)
    vs = vs.reshape(bsz, seq, SWA_KV_HEADS, SWA_HEAD_DIM)
    y_swa = sliding_window_attention(qs, ks, vs, swa_slopes, swa_sinks) * jax.nn.silu(z_swa)

    y = jnp.concatenate([y_ssm, y_diff.astype(x.dtype), y_swa.astype(x.dtype)], axis=-1) @ w_out
    return x + gate[:, None, :] * y


def setup_inputs(seed: int = 0) -> dict:
    key = jax.random.key(seed)
    ks = jax.random.split(key, 26)
    f32 = jnp.float32

    def nrm(k, shape, scale):
        return jax.random.normal(k, shape, f32) * scale

    L, G, P, H = DEPTH, SSM_GROUPS, SSM_STATE, SSM_GROUP
    ada_b = nrm(ks[4], (L, 3 * D_MODEL), 0.01)
    ada_b = ada_b.at[:, 2 * D_MODEL:].add(1.0)
    lam_im = jnp.broadcast_to(jnp.pi * jnp.arange(P, dtype=f32), (L, G, P)) + nrm(ks[8], (L, G, P), 0.01)
    return {
        "x": nrm(ks[0], (BATCH, SEQ, D_MODEL), 1.0),
        "c": nrm(ks[1], (BATCH, D_MODEL), 1.0),
        "norm_gain": 1.0 + nrm(ks[2], (L, D_MODEL), 0.02),
        "ada_w": nrm(ks[3], (L, D_MODEL, 3 * D_MODEL), 0.1 * D_MODEL ** -0.5),
        "ada_b": ada_b,
        "w_in": nrm(ks[5], (L, D_MODEL, IN_COLS), D_MODEL ** -0.5),
        "w_out": nrm(ks[6], (L, D_MIX, D_MODEL), D_MIX ** -0.5),
        "ssm_lam_re": -0.5 + nrm(ks[7], (L, G, P), 0.01),
        "ssm_lam_im": lam_im,
        "ssm_log_step": jax.random.uniform(ks[9], (L, G), f32, math.log(1e-3), math.log(1e-1)),
        "ssm_b_re": nrm(ks[10], (L, G, P, H), (2 * H) ** -0.5),
        "ssm_b_im": nrm(ks[11], (L, G, P, H), (2 * H) ** -0.5),
        "ssm_c_re": nrm(ks[12], (L, G, H, P), 0.5),
        "ssm_c_im": nrm(ks[13], (L, G, H, P), 0.5),
        "ssm_d": nrm(ks[14], (L, SSM_WIDTH), 1.0),
        "glu_w": nrm(ks[15], (L, SSM_WIDTH, SSM_WIDTH), SSM_WIDTH ** -0.5),
        "glu_b": nrm(ks[16], (L, SSM_WIDTH), 0.01),
        "diff_lq1": nrm(ks[17], (L, DIFF_HEAD_DIM), 0.1),
        "diff_lk1": nrm(ks[18], (L, DIFF_HEAD_DIM), 0.1),
        "diff_lq2": nrm(ks[19], (L, DIFF_HEAD_DIM), 0.1),
        "diff_lk2": nrm(ks[20], (L, DIFF_HEAD_DIM), 0.1),
        "diff_subln": 1.0 + nrm(ks[21], (L, 2 * DIFF_HEAD_DIM), 0.02),
        "swa_sinks": nrm(ks[22], (L, SWA_Q_HEADS), 1.0),
        "final_gain": 1.0 + nrm(ks[23], (D_MODEL,), 0.02),
    }


def reference(x, c, norm_gain, ada_w, ada_b, w_in, w_out, ssm_lam_re, ssm_lam_im,
              ssm_log_step, ssm_b_re, ssm_b_im, ssm_c_re, ssm_c_im, ssm_d, glu_w, glu_b,
              diff_lq1, diff_lk1, diff_lq2, diff_lk2, diff_subln, swa_sinks, final_gain):
    cond = jax.nn.silu(c)
    for l in range(DEPTH):
        x = hybrid_layer(x, cond, l, norm_gain[l], ada_w[l], ada_b[l], w_in[l], w_out[l],
                         ssm_lam_re[l], ssm_lam_im[l], ssm_log_step[l], ssm_b_re[l], ssm_b_im[l],
                         ssm_c_re[l], ssm_c_im[l], ssm_d[l], glu_w[l], glu_b[l],
                         diff_lq1[l], diff_lk1[l], diff_lq2[l], diff_lk2[l], diff_subln[l],
                         swa_sinks[l])
    return rms_norm(x, final_gain)
```

```python
import functools
import math

import jax
import jax.numpy as jnp
from jax import lax
from jax.experimental import pallas as pl
from jax.experimental.pallas import tpu as pltpu

F32 = jnp.float32
BF16 = jnp.bfloat16

D_MODEL = 1024
DEPTH = 2
SSM_GROUP = 16
SSM_GROUPS = 16
SSM_WIDTH = 256
SSM_STATE = 64
SSM_LANES = SSM_GROUPS * SSM_STATE
DIFF_HEADS = 4
DIFF_HEAD_DIM = 64
DIFF_WIDTH = 512
SWA_Q_HEADS = 4
SWA_KV_HEADS = 2
SWA_GROUP = 2
SWA_HEAD_DIM = 64
SWA_WIDTH = 256
SWA_KV_WIDTH = 128
WINDOW = 128
N_ATTN_HEADS = 8
RMS_EPS = 1e-6

PROJ_NAMES = ("u", "z_ssm", "qd", "kd", "vd", "z_diff", "qs", "ks", "vs", "z_swa")
PROJ_SIZES = (SSM_WIDTH, SSM_WIDTH, DIFF_WIDTH, DIFF_WIDTH, DIFF_WIDTH, DIFF_WIDTH,
              SWA_WIDTH, SWA_KV_WIDTH, SWA_KV_WIDTH, SWA_WIDTH)
IN_COLS = sum(PROJ_SIZES)

SUBLANES = 8
VMEM_LIMIT = 48 * 1024 * 1024

TOK_TILE = 512
S5_CHUNK = 256
ATT_TQ = 256
ATT_TK = 256


def _silu(x):
    return x * jax.nn.sigmoid(x)


def _alibi_slope(head_index):
    return 2.0 ** (-(head_index + 1) * (8.0 / N_ATTN_HEADS))


def _ada_kernel(c_ref, w_ref, b_ref, o_ref):
    cond = _silu(c_ref[...])
    o_ref[0] = jnp.dot(cond, w_ref[0], preferred_element_type=F32) + b_ref[0]


def _ada(c, ada_w, ada_b):
    bsz = c.shape[0]
    col = D_MODEL
    return pl.pallas_call(
        _ada_kernel,
        grid=(DEPTH, 3),
        in_specs=[
            pl.BlockSpec((bsz, D_MODEL), lambda l, j: (0, 0)),
            pl.BlockSpec((1, D_MODEL, col), lambda l, j: (l, 0, j)),
            pl.BlockSpec((1, 1, col), lambda l, j: (l, 0, j)),
        ],
        out_specs=pl.BlockSpec((1, bsz, col), lambda l, j: (l, 0, j)),
        out_shape=jax.ShapeDtypeStruct((DEPTH, bsz, 3 * D_MODEL), F32),
        compiler_params=pltpu.CompilerParams(
            dimension_semantics=("parallel", "parallel"), vmem_limit_bytes=VMEM_LIMIT),
        name="ada",
    )(c, ada_w, ada_b.reshape(DEPTH, 1, 3 * D_MODEL))


def _inproj_kernel(x_ref, mod_ref, g_ref, w_ref, *out_refs):
    xf = x_ref[0]
    y = xf * lax.rsqrt(jnp.mean(xf * xf, axis=-1, keepdims=True) + RMS_EPS) * g_ref[...]
    mod = mod_ref[0]
    h = (y * (1.0 + mod[1:2, :]) + mod[0:1, :]).astype(BF16)
    start = 0
    for o_ref, n in zip(out_refs, PROJ_SIZES):
        o_ref[0] = jnp.dot(h, w_ref[:, start:start + n], preferred_element_type=F32).astype(BF16)
        start += n


def _inproj(x, mod_l, gain, w_in_bf16):
    bsz, seq, _ = x.shape
    grid = (bsz, seq // TOK_TILE)
    return pl.pallas_call(
        _inproj_kernel,
        grid=grid,
        in_specs=[
            pl.BlockSpec((1, TOK_TILE, D_MODEL), lambda b, i: (b, i, 0)),
            pl.BlockSpec((1, 3, D_MODEL), lambda b, i: (b, 0, 0)),
            pl.BlockSpec((1, D_MODEL), lambda b, i: (0, 0)),
            pl.BlockSpec((D_MODEL, IN_COLS), lambda b, i: (0, 0)),
        ],
        out_specs=[pl.BlockSpec((1, TOK_TILE, n), lambda b, i: (b, i, 0)) for n in PROJ_SIZES],
        out_shape=[jax.ShapeDtypeStruct((bsz, seq, n), BF16) for n in PROJ_SIZES],
        compiler_params=pltpu.CompilerParams(
            dimension_semantics=("parallel", "parallel"), vmem_limit_bytes=VMEM_LIMIT),
        name="inproj",
    )(x, mod_l, gain.reshape(1, D_MODEL), w_in_bf16)


def _s5_tables(lam_re, lam_im, log_step, b_re, b_im, c_re, c_im):
    g, p, h = SSM_GROUPS, SSM_STATE, SSM_GROUP
    step = jnp.exp(log_step.astype(F32))[:, None]
    lr = lam_re.astype(F32)
    li = lam_im.astype(F32)
    mag = jnp.exp(lr * step)
    ang = li * step
    ab_re = mag * jnp.cos(ang)
    ab_im = mag * jnp.sin(ang)
    den = lr * lr + li * li
    f_re = ((ab_re - 1.0) * lr + ab_im * li) / den
    f_im = (ab_im * lr - (ab_re - 1.0) * li) / den
    br = b_re.astype(F32)
    bi = b_im.astype(F32)
    bb_re = f_re[..., None] * br - f_im[..., None] * bi
    bb_im = f_re[..., None] * bi + f_im[..., None] * br
    eye = jnp.eye(g, dtype=F32)
    bbd_re = jnp.einsum("gph,gk->ghkp", bb_re, eye).reshape(g * h, g * p)
    bbd_im = jnp.einsum("gph,gk->ghkp", bb_im, eye).reshape(g * h, g * p)
    bbd = jnp.concatenate([bbd_re, bbd_im], axis=1).astype(BF16)
    cbd_re = jnp.einsum("ghp,gk->gpkh", c_re.astype(F32), eye).reshape(g * p, g * h)
    cbd_im = jnp.einsum("ghp,gk->gpkh", c_im.astype(F32), eye).reshape(g * p, g * h)
    cbd = jnp.concatenate([cbd_re, -cbd_im], axis=0).astype(BF16)

    def power(n):
        m = jnp.exp(lr * step * n)
        return (m * jnp.cos(ang * n)).reshape(1, g * p), (m * jnp.sin(ang * n)).reshape(1, g * p)

    row = jnp.arange(SUBLANES, dtype=jnp.int32)[:, None]
    shift_tabs = []
    d = 1
    while d < SUBLANES:
        pr, pi = power(float(d))
        keep = (row >= d).astype(F32)
        shift_tabs += [keep * pr, keep * pi]
        d *= 2
    shift_tab = jnp.stack(shift_tabs)
    carry_re = jnp.concatenate([power(float(r + 1))[0] for r in range(SUBLANES)], axis=0)
    carry_im = jnp.concatenate([power(float(r + 1))[1] for r in range(SUBLANES)], axis=0)
    carry_tab = jnp.stack([carry_re, carry_im])
    return bbd, cbd, shift_tab, carry_tab


def _s5_kernel(u_ref, z_ref, bbd_ref, cbd_ref, shift_ref, carry_ref, d_ref, wg_ref, bg_ref,
               o_ref, st_ref, carry_scr):
    chunk = u_ref.shape[1]
    n_blocks = chunk // SUBLANES

    @pl.when(pl.program_id(1) == 0)
    def _():
        carry_scr[...] = jnp.zeros_like(carry_scr)

    u = u_ref[0]
    st_ref[...] = jnp.dot(u, bbd_ref[...], preferred_element_type=F32)

    shifts = []
    d = 1
    k = 0
    while d < SUBLANES:
        shifts.append((d, shift_ref[2 * k], shift_ref[2 * k + 1]))
        d *= 2
        k += 1
    pw_re = carry_ref[0]
    pw_im = carry_ref[1]

    def block(i, carry):
        c_re, c_im = carry
        r0 = pl.multiple_of(i * SUBLANES, SUBLANES)
        x_re = st_ref[pl.ds(r0, SUBLANES), :SSM_LANES]
        x_im = st_ref[pl.ds(r0, SUBLANES), SSM_LANES:]
        for d, a_re, a_im in shifts:
            s_re = pltpu.roll(x_re, d, axis=0)
            s_im = pltpu.roll(x_im, d, axis=0)
            x_re, x_im = (x_re + (a_re * s_re - a_im * s_im),
                          x_im + (a_re * s_im + a_im * s_re))
        x_re, x_im = (x_re + (pw_re * c_re - pw_im * c_im),
                      x_im + (pw_re * c_im + pw_im * c_re))
        st_ref[pl.ds(r0, SUBLANES), :SSM_LANES] = x_re
        st_ref[pl.ds(r0, SUBLANES), SSM_LANES:] = x_im
        last = SUBLANES - 1
        return (jnp.broadcast_to(x_re[last:, :], x_re.shape),
                jnp.broadcast_to(x_im[last:, :], x_im.shape))

    c_re, c_im = lax.fori_loop(0, n_blocks, block, (carry_scr[0], carry_scr[1]))
    carry_scr[0] = c_re
    carry_scr[1] = c_im

    y = jnp.dot(st_ref[...].astype(BF16), cbd_ref[...], preferred_element_type=F32)
    y = y + d_ref[...] * u.astype(F32)
    y = jax.nn.gelu(y)
    gate = jnp.dot(y.astype(BF16), wg_ref[...], preferred_element_type=F32) + bg_ref[...]
    y = y * jax.nn.sigmoid(gate)
    o_ref[0] = (y * _silu(z_ref[0].astype(F32))).astype(BF16)


def _s5(u, z, tables, d_skip, w_glu, b_glu):
    bsz, seq, _ = u.shape
    bbd, cbd, shift_tab, carry_tab = tables
    full = lambda *shape: pl.BlockSpec(shape, lambda b, i: (0,) * len(shape))
    return pl.pallas_call(
        _s5_kernel,
        grid=(bsz, seq // S5_CHUNK),
        in_specs=[
            pl.BlockSpec((1, S5_CHUNK, SSM_WIDTH), lambda b, i: (b, i, 0)),
            pl.BlockSpec((1, S5_CHUNK, SSM_WIDTH), lambda b, i: (b, i, 0)),
            full(SSM_WIDTH, 2 * SSM_LANES),
            full(2 * SSM_LANES, SSM_WIDTH),
            full(*shift_tab.shape),
            full(*carry_tab.shape),
            full(1, SSM_WIDTH),
            full(SSM_WIDTH, SSM_WIDTH),
            full(1, SSM_WIDTH),
        ],
        out_specs=pl.BlockSpec((1, S5_CHUNK, SSM_WIDTH), lambda b, i: (b, i, 0)),
        out_shape=jax.ShapeDtypeStruct((bsz, seq, SSM_WIDTH), BF16),
        scratch_shapes=[
            pltpu.VMEM((S5_CHUNK, 2 * SSM_LANES), F32),
            pltpu.VMEM((2, SUBLANES, SSM_LANES), F32),
        ],
        compiler_params=pltpu.CompilerParams(
            dimension_semantics=("parallel", "arbitrary"), vmem_limit_bytes=VMEM_LIMIT),
        name="s5",
    )(u, z, bbd, cbd, shift_tab, carry_tab, d_skip.reshape(1, SSM_WIDTH).astype(F32),
      w_glu.astype(BF16), b_glu.reshape(1, SSM_WIDTH).astype(F32))


def _diff_kernel(q_ref, k_ref, v_ref, z_ref, lam_ref, gain_ref, slope_ref, o_ref, *, lam_init):
    qi = pl.program_id(2)
    dh = DIFF_HEAD_DIM
    scale = dh ** -0.5
    q = q_ref[0]
    q1 = (q[:, :dh].astype(F32) * scale).astype(BF16)
    q2 = (q[:, dh:].astype(F32) * scale).astype(BF16)
    slope = slope_ref[0][:, :1]
    qpos = qi * ATT_TQ + lax.broadcasted_iota(jnp.int32, (ATT_TQ, ATT_TK), 0)
    col = lax.broadcasted_iota(jnp.int32, (ATT_TQ, ATT_TK), 1)

    def softmax_step(s, dist, m, l, acc, v):
        s = jnp.where(dist >= 0, s - slope * dist.astype(F32), -jnp.inf)
        m_new = jnp.maximum(m, jnp.max(s, axis=-1, keepdims=True))
        alpha = jnp.exp(m - m_new)
        p = jnp.exp(s - m_new)
        l = alpha * l + jnp.sum(p, axis=-1, keepdims=True)
        acc = alpha * acc + jnp.dot(p.astype(BF16), v, preferred_element_type=F32)
        return m_new, l, acc

    def kv_step(j, carry):
        m1, l1, a1, m2, l2, a2 = carry
        k0 = pl.multiple_of(j * ATT_TK, ATT_TK)
        k = k_ref[0, pl.ds(k0, ATT_TK), :]
        v = v_ref[0, pl.ds(k0, ATT_TK), :]
        dist = qpos - (k0 + col)
        dn = (((1,), (1,)), ((), ()))
        s1 = lax.dot_general(q1, k[:, :dh], dn, preferred_element_type=F32)
        s2 = lax.dot_general(q2, k[:, dh:], dn, preferred_element_type=F32)
        m1, l1, a1 = softmax_step(s1, dist, m1, l1, a1, v)
        m2, l2, a2 = softmax_step(s2, dist, m2, l2, a2, v)
        return m1, l1, a1, m2, l2, a2

    neg = jnp.full((ATT_TQ, 1), -jnp.inf, F32)
    zero = jnp.zeros((ATT_TQ, 1), F32)
    acc0 = jnp.zeros((ATT_TQ, 2 * dh), F32)
    m1, l1, a1, m2, l2, a2 = lax.fori_loop(0, qi + 1, kv_step, (neg, zero, acc0, neg, zero, acc0))

    lp = lam_ref[...]
    lam = (jnp.exp(jnp.sum(lp[0:1] * lp[1:2], axis=-1, keepdims=True))
           - jnp.exp(jnp.sum(lp[2:3] * lp[3:4], axis=-1, keepdims=True)) + lam_init)
    o = a1 / l1 - lam * (a2 / l2)
    o = o * lax.rsqrt(jnp.mean(o * o, axis=-1, keepdims=True) + RMS_EPS) * gain_ref[...]
    o = o * (1.0 - lam_init)
    o_ref[0] = (o * _silu(z_ref[0].astype(F32))).astype(BF16)


def _diff_attention(qd, kd, vd, zd, lam_params, subln_gain, lam_init):
    bsz, seq, _ = qd.shape
    hw = 2 * DIFF_HEAD_DIM
    slopes = jnp.stack([jnp.full((1, 128), _alibi_slope(SWA_Q_HEADS + h), F32)
                        for h in range(DIFF_HEADS)])
    tile = pl.BlockSpec((1, ATT_TQ, hw), lambda b, h, i: (b, i, h))
    whole = pl.BlockSpec((1, seq, hw), lambda b, h, i: (b, 0, h))
    return pl.pallas_call(
        functools.partial(_diff_kernel, lam_init=lam_init),
        grid=(bsz, DIFF_HEADS, seq // ATT_TQ),
        in_specs=[
            tile, whole, whole, tile,
            pl.BlockSpec((4, DIFF_HEAD_DIM), lambda b, h, i: (0, 0)),
            pl.BlockSpec((1, hw), lambda b, h, i: (0, 0)),
            pl.BlockSpec((1, 1, 128), lambda b, h, i: (h, 0, 0)),
        ],
        out_specs=tile,
        out_shape=jax.ShapeDtypeStruct((bsz, seq, DIFF_WIDTH), BF16),
        compiler_params=pltpu.CompilerParams(
            dimension_semantics=("parallel", "parallel", "parallel"), vmem_limit_bytes=VMEM_LIMIT),
        name="diffattn",
    )(qd, kd, vd, zd, lam_params, subln_gain.reshape(1, hw).astype(F32), slopes)


def _swa_kernel(q_ref, kp_ref, kc_ref, vp_ref, vc_ref, z_ref, sink_ref, o_ref):
    i = pl.program_id(1)
    dh = SWA_HEAD_DIM
    w = WINDOW
    scale = dh ** -0.5
    row = lax.broadcasted_iota(jnp.int32, (w, w), 0)
    col = lax.broadcasted_iota(jnp.int32, (w, w), 1)
    dist_prev = (w + row - col)
    dist_cur = row - col
    mask_prev = jnp.logical_and(dist_prev < w, i > 0)
    mask_cur = dist_cur >= 0
    dn = (((1,), (1,)), ((), ()))
    for hq in range(SWA_Q_HEADS):
        kh = hq // SWA_GROUP
        slope = _alibi_slope(hq)
        q = (q_ref[0, :, hq * dh:(hq + 1) * dh].astype(F32) * scale).astype(BF16)
        kp = kp_ref[0, :, kh * dh:(kh + 1) * dh]
        kc = kc_ref[0, :, kh * dh:(kh + 1) * dh]
        vp = vp_ref[0, :, kh * dh:(kh + 1) * dh]
        vc = vc_ref[0, :, kh * dh:(kh + 1) * dh]
        sp = lax.dot_general(q, kp, dn, preferred_element_type=F32)
        sc = lax.dot_general(q, kc, dn, preferred_element_type=F32)
        sp = jnp.where(mask_prev, sp - slope * dist_prev.astype(F32), -jnp.inf)
        sc = jnp.where(mask_cur, sc - slope * dist_cur.astype(F32), -jnp.inf)
        sink = sink_ref[hq][:, :1]
        m = jnp.maximum(jnp.maximum(jnp.max(sp, axis=-1, keepdims=True),
                                    jnp.max(sc, axis=-1, keepdims=True)), sink)
        ep = jnp.exp(sp - m)
        ec = jnp.exp(sc - m)
        denom = (jnp.sum(ep, axis=-1, keepdims=True) + jnp.sum(ec, axis=-1, keepdims=True)
                 + jnp.exp(sink - m))
        inv = 1.0 / denom
        o = (jnp.dot((ep * inv).astype(BF16), vp, preferred_element_type=F32)
             + jnp.dot((ec * inv).astype(BF16), vc, preferred_element_type=F32))
        z = z_ref[0, :, hq * dh:(hq + 1) * dh].astype(F32)
        o_ref[0, :, hq * dh:(hq + 1) * dh] = (o * _silu(z)).astype(BF16)


def _swa(qs, ks, vs, zs, sinks):
    bsz, seq, _ = qs.shape
    nb = seq // WINDOW
    sink_rows = jnp.broadcast_to(sinks.astype(F32).reshape(SWA_Q_HEADS, 1, 1), (SWA_Q_HEADS, 1, 128))
    qspec = pl.BlockSpec((1, WINDOW, SWA_WIDTH), lambda b, i: (b, i, 0))
    cur = pl.BlockSpec((1, WINDOW, SWA_KV_WIDTH), lambda b, i: (b, i, 0))
    prev = pl.BlockSpec((1, WINDOW, SWA_KV_WIDTH), lambda b, i: (b, jnp.maximum(i - 1, 0), 0))
    return pl.pallas_call(
        _swa_kernel,
        grid=(bsz, nb),
        in_specs=[qspec, prev, cur, prev, cur, qspec,
                  pl.BlockSpec((SWA_Q_HEADS, 1, 128), lambda b, i: (0, 0, 0))],
        out_specs=qspec,
        out_shape=jax.ShapeDtypeStruct((bsz, seq, SWA_WIDTH), BF16),
        compiler_params=pltpu.CompilerParams(
            dimension_semantics=("parallel", "parallel"), vmem_limit_bytes=VMEM_LIMIT),
        name="swa",
    )(qs, ks, ks, vs, vs, zs, sink_rows)


def _outproj_kernel(x_ref, ys_ref, yd_ref, yw_ref, mod_ref, w_ref, fg_ref, o_ref, *, final_norm):
    a = SSM_WIDTH
    b = SSM_WIDTH + DIFF_WIDTH
    y = (jnp.dot(ys_ref[0], w_ref[:a, :], preferred_element_type=F32)
         + jnp.dot(yd_ref[0], w_ref[a:b, :], preferred_element_type=F32)
         + jnp.dot(yw_ref[0], w_ref[b:, :], preferred_element_type=F32))
    out = x_ref[0] + mod_ref[0][2:3, :] * y
    if final_norm:
        out = out * lax.rsqrt(jnp.mean(out * out, axis=-1, keepdims=True) + RMS_EPS) * fg_ref[...]
    o_ref[0] = out


def _outproj(x, y_ssm, y_diff, y_swa, mod_l, w_out_bf16, final_gain, final_norm):
    bsz, seq, _ = x.shape
    tok = lambda n: pl.BlockSpec((1, TOK_TILE, n), lambda b, i: (b, i, 0))
    return pl.pallas_call(
        functools.partial(_outproj_kernel, final_norm=final_norm),
        grid=(bsz, seq // TOK_TILE),
        in_specs=[
            tok(D_MODEL), tok(SSM_WIDTH), tok(DIFF_WIDTH), tok(SWA_WIDTH),
            pl.BlockSpec((1, 3, D_MODEL), lambda b, i: (b, 0, 0)),
            pl.BlockSpec((D_MODEL, D_MODEL), lambda b, i: (0, 0)),
            pl.BlockSpec((1, D_MODEL), lambda b, i: (0, 0)),
        ],
        out_specs=tok(D_MODEL),
        out_shape=jax.ShapeDtypeStruct((bsz, seq, D_MODEL), F32),
        compiler_params=pltpu.CompilerParams(
            dimension_semantics=("parallel", "parallel"), vmem_limit_bytes=VMEM_LIMIT),
        name="outproj",
    )(x, y_ssm, y_diff, y_swa, mod_l, w_out_bf16, final_gain.reshape(1, D_MODEL).astype(F32))


def kernel(x, c, norm_gain, ada_w, ada_b, w_in, w_out, ssm_lam_re, ssm_lam_im, ssm_log_step,
           ssm_b_re, ssm_b_im, ssm_c_re, ssm_c_im, ssm_d, glu_w, glu_b,
           diff_lq1, diff_lk1, diff_lq2, diff_lk2, diff_subln, swa_sinks, final_gain):
    bsz = x.shape[0]
    mod = _ada(c, ada_w, ada_b).reshape(DEPTH, bsz, 3, D_MODEL)
    for l in range(DEPTH):
        proj = dict(zip(PROJ_NAMES, _inproj(x, mod[l], norm_gain[l], w_in[l].astype(BF16))))
        tables = _s5_tables(ssm_lam_re[l], ssm_lam_im[l], ssm_log_step[l],
                            ssm_b_re[l], ssm_b_im[l], ssm_c_re[l], ssm_c_im[l])
        y_ssm = _s5(proj["u"], proj["z_ssm"], tables, ssm_d[l], glu_w[l], glu_b[l])
        lam_init = 0.8 - 0.6 * math.exp(-0.3 * l)
        lam_params = jnp.stack([diff_lq1[l], diff_lk1[l], diff_lq2[l], diff_lk2[l]]).astype(F32)
        y_diff = _diff_attention(proj["qd"], proj["kd"], proj["vd"], proj["z_diff"],
                                 lam_params, diff_subln[l], lam_init)
        y_swa = _swa(proj["qs"], proj["ks"], proj["vs"], proj["z_swa"], swa_sinks[l])
        x = _outproj(x, y_ssm, y_diff, y_swa, mod[l], w_out[l].astype(BF16), final_gain,
                     final_norm=(l == DEPTH - 1))
    return x
```

```python
import functools
import math

import jax
import jax.numpy as jnp
from jax import lax
from jax.experimental import pallas as pl
from jax.experimental.pallas import tpu as pltpu

F32 = jnp.float32
BF16 = jnp.bfloat16

D_MODEL = 1024
DEPTH = 2
SSM_GROUP = 16
SSM_GROUPS = 16
SSM_WIDTH = 256
SSM_STATE = 64
SSM_LANES = SSM_GROUPS * SSM_STATE
DIFF_HEADS = 4
DIFF_HEAD_DIM = 64
DIFF_WIDTH = 512
SWA_Q_HEADS = 4
SWA_KV_HEADS = 2
SWA_GROUP = 2
SWA_HEAD_DIM = 64
SWA_WIDTH = 256
SWA_KV_WIDTH = 128
WINDOW = 128
N_ATTN_HEADS = 8
RMS_EPS = 1e-6

PROJ_NAMES = ("u", "z_ssm", "qd", "kd", "vd", "z_diff", "qs", "ks", "vs", "z_swa")
PROJ_SIZES = (SSM_WIDTH, SSM_WIDTH, DIFF_WIDTH, DIFF_WIDTH, DIFF_WIDTH, DIFF_WIDTH,
              SWA_WIDTH, SWA_KV_WIDTH, SWA_KV_WIDTH, SWA_WIDTH)
IN_COLS = sum(PROJ_SIZES)

SUBLANES = 8
VMEM_LIMIT = 48 * 1024 * 1024

TOK_TILE = 512
S5_CHUNK = 256
ATT_TQ = 256
ATT_TK = 256


def _silu(x):
    return x * jax.nn.sigmoid(x)


def _alibi_slope(head_index):
    return 2.0 ** (-(head_index + 1) * (8.0 / N_ATTN_HEADS))


def _ada_kernel(c_ref, w_ref, b_ref, o_ref):
    cond = _silu(c_ref[...])
    o_ref[0] = jnp.dot(cond, w_ref[0], preferred_element_type=F32) + b_ref[0]


def _ada(c, ada_w, ada_b):
    bsz = c.shape[0]
    col = D_MODEL
    return pl.pallas_call(
        _ada_kernel,
        grid=(DEPTH, 3),
        in_specs=[
            pl.BlockSpec((bsz, D_MODEL), lambda l, j: (0, 0)),
            pl.BlockSpec((1, D_MODEL, col), lambda l, j: (l, 0, j)),
            pl.BlockSpec((1, 1, col), lambda l, j: (l, 0, j)),
        ],
        out_specs=pl.BlockSpec((1, bsz, col), lambda l, j: (l, 0, j)),
        out_shape=jax.ShapeDtypeStruct((DEPTH, bsz, 3 * D_MODEL), F32),
        compiler_params=pltpu.CompilerParams(
            dimension_semantics=("parallel", "parallel"), vmem_limit_bytes=VMEM_LIMIT),
        name="ada",
    )(c, ada_w, ada_b.reshape(DEPTH, 1, 3 * D_MODEL))


def _inproj_kernel(x_ref, mod_ref, g_ref, w_ref, wvt_ref, *out_refs):
    xf = x_ref[0]
    y = xf * lax.rsqrt(jnp.mean(xf * xf, axis=-1, keepdims=True) + RMS_EPS) * g_ref[...]
    mod = mod_ref[0]
    h = (y * (1.0 + mod[1:2, :]) + mod[0:1, :]).astype(BF16)
    start = 0
    for name, o_ref, n in zip(PROJ_NAMES, out_refs, PROJ_SIZES):
        if name == "vd":
            vt = lax.dot_general(wvt_ref[...], h, (((1,), (1,)), ((), ())),
                                 preferred_element_type=F32).astype(BF16)
            for t in range(TOK_TILE // ATT_TK):
                o_ref[0, t] = vt[:, t * ATT_TK:(t + 1) * ATT_TK]
        else:
            o_ref[0] = jnp.dot(h, w_ref[:, start:start + n], preferred_element_type=F32).astype(BF16)
        start += n


def _inproj(x, mod_l, gain, w_in_bf16):
    bsz, seq, _ = x.shape
    grid = (bsz, seq // TOK_TILE)
    vd_start = sum(PROJ_SIZES[:PROJ_NAMES.index("vd")])
    w_vd_t = w_in_bf16[:, vd_start:vd_start + DIFF_WIDTH].T
    tiles_per_step = TOK_TILE // ATT_TK
    out_specs, out_shape = [], []
    for name, n in zip(PROJ_NAMES, PROJ_SIZES):
        if name == "vd":
            out_specs.append(pl.BlockSpec((1, tiles_per_step, n, ATT_TK), lambda b, i: (b, i, 0, 0)))
            out_shape.append(jax.ShapeDtypeStruct((bsz, seq // ATT_TK, n, ATT_TK), BF16))
        else:
            out_specs.append(pl.BlockSpec((1, TOK_TILE, n), lambda b, i: (b, i, 0)))
            out_shape.append(jax.ShapeDtypeStruct((bsz, seq, n), BF16))
    return pl.pallas_call(
        _inproj_kernel,
        grid=grid,
        in_specs=[
            pl.BlockSpec((1, TOK_TILE, D_MODEL), lambda b, i: (b, i, 0)),
            pl.BlockSpec((1, 3, D_MODEL), lambda b, i: (b, 0, 0)),
            pl.BlockSpec((1, D_MODEL), lambda b, i: (0, 0)),
            pl.BlockSpec((D_MODEL, IN_COLS), lambda b, i: (0, 0)),
            pl.BlockSpec((DIFF_WIDTH, D_MODEL), lambda b, i: (0, 0)),
        ],
        out_specs=out_specs,
        out_shape=out_shape,
        compiler_params=pltpu.CompilerParams(
            dimension_semantics=("parallel", "parallel"), vmem_limit_bytes=VMEM_LIMIT),
        name="inproj",
    )(x, mod_l, gain.reshape(1, D_MODEL), w_in_bf16, w_vd_t)


def _s5_tables(lam_re, lam_im, log_step, b_re, b_im, c_re, c_im):
    g, p, h = SSM_GROUPS, SSM_STATE, SSM_GROUP
    step = jnp.exp(log_step.astype(F32))[:, None]
    lr = lam_re.astype(F32)
    li = lam_im.astype(F32)
    mag = jnp.exp(lr * step)
    ang = li * step
    ab_re = mag * jnp.cos(ang)
    ab_im = mag * jnp.sin(ang)
    den = lr * lr + li * li
    f_re = ((ab_re - 1.0) * lr + ab_im * li) / den
    f_im = (ab_im * lr - (ab_re - 1.0) * li) / den
    br = b_re.astype(F32)
    bi = b_im.astype(F32)
    bb_re = f_re[..., None] * br - f_im[..., None] * bi
    bb_im = f_re[..., None] * bi + f_im[..., None] * br
    eye = jnp.eye(g, dtype=F32)
    bbd_re = jnp.einsum("gph,gk->ghkp", bb_re, eye).reshape(g * h, g * p)
    bbd_im = jnp.einsum("gph,gk->ghkp", bb_im, eye).reshape(g * h, g * p)
    bbd = jnp.concatenate([bbd_re, bbd_im], axis=1).astype(BF16)
    cbd_re = jnp.einsum("ghp,gk->gpkh", c_re.astype(F32), eye).reshape(g * p, g * h)
    cbd_im = jnp.einsum("ghp,gk->gpkh", c_im.astype(F32), eye).reshape(g * p, g * h)
    cbd = jnp.concatenate([cbd_re, -cbd_im], axis=0).astype(BF16)

    def power(n):
        m = jnp.exp(lr * step * n)
        return (m * jnp.cos(ang * n)).reshape(1, g * p), (m * jnp.sin(ang * n)).reshape(1, g * p)

    row = jnp.arange(SUBLANES, dtype=jnp.int32)[:, None]
    shift_tabs = []
    d = 1
    while d < SUBLANES:
        pr, pi = power(float(d))
        keep = (row >= d).astype(F32)
        shift_tabs += [keep * pr, keep * pi]
        d *= 2
    shift_tab = jnp.stack(shift_tabs)
    carry_re = jnp.concatenate([power(float(r + 1))[0] for r in range(SUBLANES)], axis=0)
    carry_im = jnp.concatenate([power(float(r + 1))[1] for r in range(SUBLANES)], axis=0)
    carry_tab = jnp.stack([carry_re, carry_im])
    return bbd, cbd, shift_tab, carry_tab


def _s5_kernel(u_ref, z_ref, bbd_ref, cbd_ref, shift_ref, carry_ref, d_ref, wg_ref, bg_ref,
               o_ref, st_ref, carry_scr):
    chunk = u_ref.shape[1]
    n_blocks = chunk // SUBLANES

    @pl.when(pl.program_id(1) == 0)
    def _():
        carry_scr[...] = jnp.zeros_like(carry_scr)

    u = u_ref[0]
    st_ref[...] = jnp.dot(u, bbd_ref[...], preferred_element_type=F32)

    shifts = []
    d = 1
    k = 0
    while d < SUBLANES:
        shifts.append((d, shift_ref[2 * k], shift_ref[2 * k + 1]))
        d *= 2
        k += 1
    pw_re = carry_ref[0]
    pw_im = carry_ref[1]

    def block(i, carry):
        c_re, c_im = carry
        r0 = pl.multiple_of(i * SUBLANES, SUBLANES)
        x_re = st_ref[pl.ds(r0, SUBLANES), :SSM_LANES]
        x_im = st_ref[pl.ds(r0, SUBLANES), SSM_LANES:]
        for d, a_re, a_im in shifts:
            s_re = pltpu.roll(x_re, d, axis=0)
            s_im = pltpu.roll(x_im, d, axis=0)
            x_re, x_im = (x_re + (a_re * s_re - a_im * s_im),
                          x_im + (a_re * s_im + a_im * s_re))
        x_re, x_im = (x_re + (pw_re * c_re - pw_im * c_im),
                      x_im + (pw_re * c_im + pw_im * c_re))
        st_ref[pl.ds(r0, SUBLANES), :SSM_LANES] = x_re
        st_ref[pl.ds(r0, SUBLANES), SSM_LANES:] = x_im
        last = SUBLANES - 1
        return (jnp.broadcast_to(x_re[last:, :], x_re.shape),
                jnp.broadcast_to(x_im[last:, :], x_im.shape))

    c_re, c_im = lax.fori_loop(0, n_blocks, block, (carry_scr[0], carry_scr[1]))
    carry_scr[0] = c_re
    carry_scr[1] = c_im

    y = jnp.dot(st_ref[...].astype(BF16), cbd_ref[...], preferred_element_type=F32)
    y = y + d_ref[...] * u.astype(F32)
    y = jax.nn.gelu(y)
    gate = jnp.dot(y.astype(BF16), wg_ref[...], preferred_element_type=F32) + bg_ref[...]
    y = y * jax.nn.sigmoid(gate)
    o_ref[0] = (y * _silu(z_ref[0].astype(F32))).astype(BF16)


def _s5(u, z, tables, d_skip, w_glu, b_glu):
    bsz, seq, _ = u.shape
    bbd, cbd, shift_tab, carry_tab = tables
    full = lambda *shape: pl.BlockSpec(shape, lambda b, i: (0,) * len(shape))
    return pl.pallas_call(
        _s5_kernel,
        grid=(bsz, seq // S5_CHUNK),
        in_specs=[
            pl.BlockSpec((1, S5_CHUNK, SSM_WIDTH), lambda b, i: (b, i, 0)),
            pl.BlockSpec((1, S5_CHUNK, SSM_WIDTH), lambda b, i: (b, i, 0)),
            full(SSM_WIDTH, 2 * SSM_LANES),
            full(2 * SSM_LANES, SSM_WIDTH),
            full(*shift_tab.shape),
            full(*carry_tab.shape),
            full(1, SSM_WIDTH),
            full(SSM_WIDTH, SSM_WIDTH),
            full(1, SSM_WIDTH),
        ],
        out_specs=pl.BlockSpec((1, S5_CHUNK, SSM_WIDTH), lambda b, i: (b, i, 0)),
        out_shape=jax.ShapeDtypeStruct((bsz, seq, SSM_WIDTH), BF16),
        scratch_shapes=[
            pltpu.VMEM((S5_CHUNK, 2 * SSM_LANES), F32),
            pltpu.VMEM((2, SUBLANES, SSM_LANES), F32),
        ],
        compiler_params=pltpu.CompilerParams(
            dimension_semantics=("parallel", "arbitrary"), vmem_limit_bytes=VMEM_LIMIT),
        name="s5",
    )(u, z, bbd, cbd, shift_tab, carry_tab, d_skip.reshape(1, SSM_WIDTH).astype(F32),
      w_glu.astype(BF16), b_glu.reshape(1, SSM_WIDTH).astype(F32))


ACC_ROWS = 2 * DIFF_HEAD_DIM + 16
POS_SPLIT = 64


def _diff_key_features(seq):
    pos = jnp.arange(seq, dtype=jnp.int32)
    hi = (pos // POS_SPLIT).astype(F32) * POS_SPLIT
    lo = (pos % POS_SPLIT).astype(F32)
    lane = jnp.arange(2 * DIFF_HEAD_DIM, dtype=jnp.int32) % DIFF_HEAD_DIM
    tabs = []
    for h in range(DIFF_HEADS):
        slope = _alibi_slope(SWA_Q_HEADS + h)
        tab = (jnp.where(lane[None, :] == 0, slope * hi[:, None], 0.0)
               + jnp.where(lane[None, :] == 1, slope * lo[:, None], 0.0))
        tabs.append(tab)
    return jnp.stack(tabs).astype(BF16)


def _diff_kernel(q_ref, k_ref, kf_ref, vt_ref, z_ref, lam_ref, gain_ref, o_ref, acc_ref, s_ref,
                 *, lam_init):
    qi = pl.program_id(1)
    dh = DIFF_HEAD_DIM
    hw = 2 * dh
    scale = dh ** -0.5
    lane_q = lax.broadcasted_iota(jnp.int32, (ATT_TQ, hw), 1)
    ones_feat = jnp.where(lane_q % dh < 2, 1.0, 0.0).astype(BF16)
    qps = []
    for h in range(DIFF_HEADS):
        q = (q_ref[0, :, h * hw:(h + 1) * hw].astype(F32) * scale).astype(BF16)
        qps.append(jnp.where(lane_q < dh, q, ones_feat))
        qps.append(jnp.where(lane_q >= dh, q, ones_feat))
    lane_k = lax.broadcasted_iota(jnp.int32, (ATT_TK, hw), 1)
    ones_row = jnp.where(lax.broadcasted_iota(jnp.int32, (ACC_ROWS - hw, ATT_TK), 0) == 0,
                         1.0, 0.0).astype(BF16)
    krow = lax.broadcasted_iota(jnp.int32, (ATT_TK, ATT_TQ), 0)
    qcol = lax.broadcasted_iota(jnp.int32, (ATT_TK, ATT_TQ), 1)
    nt = (((1,), (1,)), ((), ()))

    acc_ref[...] = jnp.zeros_like(acc_ref)

    def kv_step(j, ms, masked):
        k0 = pl.multiple_of(j * ATT_TK, ATT_TK)
        m_new = []
        for h in range(DIFF_HEADS):
            k = k_ref[0, pl.ds(k0, ATT_TK), h * hw:(h + 1) * hw]
            kf = kf_ref[h, pl.ds(k0, ATT_TK), :]
            for c, kp in ((2 * h, jnp.where(lane_k < dh, k, kf)),
                          (2 * h + 1, jnp.where(lane_k >= dh, k, kf))):
                s = lax.dot_general(kp, qps[c], nt, preferred_element_type=F32)
                if masked:
                    s = jnp.where(krow <= qcol, s, -jnp.inf)
                s_ref[c] = s
                m_new.append(jnp.maximum(ms[c], jnp.max(s, axis=0, keepdims=True)))
        for h in range(DIFF_HEADS):
            vta = jnp.concatenate([vt_ref[0, j, h * hw:(h + 1) * hw, :], ones_row], axis=0)
            for c in (2 * h, 2 * h + 1):
                alpha = jnp.exp(ms[c] - m_new[c])
                p = jnp.exp(s_ref[c] - m_new[c]).astype(BF16)
                acc_ref[c] = alpha * acc_ref[c] + jnp.dot(vta, p, preferred_element_type=F32)
        return tuple(m_new)

    neg = jnp.full((1, ATT_TQ), -jnp.inf, F32)
    ms = lax.fori_loop(0, qi, lambda j, c: kv_step(j, c, False), (neg,) * (2 * DIFF_HEADS))
    kv_step(qi, ms, True)

    lp = lam_ref[...]
    lam = (jnp.exp(jnp.sum(lp[0:1] * lp[1:2], axis=-1, keepdims=True))
           - jnp.exp(jnp.sum(lp[2:3] * lp[3:4], axis=-1, keepdims=True)) + lam_init)
    for h in range(DIFF_HEADS):
        a1 = acc_ref[2 * h]
        a2 = acc_ref[2 * h + 1]
        o_t = (a1[:hw] * (1.0 / a1[hw:hw + 1]) - lam * (a2[:hw] * (1.0 / a2[hw:hw + 1])))
        o = o_t.T
        o = o * lax.rsqrt(jnp.mean(o * o, axis=-1, keepdims=True) + RMS_EPS) * gain_ref[...]
        o = o * (1.0 - lam_init)
        z = z_ref[0, :, h * hw:(h + 1) * hw].astype(F32)
        o_ref[0, :, h * hw:(h + 1) * hw] = (o * _silu(z)).astype(BF16)


def _diff_attention(qd, kd, vd_t, zd, lam_params, subln_gain, lam_init):
    bsz, seq, _ = qd.shape
    hw = 2 * DIFF_HEAD_DIM
    n_kv = seq // ATT_TK
    tile = pl.BlockSpec((1, ATT_TQ, DIFF_WIDTH), lambda b, i: (b, i, 0))
    return pl.pallas_call(
        functools.partial(_diff_kernel, lam_init=lam_init),
        grid=(bsz, seq // ATT_TQ),
        in_specs=[
            tile,
            pl.BlockSpec((1, seq, DIFF_WIDTH), lambda b, i: (b, 0, 0)),
            pl.BlockSpec((DIFF_HEADS, seq, hw), lambda b, i: (0, 0, 0)),
            pl.BlockSpec((1, n_kv, DIFF_WIDTH, ATT_TK), lambda b, i: (b, 0, 0, 0)),
            tile,
            pl.BlockSpec((4, DIFF_HEAD_DIM), lambda b, i: (0, 0)),
            pl.BlockSpec((1, hw), lambda b, i: (0, 0)),
        ],
        out_specs=tile,
        out_shape=jax.ShapeDtypeStruct((bsz, seq, DIFF_WIDTH), BF16),
        scratch_shapes=[pltpu.VMEM((2 * DIFF_HEADS, ACC_ROWS, ATT_TQ), F32),
                        pltpu.VMEM((2 * DIFF_HEADS, ATT_TK, ATT_TQ), F32)],
        compiler_params=pltpu.CompilerParams(
            dimension_semantics=("parallel", "parallel"), vmem_limit_bytes=VMEM_LIMIT),
        name="diffattn",
    )(qd, kd, _diff_key_features(seq), vd_t, zd, lam_params, subln_gain.reshape(1, hw).astype(F32))


def _swa_kernel(q_ref, kp_ref, kc_ref, vp_ref, vc_ref, z_ref, sink_ref, o_ref):
    i = pl.program_id(1)
    dh = SWA_HEAD_DIM
    w = WINDOW
    scale = dh ** -0.5
    row = lax.broadcasted_iota(jnp.int32, (w, w), 0)
    col = lax.broadcasted_iota(jnp.int32, (w, w), 1)
    dist_prev = (w + row - col)
    dist_cur = row - col
    mask_prev = jnp.logical_and(dist_prev < w, i > 0)
    mask_cur = dist_cur >= 0
    dn = (((1,), (1,)), ((), ()))
    for hq in range(SWA_Q_HEADS):
        kh = hq // SWA_GROUP
        slope = _alibi_slope(hq)
        q = (q_ref[0, :, hq * dh:(hq + 1) * dh].astype(F32) * scale).astype(BF16)
        kp = kp_ref[0, :, kh * dh:(kh + 1) * dh]
        kc = kc_ref[0, :, kh * dh:(kh + 1) * dh]
        vp = vp_ref[0, :, kh * dh:(kh + 1) * dh]
        vc = vc_ref[0, :, kh * dh:(kh + 1) * dh]
        sp = lax.dot_general(q, kp, dn, preferred_element_type=F32)
        sc = lax.dot_general(q, kc, dn, preferred_element_type=F32)
        sp = jnp.where(mask_prev, sp - slope * dist_prev.astype(F32), -jnp.inf)
        sc = jnp.where(mask_cur, sc - slope * dist_cur.astype(F32), -jnp.inf)
        sink = sink_ref[hq][:, :1]
        m = jnp.maximum(jnp.maximum(jnp.max(sp, axis=-1, keepdims=True),
                                    jnp.max(sc, axis=-1, keepdims=True)), sink)
        ep = jnp.exp(sp - m)
        ec = jnp.exp(sc - m)
        denom = (jnp.sum(ep, axis=-1, keepdims=True) + jnp.sum(ec, axis=-1, keepdims=True)
                 + jnp.exp(sink - m))
        inv = 1.0 / denom
        o = (jnp.dot((ep * inv).astype(BF16), vp, preferred_element_type=F32)
             + jnp.dot((ec * inv).astype(BF16), vc, preferred_element_type=F32))
        z = z_ref[0, :, hq * dh:(hq + 1) * dh].astype(F32)
        o_ref[0, :, hq * dh:(hq + 1) * dh] = (o * _silu(z)).astype(BF16)


def _swa(qs, ks, vs, zs, sinks):
    bsz, seq, _ = qs.shape
    nb = seq // WINDOW
    sink_rows = jnp.broadcast_to(sinks.astype(F32).reshape(SWA_Q_HEADS, 1, 1), (SWA_Q_HEADS, 1, 128))
    qspec = pl.BlockSpec((1, WINDOW, SWA_WIDTH), lambda b, i: (b, i, 0))
    cur = pl.BlockSpec((1, WINDOW, SWA_KV_WIDTH), lambda b, i: (b, i, 0))
    prev = pl.BlockSpec((1, WINDOW, SWA_KV_WIDTH), lambda b, i: (b, jnp.maximum(i - 1, 0), 0))
    return pl.pallas_call(
        _swa_kernel,
        grid=(bsz, nb),
        in_specs=[qspec, prev, cur, prev, cur, qspec,
                  pl.BlockSpec((SWA_Q_HEADS, 1, 128), lambda b, i: (0, 0, 0))],
        out_specs=qspec,
        out_shape=jax.ShapeDtypeStruct((bsz, seq, SWA_WIDTH), BF16),
        compiler_params=pltpu.CompilerParams(
            dimension_semantics=("parallel", "parallel"), vmem_limit_bytes=VMEM_LIMIT),
        name="swa",
    )(qs, ks, ks, vs, vs, zs, sink_rows)


def _outproj_kernel(x_ref, ys_ref, yd_ref, yw_ref, mod_ref, w_ref, fg_ref, o_ref, *, final_norm):
    a = SSM_WIDTH
    b = SSM_WIDTH + DIFF_WIDTH
    y = (jnp.dot(ys_ref[0], w_ref[:a, :], preferred_element_type=F32)
         + jnp.dot(yd_ref[0], w_ref[a:b, :], preferred_element_type=F32)
         + jnp.dot(yw_ref[0], w_ref[b:, :], preferred_element_type=F32))
    out = x_ref[0] + mod_ref[0][2:3, :] * y
    if final_norm:
        out = out * lax.rsqrt(jnp.mean(out * out, axis=-1, keepdims=True) + RMS_EPS) * fg_ref[...]
    o_ref[0] = out


def _outproj(x, y_ssm, y_diff, y_swa, mod_l, w_out_bf16, final_gain, final_norm):
    bsz, seq, _ = x.shape
    tok = lambda n: pl.BlockSpec((1, TOK_TILE, n), lambda b, i: (b, i, 0))
    return pl.pallas_call(
        functools.partial(_outproj_kernel, final_norm=final_norm),
        grid=(bsz, seq // TOK_TILE),
        in_specs=[
            tok(D_MODEL), tok(SSM_WIDTH), tok(DIFF_WIDTH), tok(SWA_WIDTH),
            pl.BlockSpec((1, 3, D_MODEL), lambda b, i: (b, 0, 0)),
            pl.BlockSpec((D_MODEL, D_MODEL), lambda b, i: (0, 0)),
            pl.BlockSpec((1, D_MODEL), lambda b, i: (0, 0)),
        ],
        out_specs=tok(D_MODEL),
        out_shape=jax.ShapeDtypeStruct((bsz, seq, D_MODEL), F32),
        compiler_params=pltpu.CompilerParams(
            dimension_semantics=("parallel", "parallel"), vmem_limit_bytes=VMEM_LIMIT),
        name="outproj",
    )(x, y_ssm, y_diff, y_swa, mod_l, w_out_bf16, final_gain.reshape(1, D_MODEL).astype(F32))


def kernel(x, c, norm_gain, ada_w, ada_b, w_in, w_out, ssm_lam_re, ssm_lam_im, ssm_log_step,
           ssm_b_re, ssm_b_im, ssm_c_re, ssm_c_im, ssm_d, glu_w, glu_b,
           diff_lq1, diff_lk1, diff_lq2, diff_lk2, diff_subln, swa_sinks, final_gain):
    bsz = x.shape[0]
    mod = _ada(c, ada_w, ada_b).reshape(DEPTH, bsz, 3, D_MODEL)
    for l in range(DEPTH):
        proj = dict(zip(PROJ_NAMES, _inproj(x, mod[l], norm_gain[l], w_in[l].astype(BF16))))
        tables = _s5_tables(ssm_lam_re[l], ssm_lam_im[l], ssm_log_step[l],
                            ssm_b_re[l], ssm_b_im[l], ssm_c_re[l], ssm_c_im[l])
        y_ssm = _s5(proj["u"], proj["z_ssm"], tables, ssm_d[l], glu_w[l], glu_b[l])
        lam_init = 0.8 - 0.6 * math.exp(-0.3 * l)
        lam_params = jnp.stack([diff_lq1[l], diff_lk1[l], diff_lq2[l], diff_lk2[l]]).astype(F32)
        y_diff = _diff_attention(proj["qd"], proj["kd"], proj["vd"], proj["z_diff"],
                                 lam_params, diff_subln[l], lam_init)
        y_swa = _swa(proj["qs"], proj["ks"], proj["vs"], proj["z_swa"], swa_sinks[l])
        x = _outproj(x, y_ssm, y_diff, y_swa, mod[l], w_out[l].astype(BF16), final_gain,
                     final_norm=(l == DEPTH - 1))
    return x
```

```python
import functools
import math

import jax
import jax.numpy as jnp
from jax import lax
from jax.experimental import pallas as pl
from jax.experimental.pallas import tpu as pltpu

F32 = jnp.float32
BF16 = jnp.bfloat16

D_MODEL = 1024
DEPTH = 2
SSM_GROUP = 16
SSM_GROUPS = 16
SSM_WIDTH = 256
SSM_STATE = 64
SSM_LANES = SSM_GROUPS * SSM_STATE
DIFF_HEADS = 4
DIFF_HEAD_DIM = 64
DIFF_WIDTH = 512
SWA_Q_HEADS = 4
SWA_KV_HEADS = 2
SWA_GROUP = 2
SWA_HEAD_DIM = 64
SWA_WIDTH = 256
SWA_KV_WIDTH = 128
WINDOW = 128
N_ATTN_HEADS = 8
RMS_EPS = 1e-6

PROJ_NAMES = ("u", "z_ssm", "qd", "kd", "vd", "z_diff", "qs", "ks", "vs", "z_swa")
PROJ_SIZES = (SSM_WIDTH, SSM_WIDTH, DIFF_WIDTH, DIFF_WIDTH, DIFF_WIDTH, DIFF_WIDTH,
              SWA_WIDTH, SWA_KV_WIDTH, SWA_KV_WIDTH, SWA_WIDTH)
IN_COLS = sum(PROJ_SIZES)

SUBLANES = 8
VMEM_LIMIT = 48 * 1024 * 1024

TOK_TILE = 512
S5_CHUNK = 256
ATT_TQ = 256
ATT_TK = 256


def _silu(x):
    return x * jax.nn.sigmoid(x)


def _alibi_slope(head_index):
    return 2.0 ** (-(head_index + 1) * (8.0 / N_ATTN_HEADS))


def _ada_kernel(c_ref, w_ref, b_ref, o_ref):
    cond = _silu(c_ref[...])
    o_ref[0] = jnp.dot(cond, w_ref[0], preferred_element_type=F32) + b_ref[0]


def _ada(c, ada_w, ada_b):
    bsz = c.shape[0]
    col = D_MODEL
    return pl.pallas_call(
        _ada_kernel,
        grid=(DEPTH, 3),
        in_specs=[
            pl.BlockSpec((bsz, D_MODEL), lambda l, j: (0, 0)),
            pl.BlockSpec((1, D_MODEL, col), lambda l, j: (l, 0, j)),
            pl.BlockSpec((1, 1, col), lambda l, j: (l, 0, j)),
        ],
        out_specs=pl.BlockSpec((1, bsz, col), lambda l, j: (l, 0, j)),
        out_shape=jax.ShapeDtypeStruct((DEPTH, bsz, 3 * D_MODEL), F32),
        compiler_params=pltpu.CompilerParams(
            dimension_semantics=("parallel", "parallel"), vmem_limit_bytes=VMEM_LIMIT),
        name="ada",
    )(c, ada_w, ada_b.reshape(DEPTH, 1, 3 * D_MODEL))


def _inproj_kernel(x_ref, mod_ref, g_ref, w_ref, wvt_ref, *out_refs):
    xf = x_ref[0]
    y = xf * lax.rsqrt(jnp.mean(xf * xf, axis=-1, keepdims=True) + RMS_EPS) * g_ref[...]
    mod = mod_ref[0]
    h = (y * (1.0 + mod[1:2, :]) + mod[0:1, :]).astype(BF16)
    vt = lax.dot_general(wvt_ref[...], h, (((1,), (1,)), ((), ())),
                         preferred_element_type=F32).astype(BF16)
    start = 0
    for name, o_ref, n in zip(PROJ_NAMES, out_refs, PROJ_SIZES):
        if name == "vd":
            for t in range(TOK_TILE // ATT_TK):
                o_ref[0, t] = vt[:DIFF_WIDTH, t * ATT_TK:(t + 1) * ATT_TK]
        elif name == "vs":
            o_ref[0] = vt[DIFF_WIDTH:, :]
        else:
            o_ref[0] = jnp.dot(h, w_ref[:, start:start + n], preferred_element_type=F32).astype(BF16)
        start += n


def _proj_start(name):
    return sum(PROJ_SIZES[:PROJ_NAMES.index(name)])


def _inproj(x, mod_l, gain, w_in):
    bsz, seq, _ = x.shape
    grid = (bsz, seq // TOK_TILE)
    w_in_bf16 = w_in.astype(BF16)
    qs0 = _proj_start("qs")
    dh = SWA_HEAD_DIM
    order = (0, 2, 1, 3)
    qs_cols = jnp.concatenate([w_in_bf16[:, qs0 + h * dh:qs0 + (h + 1) * dh] for h in order], axis=1)
    w_in_bf16 = lax.dynamic_update_slice(w_in_bf16, qs_cols, (0, qs0))
    vd0, vs0 = _proj_start("vd"), _proj_start("vs")
    w_v_t = jnp.concatenate([w_in_bf16[:, vd0:vd0 + DIFF_WIDTH],
                             w_in_bf16[:, vs0:vs0 + SWA_KV_WIDTH]], axis=1).T
    tiles_per_step = TOK_TILE // ATT_TK
    out_specs, out_shape = [], []
    for name, n in zip(PROJ_NAMES, PROJ_SIZES):
        if name == "vd":
            out_specs.append(pl.BlockSpec((1, tiles_per_step, n, ATT_TK), lambda b, i: (b, i, 0, 0)))
            out_shape.append(jax.ShapeDtypeStruct((bsz, seq // ATT_TK, n, ATT_TK), BF16))
        elif name == "vs":
            out_specs.append(pl.BlockSpec((1, n, TOK_TILE), lambda b, i: (b, 0, i)))
            out_shape.append(jax.ShapeDtypeStruct((bsz, n, seq), BF16))
        else:
            out_specs.append(pl.BlockSpec((1, TOK_TILE, n), lambda b, i: (b, i, 0)))
            out_shape.append(jax.ShapeDtypeStruct((bsz, seq, n), BF16))
    return pl.pallas_call(
        _inproj_kernel,
        grid=grid,
        in_specs=[
            pl.BlockSpec((1, TOK_TILE, D_MODEL), lambda b, i: (b, i, 0)),
            pl.BlockSpec((1, 3, D_MODEL), lambda b, i: (b, 0, 0)),
            pl.BlockSpec((1, D_MODEL), lambda b, i: (0, 0)),
            pl.BlockSpec((D_MODEL, IN_COLS), lambda b, i: (0, 0)),
            pl.BlockSpec((DIFF_WIDTH + SWA_KV_WIDTH, D_MODEL), lambda b, i: (0, 0)),
        ],
        out_specs=out_specs,
        out_shape=out_shape,
        compiler_params=pltpu.CompilerParams(
            dimension_semantics=("parallel", "parallel"), vmem_limit_bytes=VMEM_LIMIT),
        name="inproj",
    )(x, mod_l, gain.reshape(1, D_MODEL), w_in_bf16, w_v_t)


def _s5_tables(lam_re, lam_im, log_step, b_re, b_im, c_re, c_im):
    g, p, h = SSM_GROUPS, SSM_STATE, SSM_GROUP
    step = jnp.exp(log_step.astype(F32))[:, None]
    lr = lam_re.astype(F32)
    li = lam_im.astype(F32)
    mag = jnp.exp(lr * step)
    ang = li * step
    ab_re = mag * jnp.cos(ang)
    ab_im = mag * jnp.sin(ang)
    den = lr * lr + li * li
    f_re = ((ab_re - 1.0) * lr + ab_im * li) / den
    f_im = (ab_im * lr - (ab_re - 1.0) * li) / den
    br = b_re.astype(F32)
    bi = b_im.astype(F32)
    bb_re = f_re[..., None] * br - f_im[..., None] * bi
    bb_im = f_re[..., None] * bi + f_im[..., None] * br
    eye = jnp.eye(g, dtype=F32)
    bbd_re = jnp.einsum("gph,gk->ghkp", bb_re, eye).reshape(g * h, g * p)
    bbd_im = jnp.einsum("gph,gk->ghkp", bb_im, eye).reshape(g * h, g * p)
    bbd = jnp.concatenate([bbd_re, bbd_im], axis=1).astype(BF16)
    cbd_re = jnp.einsum("ghp,gk->gpkh", c_re.astype(F32), eye).reshape(g * p, g * h)
    cbd_im = jnp.einsum("ghp,gk->gpkh", c_im.astype(F32), eye).reshape(g * p, g * h)
    cbd = jnp.concatenate([cbd_re, -cbd_im], axis=0).astype(BF16)

    def power(n):
        m = jnp.exp(lr * step * n)
        return (m * jnp.cos(ang * n)).reshape(1, g * p), (m * jnp.sin(ang * n)).reshape(1, g * p)

    row = jnp.arange(SUBLANES, dtype=jnp.int32)[:, None]
    shift_tabs = []
    d = 1
    while d < SUBLANES:
        pr, pi = power(float(d))
        keep = (row >= d).astype(F32)
        shift_tabs += [keep * pr, keep * pi]
        d *= 2
    shift_tab = jnp.stack(shift_tabs)
    carry_re = jnp.concatenate([power(float(r + 1))[0] for r in range(SUBLANES)], axis=0)
    carry_im = jnp.concatenate([power(float(r + 1))[1] for r in range(SUBLANES)], axis=0)
    carry_tab = jnp.stack([carry_re, carry_im])
    return bbd, cbd, shift_tab, carry_tab


def _s5_kernel(u_ref, z_ref, bbd_ref, cbd_ref, shift_ref, carry_ref, d_ref, wg_ref, bg_ref,
               o_ref, st_ref, carry_scr):
    chunk = u_ref.shape[1]
    n_blocks = chunk // SUBLANES

    @pl.when(pl.program_id(1) == 0)
    def _():
        carry_scr[...] = jnp.zeros_like(carry_scr)

    u = u_ref[0]
    st_ref[...] = jnp.dot(u, bbd_ref[...], preferred_element_type=F32)

    shifts = []
    d = 1
    k = 0
    while d < SUBLANES:
        shifts.append((d, shift_ref[2 * k], shift_ref[2 * k + 1]))
        d *= 2
        k += 1
    pw_re = carry_ref[0]
    pw_im = carry_ref[1]

    def block(i, carry):
        c_re, c_im = carry
        r0 = pl.multiple_of(i * SUBLANES, SUBLANES)
        x_re = st_ref[pl.ds(r0, SUBLANES), :SSM_LANES]
        x_im = st_ref[pl.ds(r0, SUBLANES), SSM_LANES:]
        for d, a_re, a_im in shifts:
            s_re = pltpu.roll(x_re, d, axis=0)
            s_im = pltpu.roll(x_im, d, axis=0)
            x_re, x_im = (x_re + (a_re * s_re - a_im * s_im),
                          x_im + (a_re * s_im + a_im * s_re))
        x_re, x_im = (x_re + (pw_re * c_re - pw_im * c_im),
                      x_im + (pw_re * c_im + pw_im * c_re))
        st_ref[pl.ds(r0, SUBLANES), :SSM_LANES] = x_re
        st_ref[pl.ds(r0, SUBLANES), SSM_LANES:] = x_im
        last = SUBLANES - 1
        return (jnp.broadcast_to(x_re[last:, :], x_re.shape),
                jnp.broadcast_to(x_im[last:, :], x_im.shape))

    c_re, c_im = lax.fori_loop(0, n_blocks, block, (carry_scr[0], carry_scr[1]))
    carry_scr[0] = c_re
    carry_scr[1] = c_im

    y = jnp.dot(st_ref[...].astype(BF16), cbd_ref[...], preferred_element_type=F32)
    y = y + d_ref[...] * u.astype(F32)
    y = jax.nn.gelu(y)
    gate = jnp.dot(y.astype(BF16), wg_ref[...], preferred_element_type=F32) + bg_ref[...]
    y = y * jax.nn.sigmoid(gate)
    o_ref[0] = (y * _silu(z_ref[0].astype(F32))).astype(BF16)


def _s5(u, z, tables, d_skip, w_glu, b_glu):
    bsz, seq, _ = u.shape
    bbd, cbd, shift_tab, carry_tab = tables
    full = lambda *shape: pl.BlockSpec(shape, lambda b, i: (0,) * len(shape))
    return pl.pallas_call(
        _s5_kernel,
        grid=(bsz, seq // S5_CHUNK),
        in_specs=[
            pl.BlockSpec((1, S5_CHUNK, SSM_WIDTH), lambda b, i: (b, i, 0)),
            pl.BlockSpec((1, S5_CHUNK, SSM_WIDTH), lambda b, i: (b, i, 0)),
            full(SSM_WIDTH, 2 * SSM_LANES),
            full(2 * SSM_LANES, SSM_WIDTH),
            full(*shift_tab.shape),
            full(*carry_tab.shape),
            full(1, SSM_WIDTH),
            full(SSM_WIDTH, SSM_WIDTH),
            full(1, SSM_WIDTH),
        ],
        out_specs=pl.BlockSpec((1, S5_CHUNK, SSM_WIDTH), lambda b, i: (b, i, 0)),
        out_shape=jax.ShapeDtypeStruct((bsz, seq, SSM_WIDTH), BF16),
        scratch_shapes=[
            pltpu.VMEM((S5_CHUNK, 2 * SSM_LANES), F32),
            pltpu.VMEM((2, SUBLANES, SSM_LANES), F32),
        ],
        compiler_params=pltpu.CompilerParams(
            dimension_semantics=("parallel", "arbitrary"), vmem_limit_bytes=VMEM_LIMIT),
        name="s5",
    )(u, z, bbd, cbd, shift_tab, carry_tab, d_skip.reshape(1, SSM_WIDTH).astype(F32),
      w_glu.astype(BF16), b_glu.reshape(1, SSM_WIDTH).astype(F32))


ACC_ROWS = 2 * DIFF_HEAD_DIM + 16
POS_SPLIT = 64


def _diff_key_features(seq):
    pos = jnp.arange(seq, dtype=jnp.int32)
    hi = (pos // POS_SPLIT).astype(F32) * POS_SPLIT
    lo = (pos % POS_SPLIT).astype(F32)
    lane = jnp.arange(2 * DIFF_HEAD_DIM, dtype=jnp.int32) % DIFF_HEAD_DIM
    tabs = []
    for h in range(DIFF_HEADS):
        slope = _alibi_slope(SWA_Q_HEADS + h)
        tab = (jnp.where(lane[None, :] == 0, slope * hi[:, None], 0.0)
               + jnp.where(lane[None, :] == 1, slope * lo[:, None], 0.0))
        tabs.append(tab)
    return jnp.stack(tabs).astype(BF16)


def _diff_kernel(q_ref, k_ref, kf_ref, vt_ref, z_ref, lam_ref, gain_ref, o_ref, acc_ref, s_ref,
                 *, lam_init):
    qi = pl.program_id(1)
    dh = DIFF_HEAD_DIM
    hw = 2 * dh
    scale = dh ** -0.5
    lane_q = lax.broadcasted_iota(jnp.int32, (ATT_TQ, hw), 1)
    ones_feat = jnp.where(lane_q % dh < 2, 1.0, 0.0).astype(BF16)
    qps = []
    for h in range(DIFF_HEADS):
        q = (q_ref[0, :, h * hw:(h + 1) * hw].astype(F32) * scale).astype(BF16)
        qps.append(jnp.where(lane_q < dh, q, ones_feat))
        qps.append(jnp.where(lane_q >= dh, q, ones_feat))
    lane_k = lax.broadcasted_iota(jnp.int32, (ATT_TK, hw), 1)
    ones_row = jnp.where(lax.broadcasted_iota(jnp.int32, (ACC_ROWS - hw, ATT_TK), 0) == 0,
                         1.0, 0.0).astype(BF16)
    krow = lax.broadcasted_iota(jnp.int32, (ATT_TK, ATT_TQ), 0)
    qcol = lax.broadcasted_iota(jnp.int32, (ATT_TK, ATT_TQ), 1)
    nt = (((1,), (1,)), ((), ()))

    acc_ref[...] = jnp.zeros_like(acc_ref)

    def kv_step(j, ms, masked):
        k0 = pl.multiple_of(j * ATT_TK, ATT_TK)
        m_new = []
        for h in range(DIFF_HEADS):
            k = k_ref[0, pl.ds(k0, ATT_TK), h * hw:(h + 1) * hw]
            kf = kf_ref[h, pl.ds(k0, ATT_TK), :]
            for c, kp in ((2 * h, jnp.where(lane_k < dh, k, kf)),
                          (2 * h + 1, jnp.where(lane_k >= dh, k, kf))):
                s = lax.dot_general(kp, qps[c], nt, preferred_element_type=F32)
                if masked:
                    s = jnp.where(krow <= qcol, s, -jnp.inf)
                s_ref[c] = s
                m_new.append(jnp.maximum(ms[c], jnp.max(s, axis=0, keepdims=True)))
        for h in range(DIFF_HEADS):
            vta = jnp.concatenate([vt_ref[0, j, h * hw:(h + 1) * hw, :], ones_row], axis=0)
            for c in (2 * h, 2 * h + 1):
                alpha = jnp.exp(ms[c] - m_new[c])
                p = jnp.exp(s_ref[c] - m_new[c]).astype(BF16)
                acc_ref[c] = alpha * acc_ref[c] + jnp.dot(vta, p, preferred_element_type=F32)
        return tuple(m_new)

    neg = jnp.full((1, ATT_TQ), -jnp.inf, F32)
    ms = lax.fori_loop(0, qi, lambda j, c: kv_step(j, c, False), (neg,) * (2 * DIFF_HEADS))
    kv_step(qi, ms, True)

    lp = lam_ref[...]
    lam = (jnp.exp(jnp.sum(lp[0:1] * lp[1:2], axis=-1, keepdims=True))
           - jnp.exp(jnp.sum(lp[2:3] * lp[3:4], axis=-1, keepdims=True)) + lam_init)
    for h in range(DIFF_HEADS):
        a1 = acc_ref[2 * h]
        a2 = acc_ref[2 * h + 1]
        o_t = (a1[:hw] * (1.0 / a1[hw:hw + 1]) - lam * (a2[:hw] * (1.0 / a2[hw:hw + 1])))
        o = o_t.T
        o = o * lax.rsqrt(jnp.mean(o * o, axis=-1, keepdims=True) + RMS_EPS) * gain_ref[...]
        o = o * (1.0 - lam_init)
        z = z_ref[0, :, h * hw:(h + 1) * hw].astype(F32)
        o_ref[0, :, h * hw:(h + 1) * hw] = (o * _silu(z)).astype(BF16)


def _diff_attention(qd, kd, vd_t, zd, lam_params, subln_gain, lam_init):
    bsz, seq, _ = qd.shape
    hw = 2 * DIFF_HEAD_DIM
    n_kv = seq // ATT_TK
    tile = pl.BlockSpec((1, ATT_TQ, DIFF_WIDTH), lambda b, i: (b, i, 0))
    return pl.pallas_call(
        functools.partial(_diff_kernel, lam_init=lam_init),
        grid=(bsz, seq // ATT_TQ),
        in_specs=[
            tile,
            pl.BlockSpec((1, seq, DIFF_WIDTH), lambda b, i: (b, 0, 0)),
            pl.BlockSpec((DIFF_HEADS, seq, hw), lambda b, i: (0, 0, 0)),
            pl.BlockSpec((1, n_kv, DIFF_WIDTH, ATT_TK), lambda b, i: (b, 0, 0, 0)),
            tile,
            pl.BlockSpec((4, DIFF_HEAD_DIM), lambda b, i: (0, 0)),
            pl.BlockSpec((1, hw), lambda b, i: (0, 0)),
        ],
        out_specs=tile,
        out_shape=jax.ShapeDtypeStruct((bsz, seq, DIFF_WIDTH), BF16),
        scratch_shapes=[pltpu.VMEM((2 * DIFF_HEADS, ACC_ROWS, ATT_TQ), F32),
                        pltpu.VMEM((2 * DIFF_HEADS, ATT_TK, ATT_TQ), F32)],
        compiler_params=pltpu.CompilerParams(
            dimension_semantics=("parallel", "parallel"), vmem_limit_bytes=VMEM_LIMIT),
        name="diffattn",
    )(qd, kd, _diff_key_features(seq), vd_t, zd, lam_params, subln_gain.reshape(1, hw).astype(F32))


SWA_STEP_BLOCKS = 4
SWA_ACC_ROWS = SWA_HEAD_DIM + 16


def _swa_features():
    w = WINDOW
    lane = jnp.arange(128, dtype=jnp.int32)[None, :]
    r = jnp.arange(2 * w, dtype=F32)[:, None]
    kf = jnp.where(lane == 0, r, 0.0) + jnp.where(lane == 1, 1.0, 0.0)
    c = jnp.arange(w, dtype=F32)[:, None]
    qf = jnp.stack([jnp.where(lane == 0, _alibi_slope(h), 0.0)
                    + jnp.where(lane == 1, -_alibi_slope(h) * (w + c), 0.0)
                    for h in range(SWA_Q_HEADS)])
    return kf.astype(BF16), qf.astype(BF16)


def _swa_kernel(q_ref, kp_ref, kc_ref, vp_ref, vc_ref, z_ref, sink_ref, kf_ref, qf_ref, o_ref, s_ref):
    i = pl.program_id(1)
    w = WINDOW
    dh = SWA_HEAD_DIM
    scale = dh ** -0.5
    nt = (((1,), (1,)), ((), ()))
    lane = lax.broadcasted_iota(jnp.int32, (w, 2 * dh), 1)
    keys = jnp.concatenate([kp_ref[0], kc_ref[0]], axis=0)
    v_t = jnp.concatenate([vp_ref[0], vc_ref[0]], axis=1)
    kf = kf_ref[...]
    ones_row = jnp.where(lax.broadcasted_iota(jnp.int32, (SWA_ACC_ROWS - dh, 2 * w), 0) == 0,
                         1.0, 0.0).astype(BF16)
    r = lax.broadcasted_iota(jnp.int32, (2 * w, w), 0)
    c = lax.broadcasted_iota(jnp.int32, (2 * w, w), 1)
    band = jnp.logical_and(r > c, r <= c + w)
    first = jnp.logical_and(band, jnp.logical_or(r >= w, i > 0))

    for n in range(SWA_STEP_BLOCKS):
        q = (q_ref[0, n * w:(n + 1) * w, :].astype(F32) * scale).astype(BF16)
        qa, qb = q[:, :2 * dh], q[:, 2 * dh:]
        zero = jnp.zeros_like(qa)
        heads = (jnp.where(lane < dh, qa, zero), jnp.where(lane < dh, qb, zero),
                 jnp.where(lane >= dh, qa, zero), jnp.where(lane >= dh, qb, zero))
        qpp = jnp.concatenate([jnp.concatenate([heads[h], qf_ref[h]], axis=1)
                               for h in range(SWA_Q_HEADS)], axis=0)
        kpp = jnp.concatenate([keys[n * w:(n + 2) * w], kf], axis=1)
        s_ref[n] = lax.dot_general(kpp, qpp, nt, preferred_element_type=F32)

    for n in range(SWA_STEP_BLOCKS):
        valid = first if n == 0 else band
        outs = []
        for kh in range(SWA_KV_HEADS):
            vta = jnp.concatenate([v_t[kh * dh:(kh + 1) * dh, n * w:(n + 2) * w], ones_row], axis=0)
            es, ms = [], []
            for g in range(SWA_GROUP):
                h = SWA_GROUP * kh + g
                s = jnp.where(valid, s_ref[n, :, h * w:(h + 1) * w], -jnp.inf)
                m = jnp.maximum(jnp.max(s, axis=0, keepdims=True), sink_ref[h][:, :1])
                es.append(jnp.exp(s - m).astype(BF16))
                ms.append(m)
            acc = jnp.dot(vta, jnp.concatenate(es, axis=1), preferred_element_type=F32)
            for g in range(SWA_GROUP):
                h = SWA_GROUP * kh + g
                denom = acc[dh:dh + 1, g * w:(g + 1) * w] + jnp.exp(sink_ref[h][:, :1] - ms[g])
                outs.append(acc[:dh, g * w:(g + 1) * w] * (1.0 / denom))
        o = jnp.concatenate(outs, axis=0).T
        z = z_ref[0, n * w:(n + 1) * w, :].astype(F32)
        o_ref[0, n * w:(n + 1) * w, :] = (o * _silu(z)).astype(BF16)


def _swa(qs, ks, vs_t, zs, sinks):
    bsz, seq, _ = qs.shape
    w = WINDOW
    n = SWA_STEP_BLOCKS
    sink_rows = jnp.broadcast_to(sinks.astype(F32).reshape(SWA_Q_HEADS, 1, 1), (SWA_Q_HEADS, 1, 128))
    kf, qf = _swa_features()
    qspec = pl.BlockSpec((1, n * w, SWA_WIDTH), lambda b, i: (b, i, 0))
    prev_block = lambda i: jnp.maximum(i * n - 1, 0)
    return pl.pallas_call(
        _swa_kernel,
        grid=(bsz, seq // (n * w)),
        in_specs=[
            qspec,
            pl.BlockSpec((1, w, SWA_KV_WIDTH), lambda b, i: (b, prev_block(i), 0)),
            pl.BlockSpec((1, n * w, SWA_KV_WIDTH), lambda b, i: (b, i, 0)),
            pl.BlockSpec((1, SWA_KV_WIDTH, w), lambda b, i: (b, 0, prev_block(i))),
            pl.BlockSpec((1, SWA_KV_WIDTH, n * w), lambda b, i: (b, 0, i)),
            qspec,
            pl.BlockSpec((SWA_Q_HEADS, 1, 128), lambda b, i: (0, 0, 0)),
            pl.BlockSpec((2 * w, 128), lambda b, i: (0, 0)),
            pl.BlockSpec((SWA_Q_HEADS, w, 128), lambda b, i: (0, 0, 0)),
        ],
        out_specs=qspec,
        out_shape=jax.ShapeDtypeStruct((bsz, seq, SWA_WIDTH), BF16),
        scratch_shapes=[pltpu.VMEM((n, 2 * w, SWA_Q_HEADS * w), F32)],
        compiler_params=pltpu.CompilerParams(
            dimension_semantics=("parallel", "parallel"), vmem_limit_bytes=VMEM_LIMIT),
        name="swa",
    )(qs, ks, ks, vs_t, vs_t, zs, sink_rows, kf, qf)


def _outproj_kernel(x_ref, ys_ref, yd_ref, yw_ref, mod_ref, w_ref, fg_ref, o_ref, *, final_norm):
    a = SSM_WIDTH
    b = SSM_WIDTH + DIFF_WIDTH
    y = (jnp.dot(ys_ref[0], w_ref[:a, :], preferred_element_type=F32)
         + jnp.dot(yd_ref[0], w_ref[a:b, :], preferred_element_type=F32)
         + jnp.dot(yw_ref[0], w_ref[b:, :], preferred_element_type=F32))
    out = x_ref[0] + mod_ref[0][2:3, :] * y
    if final_norm:
        out = out * lax.rsqrt(jnp.mean(out * out, axis=-1, keepdims=True) + RMS_EPS) * fg_ref[...]
    o_ref[0] = out


def _outproj(x, y_ssm, y_diff, y_swa, mod_l, w_out_bf16, final_gain, final_norm):
    bsz, seq, _ = x.shape
    tok = lambda n: pl.BlockSpec((1, TOK_TILE, n), lambda b, i: (b, i, 0))
    return pl.pallas_call(
        functools.partial(_outproj_kernel, final_norm=final_norm),
        grid=(bsz, seq // TOK_TILE),
        in_specs=[
            tok(D_MODEL), tok(SSM_WIDTH), tok(DIFF_WIDTH), tok(SWA_WIDTH),
            pl.BlockSpec((1, 3, D_MODEL), lambda b, i: (b, 0, 0)),
            pl.BlockSpec((D_MODEL, D_MODEL), lambda b, i: (0, 0)),
            pl.BlockSpec((1, D_MODEL), lambda b, i: (0, 0)),
        ],
        out_specs=tok(D_MODEL),
        out_shape=jax.ShapeDtypeStruct((bsz, seq, D_MODEL), F32),
        compiler_params=pltpu.CompilerParams(
            dimension_semantics=("parallel", "parallel"), vmem_limit_bytes=VMEM_LIMIT),
        name="outproj",
    )(x, y_ssm, y_diff, y_swa, mod_l, w_out_bf16, final_gain.reshape(1, D_MODEL).astype(F32))


def kernel(x, c, norm_gain, ada_w, ada_b, w_in, w_out, ssm_lam_re, ssm_lam_im, ssm_log_step,
           ssm_b_re, ssm_b_im, ssm_c_re, ssm_c_im, ssm_d, glu_w, glu_b,
           diff_lq1, diff_lk1, diff_lq2, diff_lk2, diff_subln, swa_sinks, final_gain):
    bsz = x.shape[0]
    mod = _ada(c, ada_w, ada_b).reshape(DEPTH, bsz, 3, D_MODEL)
    for l in range(DEPTH):
        proj = dict(zip(PROJ_NAMES, _inproj(x, mod[l], norm_gain[l], w_in[l])))
        tables = _s5_tables(ssm_lam_re[l], ssm_lam_im[l], ssm_log_step[l],
                            ssm_b_re[l], ssm_b_im[l], ssm_c_re[l], ssm_c_im[l])
        y_ssm = _s5(proj["u"], proj["z_ssm"], tables, ssm_d[l], glu_w[l], glu_b[l])
        lam_init = 0.8 - 0.6 * math.exp(-0.3 * l)
        lam_params = jnp.stack([diff_lq1[l], diff_lk1[l], diff_lq2[l], diff_lk2[l]]).astype(F32)
        y_diff = _diff_attention(proj["qd"], proj["kd"], proj["vd"], proj["z_diff"],
                                 lam_params, diff_subln[l], lam_init)
        y_swa = _swa(proj["qs"], proj["ks"], proj["vs"], proj["z_swa"], swa_sinks[l])
        x = _outproj(x, y_ssm, y_diff, y_swa, mod[l], w_out[l].astype(BF16), final_gain,
                     final_norm=(l == DEPTH - 1))
    return x
```

```python
import functools
import math

import jax
import jax.numpy as jnp
from jax import lax
from jax.experimental import pallas as pl
from jax.experimental.pallas import tpu as pltpu

F32 = jnp.float32
BF16 = jnp.bfloat16

D_MODEL = 1024
DEPTH = 2
SSM_GROUP = 16
SSM_GROUPS = 16
SSM_WIDTH = 256
SSM_STATE = 64
SSM_LANES = SSM_GROUPS * SSM_STATE
DIFF_HEADS = 4
DIFF_HEAD_DIM = 64
DIFF_WIDTH = 512
SWA_Q_HEADS = 4
SWA_KV_HEADS = 2
SWA_GROUP = 2
SWA_HEAD_DIM = 64
SWA_WIDTH = 256
SWA_KV_WIDTH = 128
WINDOW = 128
N_ATTN_HEADS = 8
RMS_EPS = 1e-6

PROJ_NAMES = ("u", "z_ssm", "qd", "kd", "vd", "z_diff", "qs", "ks", "vs", "z_swa")
PROJ_SIZES = (SSM_WIDTH, SSM_WIDTH, DIFF_WIDTH, DIFF_WIDTH, DIFF_WIDTH, DIFF_WIDTH,
              SWA_WIDTH, SWA_KV_WIDTH, SWA_KV_WIDTH, SWA_WIDTH)
IN_COLS = sum(PROJ_SIZES)

SUBLANES = 8
VMEM_LIMIT = 48 * 1024 * 1024

TOK_TILE = 512
S5_CHUNK = 256
S5_BLOCK = 32
ATT_TQ = 256
ATT_TK = 256


def _silu(x):
    return x * jax.nn.sigmoid(x)


def _alibi_slope(head_index):
    return 2.0 ** (-(head_index + 1) * (8.0 / N_ATTN_HEADS))


def _ada_kernel(c_ref, w_ref, b_ref, o_ref):
    cond = _silu(c_ref[...])
    o_ref[0] = jnp.dot(cond, w_ref[0], preferred_element_type=F32) + b_ref[0]


def _ada(c, ada_w, ada_b):
    bsz = c.shape[0]
    col = D_MODEL
    return pl.pallas_call(
        _ada_kernel,
        grid=(DEPTH, 3),
        in_specs=[
            pl.BlockSpec((bsz, D_MODEL), lambda l, j: (0, 0)),
            pl.BlockSpec((1, D_MODEL, col), lambda l, j: (l, 0, j)),
            pl.BlockSpec((1, 1, col), lambda l, j: (l, 0, j)),
        ],
        out_specs=pl.BlockSpec((1, bsz, col), lambda l, j: (l, 0, j)),
        out_shape=jax.ShapeDtypeStruct((DEPTH, bsz, 3 * D_MODEL), F32),
        compiler_params=pltpu.CompilerParams(
            dimension_semantics=("parallel", "parallel"), vmem_limit_bytes=VMEM_LIMIT),
        name="ada",
    )(c, ada_w, ada_b.reshape(DEPTH, 1, 3 * D_MODEL))


def _inproj_kernel(x_ref, mod_ref, g_ref, w_ref, wvt_ref, *out_refs):
    xf = x_ref[0]
    y = xf * lax.rsqrt(jnp.mean(xf * xf, axis=-1, keepdims=True) + RMS_EPS) * g_ref[...]
    mod = mod_ref[0]
    h = (y * (1.0 + mod[1:2, :]) + mod[0:1, :]).astype(BF16)
    vt = lax.dot_general(wvt_ref[...], h, (((1,), (1,)), ((), ())),
                         preferred_element_type=F32).astype(BF16)
    start = 0
    for name, o_ref, n in zip(PROJ_NAMES, out_refs, PROJ_SIZES):
        if name == "vd":
            for t in range(TOK_TILE // ATT_TK):
                o_ref[0, t] = vt[:DIFF_WIDTH, t * ATT_TK:(t + 1) * ATT_TK]
        elif name == "vs":
            o_ref[0] = vt[DIFF_WIDTH:, :]
        else:
            o_ref[0] = jnp.dot(h, w_ref[:, start:start + n], preferred_element_type=F32).astype(BF16)
        start += n


def _proj_start(name):
    return sum(PROJ_SIZES[:PROJ_NAMES.index(name)])


def _inproj(x, mod_l, gain, w_in):
    bsz, seq, _ = x.shape
    grid = (bsz, seq // TOK_TILE)
    w_in_bf16 = w_in.astype(BF16)
    qs0 = _proj_start("qs")
    dh = SWA_HEAD_DIM
    order = (0, 2, 1, 3)
    qs_cols = jnp.concatenate([w_in_bf16[:, qs0 + h * dh:qs0 + (h + 1) * dh] for h in order], axis=1)
    w_in_bf16 = lax.dynamic_update_slice(w_in_bf16, qs_cols, (0, qs0))
    vd0, vs0 = _proj_start("vd"), _proj_start("vs")
    w_v_t = jnp.concatenate([w_in_bf16[:, vd0:vd0 + DIFF_WIDTH],
                             w_in_bf16[:, vs0:vs0 + SWA_KV_WIDTH]], axis=1).T
    tiles_per_step = TOK_TILE // ATT_TK
    out_specs, out_shape = [], []
    for name, n in zip(PROJ_NAMES, PROJ_SIZES):
        if name == "vd":
            out_specs.append(pl.BlockSpec((1, tiles_per_step, n, ATT_TK), lambda b, i: (b, i, 0, 0)))
            out_shape.append(jax.ShapeDtypeStruct((bsz, seq // ATT_TK, n, ATT_TK), BF16))
        elif name == "vs":
            out_specs.append(pl.BlockSpec((1, n, TOK_TILE), lambda b, i: (b, 0, i)))
            out_shape.append(jax.ShapeDtypeStruct((bsz, n, seq), BF16))
        else:
            out_specs.append(pl.BlockSpec((1, TOK_TILE, n), lambda b, i: (b, i, 0)))
            out_shape.append(jax.ShapeDtypeStruct((bsz, seq, n), BF16))
    return pl.pallas_call(
        _inproj_kernel,
        grid=grid,
        in_specs=[
            pl.BlockSpec((1, TOK_TILE, D_MODEL), lambda b, i: (b, i, 0)),
            pl.BlockSpec((1, 3, D_MODEL), lambda b, i: (b, 0, 0)),
            pl.BlockSpec((1, D_MODEL), lambda b, i: (0, 0)),
            pl.BlockSpec((D_MODEL, IN_COLS), lambda b, i: (0, 0)),
            pl.BlockSpec((DIFF_WIDTH + SWA_KV_WIDTH, D_MODEL), lambda b, i: (0, 0)),
        ],
        out_specs=out_specs,
        out_shape=out_shape,
        compiler_params=pltpu.CompilerParams(
            dimension_semantics=("parallel", "parallel"), vmem_limit_bytes=VMEM_LIMIT),
        name="inproj",
    )(x, mod_l, gain.reshape(1, D_MODEL), w_in_bf16, w_v_t)


def _s5_tables(lam_re, lam_im, log_step, b_re, b_im, c_re, c_im):
    g, p, h = SSM_GROUPS, SSM_STATE, SSM_GROUP
    step = jnp.exp(log_step.astype(F32))[:, None]
    lr = lam_re.astype(F32)
    li = lam_im.astype(F32)
    mag = jnp.exp(lr * step)
    ang = li * step
    ab_re = mag * jnp.cos(ang)
    ab_im = mag * jnp.sin(ang)
    den = lr * lr + li * li
    f_re = ((ab_re - 1.0) * lr + ab_im * li) / den
    f_im = (ab_im * lr - (ab_re - 1.0) * li) / den
    br = b_re.astype(F32)
    bi = b_im.astype(F32)
    bb_re = f_re[..., None] * br - f_im[..., None] * bi
    bb_im = f_re[..., None] * bi + f_im[..., None] * br
    eye = jnp.eye(g, dtype=F32)
    bbd_re = jnp.einsum("gph,gk->ghkp", bb_re, eye).reshape(g * h, g * p)
    bbd_im = jnp.einsum("gph,gk->ghkp", bb_im, eye).reshape(g * h, g * p)
    bbd = jnp.concatenate([bbd_re, bbd_im], axis=1).astype(BF16)
    cbd_re = jnp.einsum("ghp,gk->gpkh", c_re.astype(F32), eye).reshape(g * p, g * h)
    cbd_im = jnp.einsum("ghp,gk->gpkh", c_im.astype(F32), eye).reshape(g * p, g * h)
    cbd = jnp.concatenate([cbd_re, -cbd_im], axis=0).astype(BF16)

    def power(n):
        n = jnp.asarray(n, F32).reshape(-1, 1, 1)
        m = jnp.exp(lr * step * n)
        return ((m * jnp.cos(ang * n)).reshape(-1, g * p), (m * jnp.sin(ang * n)).reshape(-1, g * p))

    c = S5_BLOCK // 2
    rows = jnp.arange(S5_BLOCK, dtype=F32)
    pre = power(c - rows)
    post = power(rows - c)
    row_tabs = jnp.stack([pre[0], pre[1], post[0], post[1]])
    vec = [power(float(S5_BLOCK - 1 - c)),
           power(float(S5_BLOCK)),
           power(float(c + 1))]
    vec_tabs = jnp.concatenate([t for pair in vec for t in pair], axis=0)
    return bbd, cbd, row_tabs, vec_tabs


def _cmul(a_re, a_im, b_re, b_im):
    return a_re * b_re - a_im * b_im, a_re * b_im + a_im * b_re


def _s5_kernel(u_ref, z_ref, bbd_ref, cbd_ref, tri_ref, row_ref, vec_ref, d_ref, wg_ref, bg_ref,
               o_ref, xs_ref, carry_scr):
    chunk = u_ref.shape[1]
    nb = chunk // S5_BLOCK
    n = SSM_LANES

    @pl.when(pl.program_id(1) == 0)
    def _():
        carry_scr[...] = jnp.zeros_like(carry_scr)

    u = u_ref[0]
    bu = jnp.dot(u, bbd_ref[...], preferred_element_type=F32)
    bu = bu.reshape(nb, S5_BLOCK, 2 * n)
    z_re, z_im = _cmul(row_ref[0], row_ref[1], bu[:, :, :n], bu[:, :, n:])
    zs = jnp.concatenate([z_re, z_im], axis=-1).reshape(chunk, 2 * n).astype(BF16)
    xs_ref[...] = jnp.dot(tri_ref[...], zs, preferred_element_type=F32)

    ends = jnp.concatenate([xs_ref[(k + 1) * S5_BLOCK - 1:(k + 1) * S5_BLOCK, :] for k in range(nb)],
                           axis=0)
    e_re, e_im = _cmul(vec_ref[0:1], vec_ref[1:2], ends[:, :n], ends[:, n:])
    s_re = carry_scr[0:1, :]
    s_im = carry_scr[1:2, :]
    inj_re, inj_im = [], []
    for k in range(nb):
        g_re, g_im = _cmul(vec_ref[4:5], vec_ref[5:6], s_re, s_im)
        inj_re.append(g_re)
        inj_im.append(g_im)
        d_re, d_im = _cmul(vec_ref[2:3], vec_ref[3:4], s_re, s_im)
        s_re = e_re[k:k + 1] + d_re
        s_im = e_im[k:k + 1] + d_im
    carry_scr[0:1, :] = s_re
    carry_scr[1:2, :] = s_im
    inj_re = jnp.concatenate(inj_re, axis=0)[:, None, :]
    inj_im = jnp.concatenate(inj_im, axis=0)[:, None, :]

    xs = xs_ref[...].reshape(nb, S5_BLOCK, 2 * n)
    x_re, x_im = _cmul(row_ref[2], row_ref[3], xs[:, :, :n] + inj_re, xs[:, :, n:] + inj_im)
    st = jnp.concatenate([x_re, x_im], axis=-1).reshape(chunk, 2 * n).astype(BF16)

    y = jnp.dot(st, cbd_ref[...], preferred_element_type=F32)
    y = y + d_ref[...] * u.astype(F32)
    y = jax.nn.gelu(y)
    gate = jnp.dot(y.astype(BF16), wg_ref[...], preferred_element_type=F32) + bg_ref[...]
    y = y * jax.nn.sigmoid(gate)
    o_ref[0] = (y * _silu(z_ref[0].astype(F32))).astype(BF16)


def _s5(u, z, tables, d_skip, w_glu, b_glu):
    bsz, seq, _ = u.shape
    bbd, cbd, row_tabs, vec_tabs = tables
    t = jnp.arange(S5_CHUNK, dtype=jnp.int32)
    tri = jnp.logical_and(t[:, None] // S5_BLOCK == t[None, :] // S5_BLOCK,
                          t[:, None] >= t[None, :]).astype(BF16)
    full = lambda *shape: pl.BlockSpec(shape, lambda b, i: (0,) * len(shape))
    return pl.pallas_call(
        _s5_kernel,
        grid=(bsz, seq // S5_CHUNK),
        in_specs=[
            pl.BlockSpec((1, S5_CHUNK, SSM_WIDTH), lambda b, i: (b, i, 0)),
            pl.BlockSpec((1, S5_CHUNK, SSM_WIDTH), lambda b, i: (b, i, 0)),
            full(SSM_WIDTH, 2 * SSM_LANES),
            full(2 * SSM_LANES, SSM_WIDTH),
            full(S5_CHUNK, S5_CHUNK),
            full(*row_tabs.shape),
            full(*vec_tabs.shape),
            full(1, SSM_WIDTH),
            full(SSM_WIDTH, SSM_WIDTH),
            full(1, SSM_WIDTH),
        ],
        out_specs=pl.BlockSpec((1, S5_CHUNK, SSM_WIDTH), lambda b, i: (b, i, 0)),
        out_shape=jax.ShapeDtypeStruct((bsz, seq, SSM_WIDTH), BF16),
        scratch_shapes=[
            pltpu.VMEM((S5_CHUNK, 2 * SSM_LANES), F32),
            pltpu.VMEM((SUBLANES, SSM_LANES), F32),
        ],
        compiler_params=pltpu.CompilerParams(
            dimension_semantics=("parallel", "arbitrary"), vmem_limit_bytes=VMEM_LIMIT),
        name="s5",
    )(u, z, bbd, cbd, tri, row_tabs, vec_tabs, d_skip.reshape(1, SSM_WIDTH).astype(F32),
      w_glu.astype(BF16), b_glu.reshape(1, SSM_WIDTH).astype(F32))


ACC_ROWS = 2 * DIFF_HEAD_DIM + 16
POS_SPLIT = 64


def _diff_key_features(seq):
    pos = jnp.arange(seq, dtype=jnp.int32)
    hi = (pos // POS_SPLIT).astype(F32) * POS_SPLIT
    lo = (pos % POS_SPLIT).astype(F32)
    lane = jnp.arange(2 * DIFF_HEAD_DIM, dtype=jnp.int32) % DIFF_HEAD_DIM
    tabs = []
    for h in range(DIFF_HEADS):
        slope = _alibi_slope(SWA_Q_HEADS + h)
        tab = (jnp.where(lane[None, :] == 0, slope * hi[:, None], 0.0)
               + jnp.where(lane[None, :] == 1, slope * lo[:, None], 0.0))
        tabs.append(tab)
    return jnp.stack(tabs).astype(BF16)


def _diff_kernel(q_ref, k_ref, kf_ref, vt_ref, z_ref, lam_ref, gain_ref, o_ref, acc_ref, s_ref,
                 *, lam_init):
    qi = pl.program_id(1)
    dh = DIFF_HEAD_DIM
    hw = 2 * dh
    scale = dh ** -0.5
    lane_q = lax.broadcasted_iota(jnp.int32, (ATT_TQ, hw), 1)
    ones_feat = jnp.where(lane_q % dh < 2, 1.0, 0.0).astype(BF16)
    qps = []
    for h in range(DIFF_HEADS):
        q = (q_ref[0, :, h * hw:(h + 1) * hw].astype(F32) * scale).astype(BF16)
        qps.append(jnp.where(lane_q < dh, q, ones_feat))
        qps.append(jnp.where(lane_q >= dh, q, ones_feat))
    lane_k = lax.broadcasted_iota(jnp.int32, (ATT_TK, hw), 1)
    ones_row = jnp.where(lax.broadcasted_iota(jnp.int32, (ACC_ROWS - hw, ATT_TK), 0) == 0,
                         1.0, 0.0).astype(BF16)
    krow = lax.broadcasted_iota(jnp.int32, (ATT_TK, ATT_TQ), 0)
    qcol = lax.broadcasted_iota(jnp.int32, (ATT_TK, ATT_TQ), 1)
    nt = (((1,), (1,)), ((), ()))

    acc_ref[...] = jnp.zeros_like(acc_ref)

    def kv_step(j, ms, masked):
        k0 = pl.multiple_of(j * ATT_TK, ATT_TK)
        m_new = []
        for h in range(DIFF_HEADS):
            k = k_ref[0, pl.ds(k0, ATT_TK), h * hw:(h + 1) * hw]
            kf = kf_ref[h, pl.ds(k0, ATT_TK), :]
            for c, kp in ((2 * h, jnp.where(lane_k < dh, k, kf)),
                          (2 * h + 1, jnp.where(lane_k >= dh, k, kf))):
                s = lax.dot_general(kp, qps[c], nt, preferred_element_type=F32)
                if masked:
                    s = jnp.where(krow <= qcol, s, -jnp.inf)
                s_ref[c] = s
                m_new.append(jnp.maximum(ms[c], jnp.max(s, axis=0, keepdims=True)))
        for h in range(DIFF_HEADS):
            vta = jnp.concatenate([vt_ref[0, j, h * hw:(h + 1) * hw, :], ones_row], axis=0)
            for c in (2 * h, 2 * h + 1):
                alpha = jnp.exp(ms[c] - m_new[c])
                p = jnp.exp(s_ref[c] - m_new[c]).astype(BF16)
                acc_ref[c] = alpha * acc_ref[c] + jnp.dot(vta, p, preferred_element_type=F32)
        return tuple(m_new)

    neg = jnp.full((1, ATT_TQ), -jnp.inf, F32)
    ms = lax.fori_loop(0, qi, lambda j, c: kv_step(j, c, False), (neg,) * (2 * DIFF_HEADS))
    kv_step(qi, ms, True)

    lp = lam_ref[...]
    lam = (jnp.exp(jnp.sum(lp[0:1] * lp[1:2], axis=-1, keepdims=True))
           - jnp.exp(jnp.sum(lp[2:3] * lp[3:4], axis=-1, keepdims=True)) + lam_init)
    for h in range(DIFF_HEADS):
        a1 = acc_ref[2 * h]
        a2 = acc_ref[2 * h + 1]
        o_t = (a1[:hw] * (1.0 / a1[hw:hw + 1]) - lam * (a2[:hw] * (1.0 / a2[hw:hw + 1])))
        o = o_t.T
        o = o * lax.rsqrt(jnp.mean(o * o, axis=-1, keepdims=True) + RMS_EPS) * gain_ref[...]
        o = o * (1.0 - lam_init)
        z = z_ref[0, :, h * hw:(h + 1) * hw].astype(F32)
        o_ref[0, :, h * hw:(h + 1) * hw] = (o * _silu(z)).astype(BF16)


def _diff_attention(qd, kd, vd_t, zd, lam_params, subln_gain, lam_init):
    bsz, seq, _ = qd.shape
    hw = 2 * DIFF_HEAD_DIM
    n_kv = seq // ATT_TK
    tile = pl.BlockSpec((1, ATT_TQ, DIFF_WIDTH), lambda b, i: (b, i, 0))
    return pl.pallas_call(
        functools.partial(_diff_kernel, lam_init=lam_init),
        grid=(bsz, seq // ATT_TQ),
        in_specs=[
            tile,
            pl.BlockSpec((1, seq, DIFF_WIDTH), lambda b, i: (b, 0, 0)),
            pl.BlockSpec((DIFF_HEADS, seq, hw), lambda b, i: (0, 0, 0)),
            pl.BlockSpec((1, n_kv, DIFF_WIDTH, ATT_TK), lambda b, i: (b, 0, 0, 0)),
            tile,
            pl.BlockSpec((4, DIFF_HEAD_DIM), lambda b, i: (0, 0)),
            pl.BlockSpec((1, hw), lambda b, i: (0, 0)),
        ],
        out_specs=tile,
        out_shape=jax.ShapeDtypeStruct((bsz, seq, DIFF_WIDTH), BF16),
        scratch_shapes=[pltpu.VMEM((2 * DIFF_HEADS, ACC_ROWS, ATT_TQ), F32),
                        pltpu.VMEM((2 * DIFF_HEADS, ATT_TK, ATT_TQ), F32)],
        compiler_params=pltpu.CompilerParams(
            dimension_semantics=("parallel", "parallel"), vmem_limit_bytes=VMEM_LIMIT),
        name="diffattn",
    )(qd, kd, _diff_key_features(seq), vd_t, zd, lam_params, subln_gain.reshape(1, hw).astype(F32))


SWA_STEP_BLOCKS = 4
SWA_ACC_ROWS = SWA_HEAD_DIM + 16


def _swa_features():
    w = WINDOW
    lane = jnp.arange(128, dtype=jnp.int32)[None, :]
    r = jnp.arange(2 * w, dtype=F32)[:, None]
    kf = jnp.where(lane == 0, r, 0.0) + jnp.where(lane == 1, 1.0, 0.0)
    c = jnp.arange(w, dtype=F32)[:, None]
    qf = jnp.stack([jnp.where(lane == 0, _alibi_slope(h), 0.0)
                    + jnp.where(lane == 1, -_alibi_slope(h) * (w + c), 0.0)
                    for h in range(SWA_Q_HEADS)])
    return kf.astype(BF16), qf.astype(BF16)


def _swa_kernel(q_ref, kp_ref, kc_ref, vp_ref, vc_ref, z_ref, sink_ref, kf_ref, qf_ref, o_ref, s_ref):
    i = pl.program_id(1)
    w = WINDOW
    dh = SWA_HEAD_DIM
    scale = dh ** -0.5
    nt = (((1,), (1,)), ((), ()))
    lane = lax.broadcasted_iota(jnp.int32, (w, 2 * dh), 1)
    keys = jnp.concatenate([kp_ref[0], kc_ref[0]], axis=0)
    v_t = jnp.concatenate([vp_ref[0], vc_ref[0]], axis=1)
    kf = kf_ref[...]
    ones_row = jnp.where(lax.broadcasted_iota(jnp.int32, (SWA_ACC_ROWS - dh, 2 * w), 0) == 0,
                         1.0, 0.0).astype(BF16)
    r = lax.broadcasted_iota(jnp.int32, (2 * w, w), 0)
    c = lax.broadcasted_iota(jnp.int32, (2 * w, w), 1)
    band = jnp.logical_and(r > c, r <= c + w)
    first = jnp.logical_and(band, jnp.logical_or(r >= w, i > 0))

    for n in range(SWA_STEP_BLOCKS):
        q = (q_ref[0, n * w:(n + 1) * w, :].astype(F32) * scale).astype(BF16)
        qa, qb = q[:, :2 * dh], q[:, 2 * dh:]
        zero = jnp.zeros_like(qa)
        heads = (jnp.where(lane < dh, qa, zero), jnp.where(lane < dh, qb, zero),
                 jnp.where(lane >= dh, qa, zero), jnp.where(lane >= dh, qb, zero))
        qpp = jnp.concatenate([jnp.concatenate([heads[h], qf_ref[h]], axis=1)
                               for h in range(SWA_Q_HEADS)], axis=0)
        kpp = jnp.concatenate([keys[n * w:(n + 2) * w], kf], axis=1)
        s_ref[n] = lax.dot_general(kpp, qpp, nt, preferred_element_type=F32)

    for n in range(SWA_STEP_BLOCKS):
        valid = first if n == 0 else band
        outs = []
        for kh in range(SWA_KV_HEADS):
            vta = jnp.concatenate([v_t[kh * dh:(kh + 1) * dh, n * w:(n + 2) * w], ones_row], axis=0)
            es, ms = [], []
            for g in range(SWA_GROUP):
                h = SWA_GROUP * kh + g
                s = jnp.where(valid, s_ref[n, :, h * w:(h + 1) * w], -jnp.inf)
                m = jnp.maximum(jnp.max(s, axis=0, keepdims=True), sink_ref[h][:, :1])
                es.append(jnp.exp(s - m).astype(BF16))
                ms.append(m)
            acc = jnp.dot(vta, jnp.concatenate(es, axis=1), preferred_element_type=F32)
            for g in range(SWA_GROUP):
                h = SWA_GROUP * kh + g
                denom = acc[dh:dh + 1, g * w:(g + 1) * w] + jnp.exp(sink_ref[h][:, :1] - ms[g])
                outs.append(acc[:dh, g * w:(g + 1) * w] * (1.0 / denom))
        o = jnp.concatenate(outs, axis=0).T
        z = z_ref[0, n * w:(n + 1) * w, :].astype(F32)
        o_ref[0, n * w:(n + 1) * w, :] = (o * _silu(z)).astype(BF16)


def _swa(qs, ks, vs_t, zs, sinks):
    bsz, seq, _ = qs.shape
    w = WINDOW
    n = SWA_STEP_BLOCKS
    sink_rows = jnp.broadcast_to(sinks.astype(F32).reshape(SWA_Q_HEADS, 1, 1), (SWA_Q_HEADS, 1, 128))
    kf, qf = _swa_features()
    qspec = pl.BlockSpec((1, n * w, SWA_WIDTH), lambda b, i: (b, i, 0))
    prev_block = lambda i: jnp.maximum(i * n - 1, 0)
    return pl.pallas_call(
        _swa_kernel,
        grid=(bsz, seq // (n * w)),
        in_specs=[
            qspec,
            pl.BlockSpec((1, w, SWA_KV_WIDTH), lambda b, i: (b, prev_block(i), 0)),
            pl.BlockSpec((1, n * w, SWA_KV_WIDTH), lambda b, i: (b, i, 0)),
            pl.BlockSpec((1, SWA_KV_WIDTH, w), lambda b, i: (b, 0, prev_block(i))),
            pl.BlockSpec((1, SWA_KV_WIDTH, n * w), lambda b, i: (b, 0, i)),
            qspec,
            pl.BlockSpec((SWA_Q_HEADS, 1, 128), lambda b, i: (0, 0, 0)),
            pl.BlockSpec((2 * w, 128), lambda b, i: (0, 0)),
            pl.BlockSpec((SWA_Q_HEADS, w, 128), lambda b, i: (0, 0, 0)),
        ],
        out_specs=qspec,
        out_shape=jax.ShapeDtypeStruct((bsz, seq, SWA_WIDTH), BF16),
        scratch_shapes=[pltpu.VMEM((n, 2 * w, SWA_Q_HEADS * w), F32)],
        compiler_params=pltpu.CompilerParams(
            dimension_semantics=("parallel", "parallel"), vmem_limit_bytes=VMEM_LIMIT),
        name="swa",
    )(qs, ks, ks, vs_t, vs_t, zs, sink_rows, kf, qf)


def _outproj_kernel(x_ref, ys_ref, yd_ref, yw_ref, mod_ref, w_ref, fg_ref, o_ref, *, final_norm):
    a = SSM_WIDTH
    b = SSM_WIDTH + DIFF_WIDTH
    y = (jnp.dot(ys_ref[0], w_ref[:a, :], preferred_element_type=F32)
         + jnp.dot(yd_ref[0], w_ref[a:b, :], preferred_element_type=F32)
         + jnp.dot(yw_ref[0], w_ref[b:, :], preferred_element_type=F32))
    out = x_ref[0] + mod_ref[0][2:3, :] * y
    if final_norm:
        out = out * lax.rsqrt(jnp.mean(out * out, axis=-1, keepdims=True) + RMS_EPS) * fg_ref[...]
    o_ref[0] = out


def _outproj(x, y_ssm, y_diff, y_swa, mod_l, w_out_bf16, final_gain, final_norm):
    bsz, seq, _ = x.shape
    tok = lambda n: pl.BlockSpec((1, TOK_TILE, n), lambda b, i: (b, i, 0))
    return pl.pallas_call(
        functools.partial(_outproj_kernel, final_norm=final_norm),
        grid=(bsz, seq // TOK_TILE),
        in_specs=[
            tok(D_MODEL), tok(SSM_WIDTH), tok(DIFF_WIDTH), tok(SWA_WIDTH),
            pl.BlockSpec((1, 3, D_MODEL), lambda b, i: (b, 0, 0)),
            pl.BlockSpec((D_MODEL, D_MODEL), lambda b, i: (0, 0)),
            pl.BlockSpec((1, D_MODEL), lambda b, i: (0, 0)),
        ],
        out_specs=tok(D_MODEL),
        out_shape=jax.ShapeDtypeStruct((bsz, seq, D_MODEL), F32),
        compiler_params=pltpu.CompilerParams(
            dimension_semantics=("parallel", "parallel"), vmem_limit_bytes=VMEM_LIMIT),
        name="outproj",
    )(x, y_ssm, y_diff, y_swa, mod_l, w_out_bf16, final_gain.reshape(1, D_MODEL).astype(F32))


def kernel(x, c, norm_gain, ada_w, ada_b, w_in, w_out, ssm_lam_re, ssm_lam_im, ssm_log_step,
           ssm_b_re, ssm_b_im, ssm_c_re, ssm_c_im, ssm_d, glu_w, glu_b,
           diff_lq1, diff_lk1, diff_lq2, diff_lk2, diff_subln, swa_sinks, final_gain):
    bsz = x.shape[0]
    mod = _ada(c, ada_w, ada_b).reshape(DEPTH, bsz, 3, D_MODEL)
    for l in range(DEPTH):
        proj = dict(zip(PROJ_NAMES, _inproj(x, mod[l], norm_gain[l], w_in[l])))
        tables = _s5_tables(ssm_lam_re[l], ssm_lam_im[l], ssm_log_step[l],
                            ssm_b_re[l], ssm_b_im[l], ssm_c_re[l], ssm_c_im[l])
        y_ssm = _s5(proj["u"], proj["z_ssm"], tables, ssm_d[l], glu_w[l], glu_b[l])
        lam_init = 0.8 - 0.6 * math.exp(-0.3 * l)
        lam_params = jnp.stack([diff_lq1[l], diff_lk1[l], diff_lq2[l], diff_lk2[l]]).astype(F32)
        y_diff = _diff_attention(proj["qd"], proj["kd"], proj["vd"], proj["z_diff"],
                                 lam_params, diff_subln[l], lam_init)
        y_swa = _swa(proj["qs"], proj["ks"], proj["vs"], proj["z_swa"], swa_sinks[l])
        x = _outproj(x, y_ssm, y_diff, y_swa, mod[l], w_out[l].astype(BF16), final_gain,
                     final_norm=(l == DEPTH - 1))
    return x
```

```python
import functools
import math

import jax
import jax.numpy as jnp
from jax import lax
from jax.experimental import pallas as pl
from jax.experimental.pallas import tpu as pltpu

F32 = jnp.float32
BF16 = jnp.bfloat16

D_MODEL = 1024
DEPTH = 2
SSM_GROUP = 16
SSM_GROUPS = 16
SSM_WIDTH = 256
SSM_STATE = 64
SSM_LANES = SSM_GROUPS * SSM_STATE
DIFF_HEADS = 4
DIFF_HEAD_DIM = 64
DIFF_WIDTH = 512
SWA_Q_HEADS = 4
SWA_KV_HEADS = 2
SWA_GROUP = 2
SWA_HEAD_DIM = 64
SWA_WIDTH = 256
SWA_KV_WIDTH = 128
WINDOW = 128
N_ATTN_HEADS = 8
RMS_EPS = 1e-6

PROJ_NAMES = ("u", "z_ssm", "qd", "kd", "vd", "z_diff", "qs", "ks", "vs", "z_swa")
PROJ_SIZES = (SSM_WIDTH, SSM_WIDTH, DIFF_WIDTH, DIFF_WIDTH, DIFF_WIDTH, DIFF_WIDTH,
              SWA_WIDTH, SWA_KV_WIDTH, SWA_KV_WIDTH, SWA_WIDTH)
IN_COLS = sum(PROJ_SIZES)

SUBLANES = 8
VMEM_LIMIT = 48 * 1024 * 1024

TOK_TILE = 512
S5_CHUNK = 256
S5_BLOCK = 32
S5_STREAMS = 2
ATT_TQ = 256
ATT_TK = 256
DIFF_STREAMS = 2


def _silu(x):
    return x * jax.nn.sigmoid(x)


def _alibi_slope(head_index):
    return 2.0 ** (-(head_index + 1) * (8.0 / N_ATTN_HEADS))


def _ada_kernel(c_ref, w_ref, b_ref, o_ref):
    cond = _silu(c_ref[...])
    o_ref[0] = jnp.dot(cond, w_ref[0], preferred_element_type=F32) + b_ref[0]


def _ada(c, ada_w, ada_b):
    bsz = c.shape[0]
    col = D_MODEL
    return pl.pallas_call(
        _ada_kernel,
        grid=(DEPTH, 3),
        in_specs=[
            pl.BlockSpec((bsz, D_MODEL), lambda l, j: (0, 0)),
            pl.BlockSpec((1, D_MODEL, col), lambda l, j: (l, 0, j)),
            pl.BlockSpec((1, 1, col), lambda l, j: (l, 0, j)),
        ],
        out_specs=pl.BlockSpec((1, bsz, col), lambda l, j: (l, 0, j)),
        out_shape=jax.ShapeDtypeStruct((DEPTH, bsz, 3 * D_MODEL), F32),
        compiler_params=pltpu.CompilerParams(
            dimension_semantics=("parallel", "parallel"), vmem_limit_bytes=VMEM_LIMIT),
        name="ada",
    )(c, ada_w, ada_b.reshape(DEPTH, 1, 3 * D_MODEL))


def _inproj_kernel(x_ref, mod_ref, g_ref, w_ref, wvt_ref, *out_refs):
    xf = x_ref[0]
    y = xf * lax.rsqrt(jnp.mean(xf * xf, axis=-1, keepdims=True) + RMS_EPS) * g_ref[...]
    mod = mod_ref[0]
    h = (y * (1.0 + mod[1:2, :]) + mod[0:1, :]).astype(BF16)
    vt = lax.dot_general(wvt_ref[...], h, (((1,), (1,)), ((), ())),
                         preferred_element_type=F32).astype(BF16)
    start = 0
    for name, o_ref, n in zip(PROJ_NAMES, out_refs, PROJ_SIZES):
        if name == "vd":
            for t in range(TOK_TILE // ATT_TK):
                o_ref[0, t] = vt[:DIFF_WIDTH, t * ATT_TK:(t + 1) * ATT_TK]
        elif name == "vs":
            o_ref[0] = vt[DIFF_WIDTH:, :]
        else:
            o_ref[0] = jnp.dot(h, w_ref[:, start:start + n], preferred_element_type=F32).astype(BF16)
        start += n


def _proj_start(name):
    return sum(PROJ_SIZES[:PROJ_NAMES.index(name)])


def _inproj(x, mod_l, gain, w_in):
    bsz, seq, _ = x.shape
    grid = (bsz, seq // TOK_TILE)
    w_in_bf16 = w_in.astype(BF16)
    qs0 = _proj_start("qs")
    dh = SWA_HEAD_DIM
    order = (0, 2, 1, 3)
    qs_cols = jnp.concatenate([w_in_bf16[:, qs0 + h * dh:qs0 + (h + 1) * dh] for h in order], axis=1)
    w_in_bf16 = lax.dynamic_update_slice(w_in_bf16, qs_cols, (0, qs0))
    vd0, vs0 = _proj_start("vd"), _proj_start("vs")
    w_v_t = jnp.concatenate([w_in_bf16[:, vd0:vd0 + DIFF_WIDTH],
                             w_in_bf16[:, vs0:vs0 + SWA_KV_WIDTH]], axis=1).T
    tiles_per_step = TOK_TILE // ATT_TK
    out_specs, out_shape = [], []
    for name, n in zip(PROJ_NAMES, PROJ_SIZES):
        if name == "vd":
            out_specs.append(pl.BlockSpec((1, tiles_per_step, n, ATT_TK), lambda b, i: (b, i, 0, 0)))
            out_shape.append(jax.ShapeDtypeStruct((bsz, seq // ATT_TK, n, ATT_TK), BF16))
        elif name == "vs":
            out_specs.append(pl.BlockSpec((1, n, TOK_TILE), lambda b, i: (b, 0, i)))
            out_shape.append(jax.ShapeDtypeStruct((bsz, n, seq), BF16))
        else:
            out_specs.append(pl.BlockSpec((1, TOK_TILE, n), lambda b, i: (b, i, 0)))
            out_shape.append(jax.ShapeDtypeStruct((bsz, seq, n), BF16))
    return pl.pallas_call(
        _inproj_kernel,
        grid=grid,
        in_specs=[
            pl.BlockSpec((1, TOK_TILE, D_MODEL), lambda b, i: (b, i, 0)),
            pl.BlockSpec((1, 3, D_MODEL), lambda b, i: (b, 0, 0)),
            pl.BlockSpec((1, D_MODEL), lambda b, i: (0, 0)),
            pl.BlockSpec((D_MODEL, IN_COLS), lambda b, i: (0, 0)),
            pl.BlockSpec((DIFF_WIDTH + SWA_KV_WIDTH, D_MODEL), lambda b, i: (0, 0)),
        ],
        out_specs=out_specs,
        out_shape=out_shape,
        compiler_params=pltpu.CompilerParams(
            dimension_semantics=("parallel", "parallel"), vmem_limit_bytes=VMEM_LIMIT),
        name="inproj",
    )(x, mod_l, gain.reshape(1, D_MODEL), w_in_bf16, w_v_t)


def _s5_tables(lam_re, lam_im, log_step, b_re, b_im, c_re, c_im):
    g, p, h = SSM_GROUPS, SSM_STATE, SSM_GROUP
    step = jnp.exp(log_step.astype(F32))[:, None]
    lr = lam_re.astype(F32)
    li = lam_im.astype(F32)
    mag = jnp.exp(lr * step)
    ang = li * step
    ab_re = mag * jnp.cos(ang)
    ab_im = mag * jnp.sin(ang)
    den = lr * lr + li * li
    f_re = ((ab_re - 1.0) * lr + ab_im * li) / den
    f_im = (ab_im * lr - (ab_re - 1.0) * li) / den
    br = b_re.astype(F32)
    bi = b_im.astype(F32)
    bb_re = f_re[..., None] * br - f_im[..., None] * bi
    bb_im = f_re[..., None] * bi + f_im[..., None] * br
    eye = jnp.eye(g, dtype=F32)
    bbd_re = jnp.einsum("gph,gk->ghkp", bb_re, eye).reshape(g * h, g * p)
    bbd_im = jnp.einsum("gph,gk->ghkp", bb_im, eye).reshape(g * h, g * p)
    bbd = jnp.concatenate([bbd_re, bbd_im], axis=1).astype(BF16)
    cbd_re = jnp.einsum("ghp,gk->gpkh", c_re.astype(F32), eye).reshape(g * p, g * h)
    cbd_im = jnp.einsum("ghp,gk->gpkh", c_im.astype(F32), eye).reshape(g * p, g * h)
    cbd = jnp.concatenate([cbd_re, -cbd_im], axis=0).astype(BF16)

    def power(n):
        n = jnp.asarray(n, F32).reshape(-1, 1, 1)
        m = jnp.exp(lr * step * n)
        return ((m * jnp.cos(ang * n)).reshape(-1, g * p), (m * jnp.sin(ang * n)).reshape(-1, g * p))

    c = S5_BLOCK // 2
    rows = jnp.arange(S5_BLOCK, dtype=F32)
    pre = power(c - rows)
    post = power(rows - c)
    row_tabs = jnp.stack([pre[0], pre[1], post[0], post[1]])
    vec = [power(float(S5_BLOCK - 1 - c)),
           power(float(S5_BLOCK)),
           power(float(c + 1))]
    vec_tabs = jnp.concatenate([t for pair in vec for t in pair], axis=0)
    return bbd, cbd, row_tabs, vec_tabs


def _cmul(a_re, a_im, b_re, b_im):
    return a_re * b_re - a_im * b_im, a_re * b_im + a_im * b_re


def _s5_kernel(u_ref, z_ref, bbd_ref, cbd_ref, tri_ref, row_ref, vec_ref, d_ref, wg_ref, bg_ref,
               o_ref, xs_ref, carry_scr):
    chunk = u_ref.shape[1]
    nb = chunk // S5_BLOCK
    n = SSM_LANES

    @pl.when(pl.program_id(1) == 0)
    def _():
        carry_scr[...] = jnp.zeros_like(carry_scr)

    streams = range(u_ref.shape[0])
    us = [u_ref[s] for s in streams]
    zss = []
    for s in streams:
        bu = jnp.dot(us[s], bbd_ref[...], preferred_element_type=F32)
        bu = bu.reshape(nb, S5_BLOCK, 2 * n)
        z_re, z_im = _cmul(row_ref[0], row_ref[1], bu[:, :, :n], bu[:, :, n:])
        zss.append(jnp.concatenate([z_re, z_im], axis=-1).reshape(chunk, 2 * n).astype(BF16))
    for s in streams:
        xs_ref[s] = jnp.dot(tri_ref[...], zss[s], preferred_element_type=F32)

    sts = []
    for s in streams:
        ends = jnp.concatenate([xs_ref[s, (k + 1) * S5_BLOCK - 1:(k + 1) * S5_BLOCK, :]
                                for k in range(nb)], axis=0)
        e_re, e_im = _cmul(vec_ref[0:1], vec_ref[1:2], ends[:, :n], ends[:, n:])
        s_re = carry_scr[s, 0:1, :]
        s_im = carry_scr[s, 1:2, :]
        inj_re, inj_im = [], []
        for k in range(nb):
            g_re, g_im = _cmul(vec_ref[4:5], vec_ref[5:6], s_re, s_im)
            inj_re.append(g_re)
            inj_im.append(g_im)
            d_re, d_im = _cmul(vec_ref[2:3], vec_ref[3:4], s_re, s_im)
            s_re = e_re[k:k + 1] + d_re
            s_im = e_im[k:k + 1] + d_im
        carry_scr[s, 0:1, :] = s_re
        carry_scr[s, 1:2, :] = s_im
        inj_re = jnp.concatenate(inj_re, axis=0)[:, None, :]
        inj_im = jnp.concatenate(inj_im, axis=0)[:, None, :]
        xs = xs_ref[s].reshape(nb, S5_BLOCK, 2 * n)
        x_re, x_im = _cmul(row_ref[2], row_ref[3], xs[:, :, :n] + inj_re, xs[:, :, n:] + inj_im)
        sts.append(jnp.concatenate([x_re, x_im], axis=-1).reshape(chunk, 2 * n).astype(BF16))

    ys = []
    for s in streams:
        y = jnp.dot(sts[s], cbd_ref[...], preferred_element_type=F32)
        y = y + d_ref[...] * us[s].astype(F32)
        ys.append(jax.nn.gelu(y))
    for s in streams:
        gate = jnp.dot(ys[s].astype(BF16), wg_ref[...], preferred_element_type=F32) + bg_ref[...]
        y = ys[s] * jax.nn.sigmoid(gate)
        o_ref[s] = (y * _silu(z_ref[s].astype(F32))).astype(BF16)


def _s5(u, z, tables, d_skip, w_glu, b_glu):
    bsz, seq, _ = u.shape
    bbd, cbd, row_tabs, vec_tabs = tables
    t = jnp.arange(S5_CHUNK, dtype=jnp.int32)
    tri = jnp.logical_and(t[:, None] // S5_BLOCK == t[None, :] // S5_BLOCK,
                          t[:, None] >= t[None, :]).astype(BF16)
    full = lambda *shape: pl.BlockSpec(shape, lambda b, i: (0,) * len(shape))
    return pl.pallas_call(
        _s5_kernel,
        grid=(bsz // S5_STREAMS, seq // S5_CHUNK),
        in_specs=[
            pl.BlockSpec((S5_STREAMS, S5_CHUNK, SSM_WIDTH), lambda b, i: (b, i, 0)),
            pl.BlockSpec((S5_STREAMS, S5_CHUNK, SSM_WIDTH), lambda b, i: (b, i, 0)),
            full(SSM_WIDTH, 2 * SSM_LANES),
            full(2 * SSM_LANES, SSM_WIDTH),
            full(S5_CHUNK, S5_CHUNK),
            full(*row_tabs.shape),
            full(*vec_tabs.shape),
            full(1, SSM_WIDTH),
            full(SSM_WIDTH, SSM_WIDTH),
            full(1, SSM_WIDTH),
        ],
        out_specs=pl.BlockSpec((S5_STREAMS, S5_CHUNK, SSM_WIDTH), lambda b, i: (b, i, 0)),
        out_shape=jax.ShapeDtypeStruct((bsz, seq, SSM_WIDTH), BF16),
        scratch_shapes=[
            pltpu.VMEM((S5_STREAMS, S5_CHUNK, 2 * SSM_LANES), F32),
            pltpu.VMEM((S5_STREAMS, SUBLANES, SSM_LANES), F32),
        ],
        compiler_params=pltpu.CompilerParams(
            dimension_semantics=("parallel", "arbitrary"), vmem_limit_bytes=VMEM_LIMIT),
        name="s5",
    )(u, z, bbd, cbd, tri, row_tabs, vec_tabs, d_skip.reshape(1, SSM_WIDTH).astype(F32),
      w_glu.astype(BF16), b_glu.reshape(1, SSM_WIDTH).astype(F32))


ACC_ROWS = 2 * DIFF_HEAD_DIM + 16
POS_SPLIT = 64


def _diff_key_features(seq):
    pos = jnp.arange(seq, dtype=jnp.int32)
    hi = (pos // POS_SPLIT).astype(F32) * POS_SPLIT
    lo = (pos % POS_SPLIT).astype(F32)
    lane = jnp.arange(2 * DIFF_HEAD_DIM, dtype=jnp.int32) % DIFF_HEAD_DIM
    tabs = []
    for h in range(DIFF_HEADS):
        slope = _alibi_slope(SWA_Q_HEADS + h)
        tab = (jnp.where(lane[None, :] == 0, slope * hi[:, None], 0.0)
               + jnp.where(lane[None, :] == 1, slope * lo[:, None], 0.0))
        tabs.append(tab)
    return jnp.stack(tabs).astype(BF16)


def _diff_kernel(q_ref, k_ref, kf_ref, vt_ref, z_ref, lam_ref, gain_ref, o_ref, acc_ref, s_ref,
                 *, lam_init):
    qi = pl.program_id(1)
    dh = DIFF_HEAD_DIM
    hw = 2 * dh
    scale = dh ** -0.5
    lane_q = lax.broadcasted_iota(jnp.int32, (ATT_TQ, hw), 1)
    ones_feat = jnp.where(lane_q % dh < 2, 1.0, 0.0).astype(BF16)
    units = [(b, h) for b in range(q_ref.shape[0]) for h in range(DIFF_HEADS)]
    qps = []
    for b, h in units:
        q = (q_ref[b, :, h * hw:(h + 1) * hw].astype(F32) * scale).astype(BF16)
        qps.append(jnp.where(lane_q < dh, q, ones_feat))
        qps.append(jnp.where(lane_q >= dh, q, ones_feat))
    lane_k = lax.broadcasted_iota(jnp.int32, (ATT_TK, hw), 1)
    ones_row = jnp.where(lax.broadcasted_iota(jnp.int32, (ACC_ROWS - hw, ATT_TK), 0) == 0,
                         1.0, 0.0).astype(BF16)
    krow = lax.broadcasted_iota(jnp.int32, (ATT_TK, ATT_TQ), 0)
    qcol = lax.broadcasted_iota(jnp.int32, (ATT_TK, ATT_TQ), 1)
    nt = (((1,), (1,)), ((), ()))

    acc_ref[...] = jnp.zeros_like(acc_ref)

    def kv_step(j, ms, masked):
        k0 = pl.multiple_of(j * ATT_TK, ATT_TK)
        m_new = []
        for u, (b, h) in enumerate(units):
            k = k_ref[b, pl.ds(k0, ATT_TK), h * hw:(h + 1) * hw]
            kf = kf_ref[h, pl.ds(k0, ATT_TK), :]
            for c, kp in ((2 * u, jnp.where(lane_k < dh, k, kf)),
                          (2 * u + 1, jnp.where(lane_k >= dh, k, kf))):
                s = lax.dot_general(kp, qps[c], nt, preferred_element_type=F32)
                if masked:
                    s = jnp.where(krow <= qcol, s, -jnp.inf)
                s_ref[c] = s
                m_new.append(jnp.maximum(ms[c], jnp.max(s, axis=0, keepdims=True)))
        for u, (b, h) in enumerate(units):
            vta = jnp.concatenate([vt_ref[b, j, h * hw:(h + 1) * hw, :], ones_row], axis=0)
            for c in (2 * u, 2 * u + 1):
                alpha = jnp.exp(ms[c] - m_new[c])
                p = jnp.exp(s_ref[c] - m_new[c]).astype(BF16)
                acc_ref[c] = alpha * acc_ref[c] + jnp.dot(vta, p, preferred_element_type=F32)
        return tuple(m_new)

    neg = jnp.full((1, ATT_TQ), -jnp.inf, F32)
    ms = lax.fori_loop(0, qi, lambda j, c: kv_step(j, c, False), (neg,) * (2 * len(units)))
    kv_step(qi, ms, True)

    lp = lam_ref[...]
    lam = (jnp.exp(jnp.sum(lp[0:1] * lp[1:2], axis=-1, keepdims=True))
           - jnp.exp(jnp.sum(lp[2:3] * lp[3:4], axis=-1, keepdims=True)) + lam_init)
    for u, (b, h) in enumerate(units):
        a1 = acc_ref[2 * u]
        a2 = acc_ref[2 * u + 1]
        o_t = (a1[:hw] * (1.0 / a1[hw:hw + 1]) - lam * (a2[:hw] * (1.0 / a2[hw:hw + 1])))
        o = o_t.T
        o = o * lax.rsqrt(jnp.mean(o * o, axis=-1, keepdims=True) + RMS_EPS) * gain_ref[...]
        o = o * (1.0 - lam_init)
        z = z_ref[b, :, h * hw:(h + 1) * hw].astype(F32)
        o_ref[b, :, h * hw:(h + 1) * hw] = (o * _silu(z)).astype(BF16)


def _diff_attention(qd, kd, vd_t, zd, lam_params, subln_gain, lam_init):
    bsz, seq, _ = qd.shape
    hw = 2 * DIFF_HEAD_DIM
    n_kv = seq // ATT_TK
    nb = DIFF_STREAMS
    chains = 2 * DIFF_HEADS * nb
    tile = pl.BlockSpec((nb, ATT_TQ, DIFF_WIDTH), lambda b, i: (b, i, 0))
    return pl.pallas_call(
        functools.partial(_diff_kernel, lam_init=lam_init),
        grid=(bsz // nb, seq // ATT_TQ),
        in_specs=[
            tile,
            pl.BlockSpec((nb, seq, DIFF_WIDTH), lambda b, i: (b, 0, 0)),
            pl.BlockSpec((DIFF_HEADS, seq, hw), lambda b, i: (0, 0, 0)),
            pl.BlockSpec((nb, n_kv, DIFF_WIDTH, ATT_TK), lambda b, i: (b, 0, 0, 0)),
            tile,
            pl.BlockSpec((4, DIFF_HEAD_DIM), lambda b, i: (0, 0)),
            pl.BlockSpec((1, hw), lambda b, i: (0, 0)),
        ],
        out_specs=tile,
        out_shape=jax.ShapeDtypeStruct((bsz, seq, DIFF_WIDTH), BF16),
        scratch_shapes=[pltpu.VMEM((chains, ACC_ROWS, ATT_TQ), F32),
                        pltpu.VMEM((chains, ATT_TK, ATT_TQ), F32)],
        compiler_params=pltpu.CompilerParams(
            dimension_semantics=("parallel", "parallel"), vmem_limit_bytes=VMEM_LIMIT),
        name="diffattn",
    )(qd, kd, _diff_key_features(seq), vd_t, zd, lam_params, subln_gain.reshape(1, hw).astype(F32))


SWA_STEP_BLOCKS = 4
SWA_ACC_ROWS = SWA_HEAD_DIM + 16


def _swa_features():
    w = WINDOW
    lane = jnp.arange(128, dtype=jnp.int32)[None, :]
    r = jnp.arange(2 * w, dtype=F32)[:, None]
    kf = jnp.where(lane == 0, r, 0.0) + jnp.where(lane == 1, 1.0, 0.0)
    c = jnp.arange(w, dtype=F32)[:, None]
    qf = jnp.stack([jnp.where(lane == 0, _alibi_slope(h), 0.0)
                    + jnp.where(lane == 1, -_alibi_slope(h) * (w + c), 0.0)
                    for h in range(SWA_Q_HEADS)])
    return kf.astype(BF16), qf.astype(BF16)


def _swa_kernel(q_ref, kp_ref, kc_ref, vp_ref, vc_ref, z_ref, sink_ref, kf_ref, qf_ref, o_ref, s_ref):
    i = pl.program_id(1)
    w = WINDOW
    dh = SWA_HEAD_DIM
    scale = dh ** -0.5
    nt = (((1,), (1,)), ((), ()))
    lane = lax.broadcasted_iota(jnp.int32, (w, 2 * dh), 1)
    keys = jnp.concatenate([kp_ref[0], kc_ref[0]], axis=0)
    v_t = jnp.concatenate([vp_ref[0], vc_ref[0]], axis=1)
    kf = kf_ref[...]
    ones_row = jnp.where(lax.broadcasted_iota(jnp.int32, (SWA_ACC_ROWS - dh, 2 * w), 0) == 0,
                         1.0, 0.0).astype(BF16)
    r = lax.broadcasted_iota(jnp.int32, (2 * w, w), 0)
    c = lax.broadcasted_iota(jnp.int32, (2 * w, w), 1)
    band = jnp.logical_and(r > c, r <= c + w)
    first = jnp.logical_and(band, jnp.logical_or(r >= w, i > 0))

    for n in range(SWA_STEP_BLOCKS):
        q = (q_ref[0, n * w:(n + 1) * w, :].astype(F32) * scale).astype(BF16)
        qa, qb = q[:, :2 * dh], q[:, 2 * dh:]
        zero = jnp.zeros_like(qa)
        heads = (jnp.where(lane < dh, qa, zero), jnp.where(lane < dh, qb, zero),
                 jnp.where(lane >= dh, qa, zero), jnp.where(lane >= dh, qb, zero))
        qpp = jnp.concatenate([jnp.concatenate([heads[h], qf_ref[h]], axis=1)
                               for h in range(SWA_Q_HEADS)], axis=0)
        kpp = jnp.concatenate([keys[n * w:(n + 2) * w], kf], axis=1)
        s_ref[n] = lax.dot_general(kpp, qpp, nt, preferred_element_type=F32)

    for n in range(SWA_STEP_BLOCKS):
        valid = first if n == 0 else band
        outs = []
        for kh in range(SWA_KV_HEADS):
            vta = jnp.concatenate([v_t[kh * dh:(kh + 1) * dh, n * w:(n + 2) * w], ones_row], axis=0)
            es, ms = [], []
            for g in range(SWA_GROUP):
                h = SWA_GROUP * kh + g
                s = jnp.where(valid, s_ref[n, :, h * w:(h + 1) * w], -jnp.inf)
                m = jnp.maximum(jnp.max(s, axis=0, keepdims=True), sink_ref[h][:, :1])
                es.append(jnp.exp(s - m).astype(BF16))
                ms.append(m)
            acc = jnp.dot(vta, jnp.concatenate(es, axis=1), preferred_element_type=F32)
            for g in range(SWA_GROUP):
                h = SWA_GROUP * kh + g
                denom = acc[dh:dh + 1, g * w:(g + 1) * w] + jnp.exp(sink_ref[h][:, :1] - ms[g])
                outs.append(acc[:dh, g * w:(g + 1) * w] * (1.0 / denom))
        o = jnp.concatenate(outs, axis=0).T
        z = z_ref[0, n * w:(n + 1) * w, :].astype(F32)
        o_ref[0, n * w:(n + 1) * w, :] = (o * _silu(z)).astype(BF16)


def _swa(qs, ks, vs_t, zs, sinks):
    bsz, seq, _ = qs.shape
    w = WINDOW
    n = SWA_STEP_BLOCKS
    sink_rows = jnp.broadcast_to(sinks.astype(F32).reshape(SWA_Q_HEADS, 1, 1), (SWA_Q_HEADS, 1, 128))
    kf, qf = _swa_features()
    qspec = pl.BlockSpec((1, n * w, SWA_WIDTH), lambda b, i: (b, i, 0))
    prev_block = lambda i: jnp.maximum(i * n - 1, 0)
    return pl.pallas_call(
        _swa_kernel,
        grid=(bsz, seq // (n * w)),
        in_specs=[
            qspec,
            pl.BlockSpec((1, w, SWA_KV_WIDTH), lambda b, i: (b, prev_block(i), 0)),
            pl.BlockSpec((1, n * w, SWA_KV_WIDTH), lambda b, i: (b, i, 0)),
            pl.BlockSpec((1, SWA_KV_WIDTH, w), lambda b, i: (b, 0, prev_block(i))),
            pl.BlockSpec((1, SWA_KV_WIDTH, n * w), lambda b, i: (b, 0, i)),
            qspec,
            pl.BlockSpec((SWA_Q_HEADS, 1, 128), lambda b, i: (0, 0, 0)),
            pl.BlockSpec((2 * w, 128), lambda b, i: (0, 0)),
            pl.BlockSpec((SWA_Q_HEADS, w, 128), lambda b, i: (0, 0, 0)),
        ],
        out_specs=qspec,
        out_shape=jax.ShapeDtypeStruct((bsz, seq, SWA_WIDTH), BF16),
        scratch_shapes=[pltpu.VMEM((n, 2 * w, SWA_Q_HEADS * w), F32)],
        compiler_params=pltpu.CompilerParams(
            dimension_semantics=("parallel", "parallel"), vmem_limit_bytes=VMEM_LIMIT),
        name="swa",
    )(qs, ks, ks, vs_t, vs_t, zs, sink_rows, kf, qf)


def _outproj_kernel(x_ref, ys_ref, yd_ref, yw_ref, mod_ref, w_ref, fg_ref, o_ref, *, final_norm):
    a = SSM_WIDTH
    b = SSM_WIDTH + DIFF_WIDTH
    y = (jnp.dot(ys_ref[0], w_ref[:a, :], preferred_element_type=F32)
         + jnp.dot(yd_ref[0], w_ref[a:b, :], preferred_element_type=F32)
         + jnp.dot(yw_ref[0], w_ref[b:, :], preferred_element_type=F32))
    out = x_ref[0] + mod_ref[0][2:3, :] * y
    if final_norm:
        out = out * lax.rsqrt(jnp.mean(out * out, axis=-1, keepdims=True) + RMS_EPS) * fg_ref[...]
    o_ref[0] = out


def _outproj(x, y_ssm, y_diff, y_swa, mod_l, w_out_bf16, final_gain, final_norm):
    bsz, seq, _ = x.shape
    tok = lambda n: pl.BlockSpec((1, TOK_TILE, n), lambda b, i: (b, i, 0))
    return pl.pallas_call(
        functools.partial(_outproj_kernel, final_norm=final_norm),
        grid=(bsz, seq // TOK_TILE),
        in_specs=[
            tok(D_MODEL), tok(SSM_WIDTH), tok(DIFF_WIDTH), tok(SWA_WIDTH),
            pl.BlockSpec((1, 3, D_MODEL), lambda b, i: (b, 0, 0)),
            pl.BlockSpec((D_MODEL, D_MODEL), lambda b, i: (0, 0)),
            pl.BlockSpec((1, D_MODEL), lambda b, i: (0, 0)),
        ],
        out_specs=tok(D_MODEL),
        out_shape=jax.ShapeDtypeStruct((bsz, seq, D_MODEL), F32),
        compiler_params=pltpu.CompilerParams(
            dimension_semantics=("parallel", "parallel"), vmem_limit_bytes=VMEM_LIMIT),
        name="outproj",
    )(x, y_ssm, y_diff, y_swa, mod_l, w_out_bf16, final_gain.reshape(1, D_MODEL).astype(F32))


def kernel(x, c, norm_gain, ada_w, ada_b, w_in, w_out, ssm_lam_re, ssm_lam_im, ssm_log_step,
           ssm_b_re, ssm_b_im, ssm_c_re, ssm_c_im, ssm_d, glu_w, glu_b,
           diff_lq1, diff_lk1, diff_lq2, diff_lk2, diff_subln, swa_sinks, final_gain):
    bsz = x.shape[0]
    mod = _ada(c, ada_w, ada_b).reshape(DEPTH, bsz, 3, D_MODEL)
    for l in range(DEPTH):
        proj = dict(zip(PROJ_NAMES, _inproj(x, mod[l], norm_gain[l], w_in[l])))
        tables = _s5_tables(ssm_lam_re[l], ssm_lam_im[l], ssm_log_step[l],
                            ssm_b_re[l], ssm_b_im[l], ssm_c_re[l], ssm_c_im[l])
        y_ssm = _s5(proj["u"], proj["z_ssm"], tables, ssm_d[l], glu_w[l], glu_b[l])
        lam_init = 0.8 - 0.6 * math.exp(-0.3 * l)
        lam_params = jnp.stack([diff_lq1[l], diff_lk1[l], diff_lq2[l], diff_lk2[l]]).astype(F32)
        y_diff = _diff_attention(proj["qd"], proj["kd"], proj["vd"], proj["z_diff"],
                                 lam_params, diff_subln[l], lam_init)
        y_swa = _swa(proj["qs"], proj["ks"], proj["vs"], proj["z_swa"], swa_sinks[l])
        x = _outproj(x, y_ssm, y_diff, y_swa, mod[l], w_out[l].astype(BF16), final_gain,
                     final_norm=(l == DEPTH - 1))
    return x
```

```python
import functools
import math

import jax
import jax.numpy as jnp
import numpy as np
from jax import lax
from jax.experimental import pallas as pl
from jax.experimental.pallas import tpu as pltpu

F32 = jnp.float32
BF16 = jnp.bfloat16

D_MODEL = 1024
DEPTH = 2
SSM_GROUP = 16
SSM_GROUPS = 16
SSM_WIDTH = 256
SSM_STATE = 64
SSM_LANES = SSM_GROUPS * SSM_STATE
DIFF_HEADS = 4
DIFF_HEAD_DIM = 64
DIFF_WIDTH = 512
SWA_Q_HEADS = 4
SWA_KV_HEADS = 2
SWA_GROUP = 2
SWA_HEAD_DIM = 64
SWA_WIDTH = 256
SWA_KV_WIDTH = 128
WINDOW = 128
N_ATTN_HEADS = 8
RMS_EPS = 1e-6

PROJ_NAMES = ("u", "z_ssm", "qd", "kd", "vd", "z_diff", "qs", "ks", "vs", "z_swa")
PROJ_SIZES = (SSM_WIDTH, SSM_WIDTH, DIFF_WIDTH, DIFF_WIDTH, DIFF_WIDTH, DIFF_WIDTH,
              SWA_WIDTH, SWA_KV_WIDTH, SWA_KV_WIDTH, SWA_WIDTH)
IN_COLS = sum(PROJ_SIZES)
PROJ_TRANSPOSED = ("vd", "z_diff", "vs")
LOG2E = math.log2(math.e)

SUBLANES = 8
VMEM_LIMIT = 48 * 1024 * 1024

TOK_TILE = 512
S5_CHUNK = 256
S5_BLOCK = 32
S5_STREAMS = 2
ATT_TQ = 256
ATT_TK = 256
DIFF_STREAMS = 2


def _silu(x):
    return x * jax.nn.sigmoid(x)


def _alibi_slope(head_index):
    return 2.0 ** (-(head_index + 1) * (8.0 / N_ATTN_HEADS))


def _ada_kernel(c_ref, w_ref, b_ref, o_ref):
    cond = _silu(c_ref[...])
    o_ref[0] = jnp.dot(cond, w_ref[0], preferred_element_type=F32) + b_ref[0]


def _ada(c, ada_w, ada_b):
    bsz = c.shape[0]
    col = D_MODEL
    return pl.pallas_call(
        _ada_kernel,
        grid=(DEPTH, 3),
        in_specs=[
            pl.BlockSpec((bsz, D_MODEL), lambda l, j: (0, 0)),
            pl.BlockSpec((1, D_MODEL, col), lambda l, j: (l, 0, j)),
            pl.BlockSpec((1, 1, col), lambda l, j: (l, 0, j)),
        ],
        out_specs=pl.BlockSpec((1, bsz, col), lambda l, j: (l, 0, j)),
        out_shape=jax.ShapeDtypeStruct((DEPTH, bsz, 3 * D_MODEL), F32),
        compiler_params=pltpu.CompilerParams(
            dimension_semantics=("parallel", "parallel"), vmem_limit_bytes=VMEM_LIMIT),
        name="ada",
    )(c, ada_w, ada_b.reshape(DEPTH, 1, 3 * D_MODEL))


def _inproj_kernel(x_ref, mod_ref, g_ref, w_ref, wt_ref, *out_refs):
    xf = x_ref[0]
    y = xf * lax.rsqrt(jnp.mean(xf * xf, axis=-1, keepdims=True) + RMS_EPS) * g_ref[...]
    mod = mod_ref[0]
    h = (y * (1.0 + mod[1:2, :]) + mod[0:1, :]).astype(BF16)
    t_out = lax.dot_general(wt_ref[...], h, (((1,), (1,)), ((), ())),
                            preferred_element_type=F32).astype(BF16)
    start = 0
    t_start = 0
    for name, o_ref, n in zip(PROJ_NAMES, out_refs, PROJ_SIZES):
        if name in PROJ_TRANSPOSED:
            rows = t_out[t_start:t_start + n, :]
            t_start += n
            if name == "vd":
                for t in range(TOK_TILE // ATT_TK):
                    o_ref[0, t] = rows[:, t * ATT_TK:(t + 1) * ATT_TK]
            else:
                o_ref[0] = rows
        else:
            o_ref[0] = jnp.dot(h, w_ref[:, start:start + n], preferred_element_type=F32).astype(BF16)
        start += n


def _proj_start(name):
    return sum(PROJ_SIZES[:PROJ_NAMES.index(name)])


def _inproj(x, mod_l, gain, w_in):
    bsz, seq, _ = x.shape
    grid = (bsz, seq // TOK_TILE)
    w = w_in.astype(F32)
    qd0, qs0 = _proj_start("qd"), _proj_start("qs")
    w = w.at[:, qd0:qd0 + DIFF_WIDTH].multiply(DIFF_HEAD_DIM ** -0.5 * LOG2E)
    w = w.at[:, qs0:qs0 + SWA_WIDTH].multiply(SWA_HEAD_DIM ** -0.5)
    w_in_bf16 = w.astype(BF16)
    dh = SWA_HEAD_DIM
    order = (0, 2, 1, 3)
    qs_cols = jnp.concatenate([w_in_bf16[:, qs0 + h * dh:qs0 + (h + 1) * dh] for h in order], axis=1)
    w_in_bf16 = lax.dynamic_update_slice(w_in_bf16, qs_cols, (0, qs0))
    w_t = jnp.concatenate([w_in_bf16[:, _proj_start(name):_proj_start(name) + n]
                           for name, n in zip(PROJ_NAMES, PROJ_SIZES) if name in PROJ_TRANSPOSED],
                          axis=1).T
    tiles_per_step = TOK_TILE // ATT_TK
    out_specs, out_shape = [], []
    for name, n in zip(PROJ_NAMES, PROJ_SIZES):
        if name == "vd":
            out_specs.append(pl.BlockSpec((1, tiles_per_step, n, ATT_TK), lambda b, i: (b, i, 0, 0)))
            out_shape.append(jax.ShapeDtypeStruct((bsz, seq // ATT_TK, n, ATT_TK), BF16))
        elif name in PROJ_TRANSPOSED:
            out_specs.append(pl.BlockSpec((1, n, TOK_TILE), lambda b, i: (b, 0, i)))
            out_shape.append(jax.ShapeDtypeStruct((bsz, n, seq), BF16))
        else:
            out_specs.append(pl.BlockSpec((1, TOK_TILE, n), lambda b, i: (b, i, 0)))
            out_shape.append(jax.ShapeDtypeStruct((bsz, seq, n), BF16))
    return pl.pallas_call(
        _inproj_kernel,
        grid=grid,
        in_specs=[
            pl.BlockSpec((1, TOK_TILE, D_MODEL), lambda b, i: (b, i, 0)),
            pl.BlockSpec((1, 3, D_MODEL), lambda b, i: (b, 0, 0)),
            pl.BlockSpec((1, D_MODEL), lambda b, i: (0, 0)),
            pl.BlockSpec((D_MODEL, IN_COLS), lambda b, i: (0, 0)),
            pl.BlockSpec(w_t.shape, lambda b, i: (0, 0)),
        ],
        out_specs=out_specs,
        out_shape=out_shape,
        compiler_params=pltpu.CompilerParams(
            dimension_semantics=("parallel", "parallel"), vmem_limit_bytes=VMEM_LIMIT),
        name="inproj",
    )(x, mod_l, gain.reshape(1, D_MODEL), w_in_bf16, w_t)


def _s5_tables(lam_re, lam_im, log_step, b_re, b_im, c_re, c_im):
    g, p, h = SSM_GROUPS, SSM_STATE, SSM_GROUP
    step = jnp.exp(log_step.astype(F32))[:, None]
    lr = lam_re.astype(F32)
    li = lam_im.astype(F32)
    mag = jnp.exp(lr * step)
    ang = li * step
    ab_re = mag * jnp.cos(ang)
    ab_im = mag * jnp.sin(ang)
    den = lr * lr + li * li
    f_re = ((ab_re - 1.0) * lr + ab_im * li) / den
    f_im = (ab_im * lr - (ab_re - 1.0) * li) / den
    br = b_re.astype(F32)
    bi = b_im.astype(F32)
    bb_re = f_re[..., None] * br - f_im[..., None] * bi
    bb_im = f_re[..., None] * bi + f_im[..., None] * br
    eye = jnp.eye(g, dtype=F32)
    bbd_re = jnp.einsum("gph,gk->ghkp", bb_re, eye).reshape(g * h, g * p)
    bbd_im = jnp.einsum("gph,gk->ghkp", bb_im, eye).reshape(g * h, g * p)
    bbd = jnp.concatenate([bbd_re, bbd_im], axis=1).astype(BF16)
    cbd_re = jnp.einsum("ghp,gk->gpkh", c_re.astype(F32), eye).reshape(g * p, g * h)
    cbd_im = jnp.einsum("ghp,gk->gpkh", c_im.astype(F32), eye).reshape(g * p, g * h)
    cbd = jnp.concatenate([cbd_re, -cbd_im], axis=0).astype(BF16)

    def power(n):
        n = jnp.asarray(n, F32).reshape(-1, 1, 1)
        m = jnp.exp(lr * step * n)
        return ((m * jnp.cos(ang * n)).reshape(-1, g * p), (m * jnp.sin(ang * n)).reshape(-1, g * p))

    c = S5_BLOCK // 2
    rows = jnp.arange(S5_BLOCK, dtype=F32)
    pre = power(c - rows)
    post = power(rows - c)
    row_tabs = jnp.stack([pre[0], pre[1], post[0], post[1]])
    vec = [power(float(S5_BLOCK - 1 - c)),
           power(float(S5_BLOCK)),
           power(float(c + 1))]
    vec_tabs = jnp.concatenate([t for pair in vec for t in pair], axis=0)
    return bbd, cbd, row_tabs, vec_tabs


def _cmul(a_re, a_im, b_re, b_im):
    return a_re * b_re - a_im * b_im, a_re * b_im + a_im * b_re


def _s5_kernel(u_ref, z_ref, bbd_ref, cbd_ref, tri_ref, row_ref, vec_ref, d_ref, wg_ref, bg_ref,
               o_ref, xs_ref, carry_scr):
    chunk = u_ref.shape[1]
    nb = chunk // S5_BLOCK
    n = SSM_LANES

    @pl.when(pl.program_id(1) == 0)
    def _():
        carry_scr[...] = jnp.zeros_like(carry_scr)

    streams = range(u_ref.shape[0])
    us = [u_ref[s] for s in streams]
    zss = []
    for s in streams:
        bu = jnp.dot(us[s], bbd_ref[...], preferred_element_type=F32)
        bu = bu.reshape(nb, S5_BLOCK, 2 * n)
        z_re, z_im = _cmul(row_ref[0], row_ref[1], bu[:, :, :n], bu[:, :, n:])
        zss.append(jnp.concatenate([z_re, z_im], axis=-1).reshape(chunk, 2 * n).astype(BF16))
    for s in streams:
        xs_ref[s] = jnp.dot(tri_ref[...], zss[s], preferred_element_type=F32)

    sts = []
    for s in streams:
        ends = jnp.concatenate([xs_ref[s, (k + 1) * S5_BLOCK - 1:(k + 1) * S5_BLOCK, :]
                                for k in range(nb)], axis=0)
        e_re, e_im = _cmul(vec_ref[0:1], vec_ref[1:2], ends[:, :n], ends[:, n:])
        s_re = carry_scr[s, 0:1, :]
        s_im = carry_scr[s, 1:2, :]
        inj_re, inj_im = [], []
        for k in range(nb):
            g_re, g_im = _cmul(vec_ref[4:5], vec_ref[5:6], s_re, s_im)
            inj_re.append(g_re)
            inj_im.append(g_im)
            d_re, d_im = _cmul(vec_ref[2:3], vec_ref[3:4], s_re, s_im)
            s_re = e_re[k:k + 1] + d_re
            s_im = e_im[k:k + 1] + d_im
        carry_scr[s, 0:1, :] = s_re
        carry_scr[s, 1:2, :] = s_im
        inj_re = jnp.concatenate(inj_re, axis=0)[:, None, :]
        inj_im = jnp.concatenate(inj_im, axis=0)[:, None, :]
        xs = xs_ref[s].reshape(nb, S5_BLOCK, 2 * n)
        x_re, x_im = _cmul(row_ref[2], row_ref[3], xs[:, :, :n] + inj_re, xs[:, :, n:] + inj_im)
        sts.append(jnp.concatenate([x_re, x_im], axis=-1).reshape(chunk, 2 * n).astype(BF16))

    ys = []
    for s in streams:
        y = jnp.dot(sts[s], cbd_ref[...], preferred_element_type=F32)
        y = y + d_ref[...] * us[s].astype(F32)
        ys.append(jax.nn.gelu(y))
    for s in streams:
        gate = jnp.dot(ys[s].astype(BF16), wg_ref[...], preferred_element_type=F32) + bg_ref[...]
        y = ys[s] * jax.nn.sigmoid(gate)
        o_ref[s] = (y * _silu(z_ref[s].astype(F32))).astype(BF16)


def _s5(u, z, tables, d_skip, w_glu, b_glu):
    bsz, seq, _ = u.shape
    bbd, cbd, row_tabs, vec_tabs = tables
    t = jnp.arange(S5_CHUNK, dtype=jnp.int32)
    tri = jnp.logical_and(t[:, None] // S5_BLOCK == t[None, :] // S5_BLOCK,
                          t[:, None] >= t[None, :]).astype(BF16)
    full = lambda *shape: pl.BlockSpec(shape, lambda b, i: (0,) * len(shape))
    return pl.pallas_call(
        _s5_kernel,
        grid=(bsz // S5_STREAMS, seq // S5_CHUNK),
        in_specs=[
            pl.BlockSpec((S5_STREAMS, S5_CHUNK, SSM_WIDTH), lambda b, i: (b, i, 0)),
            pl.BlockSpec((S5_STREAMS, S5_CHUNK, SSM_WIDTH), lambda b, i: (b, i, 0)),
            full(SSM_WIDTH, 2 * SSM_LANES),
            full(2 * SSM_LANES, SSM_WIDTH),
            full(S5_CHUNK, S5_CHUNK),
            full(*row_tabs.shape),
            full(*vec_tabs.shape),
            full(1, SSM_WIDTH),
            full(SSM_WIDTH, SSM_WIDTH),
            full(1, SSM_WIDTH),
        ],
        out_specs=pl.BlockSpec((S5_STREAMS, S5_CHUNK, SSM_WIDTH), lambda b, i: (b, i, 0)),
        out_shape=jax.ShapeDtypeStruct((bsz, seq, SSM_WIDTH), BF16),
        scratch_shapes=[
            pltpu.VMEM((S5_STREAMS, S5_CHUNK, 2 * SSM_LANES), F32),
            pltpu.VMEM((S5_STREAMS, SUBLANES, SSM_LANES), F32),
        ],
        compiler_params=pltpu.CompilerParams(
            dimension_semantics=("parallel", "arbitrary"), vmem_limit_bytes=VMEM_LIMIT),
        name="s5",
    )(u, z, bbd, cbd, tri, row_tabs, vec_tabs, d_skip.reshape(1, SSM_WIDTH).astype(F32),
      w_glu.astype(BF16), b_glu.reshape(1, SSM_WIDTH).astype(F32))


ACC_ROWS = 2 * DIFF_HEAD_DIM + 16
POS_SPLIT = 64


LOG2E_PARTS = 3


def _diff_features(seq):
    parts, rest = [], LOG2E
    for _ in range(LOG2E_PARTS):
        part = float(np.float32(rest).astype(BF16))
        parts.append(part)
        rest -= part
    pos = jnp.arange(seq, dtype=jnp.int32)
    hi = (pos // POS_SPLIT).astype(F32) * POS_SPLIT
    lo = (pos % POS_SPLIT).astype(F32)
    lane = (jnp.arange(2 * DIFF_HEAD_DIM, dtype=jnp.int32) % DIFF_HEAD_DIM)[None, :]
    qf = sum(jnp.where(lane % LOG2E_PARTS == i, part, 0.0) for i, part in enumerate(parts))
    qf = jnp.where(lane < 2 * LOG2E_PARTS, qf, 0.0).astype(BF16)
    tabs = []
    for h in range(DIFF_HEADS):
        slope = _alibi_slope(SWA_Q_HEADS + h)
        tabs.append(jnp.where(lane < LOG2E_PARTS, slope * hi[:, None], 0.0)
                    + jnp.where(jnp.logical_and(lane >= LOG2E_PARTS, lane < 2 * LOG2E_PARTS),
                                slope * lo[:, None], 0.0))
    return jnp.stack(tabs).astype(BF16), qf


def _diff_kernel(q_ref, k_ref, kf_ref, qf_ref, vt_ref, zt_ref, lam_ref, gain_ref, o_ref,
                 acc_ref, s_ref, qp_ref, m_ref, alpha_ref, *, lam_init):
    qi = pl.program_id(1)
    dh = DIFF_HEAD_DIM
    hw = 2 * dh
    lane_q = lax.broadcasted_iota(jnp.int32, (ATT_TQ, hw), 1)
    q_feat = jnp.broadcast_to(qf_ref[...], (ATT_TQ, hw))
    units = [(b, h) for b in range(q_ref.shape[0]) for h in range(DIFF_HEADS)]
    for u, (b, h) in enumerate(units):
        q = q_ref[b, :, h * hw:(h + 1) * hw]
        qp_ref[2 * u] = jnp.where(lane_q < dh, q, q_feat)
        qp_ref[2 * u + 1] = jnp.where(lane_q >= dh, q, q_feat)
    lane_k = lax.broadcasted_iota(jnp.int32, (ATT_TK, hw), 1)
    ones_row = jnp.where(lax.broadcasted_iota(jnp.int32, (ACC_ROWS - hw, ATT_TK), 0) == 0,
                         1.0, 0.0).astype(BF16)
    krow = lax.broadcasted_iota(jnp.int32, (ATT_TK, ATT_TQ), 0)
    qcol = lax.broadcasted_iota(jnp.int32, (ATT_TK, ATT_TQ), 1)
    nt = (((1,), (1,)), ((), ()))

    acc_ref[...] = jnp.zeros_like(acc_ref)
    m_ref[...] = jnp.full(m_ref.shape, -jnp.inf, F32)

    def kv_step(j, masked):
        k0 = pl.multiple_of(j * ATT_TK, ATT_TK)
        for u, (b, h) in enumerate(units):
            k = k_ref[b, pl.ds(k0, ATT_TK), h * hw:(h + 1) * hw]
            kf = kf_ref[h, pl.ds(k0, ATT_TK), :]
            for c, kp in ((2 * u, jnp.where(lane_k < dh, k, kf)),
                          (2 * u + 1, jnp.where(lane_k >= dh, k, kf))):
                s = lax.dot_general(kp, qp_ref[c], nt, preferred_element_type=F32)
                if masked:
                    s = jnp.where(krow <= qcol, s, -jnp.inf)
                s_ref[c] = s
                m_old = m_ref[c]
                m_new = jnp.maximum(m_old, jnp.max(s, axis=0, keepdims=True))
                m_ref[c] = m_new
                alpha_ref[c] = jnp.exp2(m_old - m_new)
        for u, (b, h) in enumerate(units):
            vta = jnp.concatenate([vt_ref[b, j, h * hw:(h + 1) * hw, :], ones_row], axis=0)
            for c in (2 * u, 2 * u + 1):
                p = jnp.exp2(s_ref[c] - m_ref[c]).astype(BF16)
                acc_ref[c] = alpha_ref[c] * acc_ref[c] + jnp.dot(vta, p, preferred_element_type=F32)

    def full_tile(j, carry):
        kv_step(j, False)
        return carry

    lax.fori_loop(0, qi, full_tile, 0)
    kv_step(qi, True)

    lp = lam_ref[...]
    lam = (jnp.exp(jnp.sum(lp[0:1] * lp[1:2], axis=-1, keepdims=True))
           - jnp.exp(jnp.sum(lp[2:3] * lp[3:4], axis=-1, keepdims=True)) + lam_init)
    gain = gain_ref[...] * (1.0 - lam_init)
    for u, (b, h) in enumerate(units):
        a1 = acc_ref[2 * u]
        a2 = acc_ref[2 * u + 1]
        o_t = (a1[:hw] * (1.0 / a1[hw:hw + 1]) - lam * (a2[:hw] * (1.0 / a2[hw:hw + 1])))
        o_t = o_t * lax.rsqrt(jnp.mean(o_t * o_t, axis=0, keepdims=True) + RMS_EPS) * gain
        z_t = zt_ref[b, h * hw:(h + 1) * hw, :].astype(F32)
        o_ref[b, h * hw:(h + 1) * hw, :] = (o_t * _silu(z_t)).astype(BF16)


def _diff_attention(qd, kd, vd_t, zd_t, lam_params, subln_gain, lam_init):
    bsz, seq, _ = qd.shape
    hw = 2 * DIFF_HEAD_DIM
    n_kv = seq // ATT_TK
    nb = DIFF_STREAMS
    chains = 2 * DIFF_HEADS * nb
    kf, qf = _diff_features(seq)
    gain_rows = jnp.broadcast_to(subln_gain.astype(F32).reshape(hw, 1), (hw, ATT_TQ))
    tile_t = pl.BlockSpec((nb, DIFF_WIDTH, ATT_TQ), lambda b, i: (b, 0, i))
    return pl.pallas_call(
        functools.partial(_diff_kernel, lam_init=lam_init),
        grid=(bsz // nb, seq // ATT_TQ),
        in_specs=[
            pl.BlockSpec((nb, ATT_TQ, DIFF_WIDTH), lambda b, i: (b, i, 0)),
            pl.BlockSpec((nb, seq, DIFF_WIDTH), lambda b, i: (b, 0, 0)),
            pl.BlockSpec((DIFF_HEADS, seq, hw), lambda b, i: (0, 0, 0)),
            pl.BlockSpec((1, hw), lambda b, i: (0, 0)),
            pl.BlockSpec((nb, n_kv, DIFF_WIDTH, ATT_TK), lambda b, i: (b, 0, 0, 0)),
            tile_t,
            pl.BlockSpec((4, DIFF_HEAD_DIM), lambda b, i: (0, 0)),
            pl.BlockSpec((hw, ATT_TQ), lambda b, i: (0, 0)),
        ],
        out_specs=tile_t,
        out_shape=jax.ShapeDtypeStruct((bsz, DIFF_WIDTH, seq), BF16),
        scratch_shapes=[pltpu.VMEM((chains, ACC_ROWS, ATT_TQ), F32),
                        pltpu.VMEM((chains, ATT_TK, ATT_TQ), F32),
                        pltpu.VMEM((chains, ATT_TQ, hw), BF16),
                        pltpu.VMEM((chains, 1, ATT_TQ), F32),
                        pltpu.VMEM((chains, 1, ATT_TQ), F32)],
        compiler_params=pltpu.CompilerParams(
            dimension_semantics=("parallel", "parallel"), vmem_limit_bytes=VMEM_LIMIT),
        name="diffattn",
    )(qd, kd, kf, qf, vd_t, zd_t, lam_params, gain_rows)


SWA_STEP_BLOCKS = 4
SWA_ACC_ROWS = SWA_HEAD_DIM + 16


def _swa_features():
    w = WINDOW
    lane = jnp.arange(128, dtype=jnp.int32)[None, :]
    r = jnp.arange(2 * w, dtype=F32)[:, None]
    kf = jnp.where(lane == 0, r, 0.0) + jnp.where(lane == 1, 1.0, 0.0)
    c = jnp.arange(w, dtype=F32)[:, None]
    qf = jnp.stack([jnp.where(lane == 0, _alibi_slope(h), 0.0)
                    + jnp.where(lane == 1, -_alibi_slope(h) * (w + c), 0.0)
                    for h in range(SWA_Q_HEADS)])
    return kf.astype(BF16), qf.astype(BF16)


def _swa_kernel(q_ref, kp_ref, kc_ref, vp_ref, vc_ref, z_ref, sink_ref, kf_ref, qf_ref, o_ref, s_ref):
    i = pl.program_id(1)
    w = WINDOW
    dh = SWA_HEAD_DIM
    nt = (((1,), (1,)), ((), ()))
    lane = lax.broadcasted_iota(jnp.int32, (w, 2 * dh), 1)
    keys = jnp.concatenate([kp_ref[0], kc_ref[0]], axis=0)
    v_t = jnp.concatenate([vp_ref[0], vc_ref[0]], axis=1)
    kf = kf_ref[...]
    ones_row = jnp.where(lax.broadcasted_iota(jnp.int32, (SWA_ACC_ROWS - dh, 2 * w), 0) == 0,
                         1.0, 0.0).astype(BF16)
    r = lax.broadcasted_iota(jnp.int32, (2 * w, w), 0)
    c = lax.broadcasted_iota(jnp.int32, (2 * w, w), 1)
    band = jnp.logical_and(r > c, r <= c + w)
    first = jnp.logical_and(band, jnp.logical_or(r >= w, i > 0))

    for n in range(SWA_STEP_BLOCKS):
        q = q_ref[0, n * w:(n + 1) * w, :]
        qa, qb = q[:, :2 * dh], q[:, 2 * dh:]
        zero = jnp.zeros_like(qa)
        heads = (jnp.where(lane < dh, qa, zero), jnp.where(lane < dh, qb, zero),
                 jnp.where(lane >= dh, qa, zero), jnp.where(lane >= dh, qb, zero))
        qpp = jnp.concatenate([jnp.concatenate([heads[h], qf_ref[h]], axis=1)
                               for h in range(SWA_Q_HEADS)], axis=0)
        kpp = jnp.concatenate([keys[n * w:(n + 2) * w], kf], axis=1)
        s_ref[n] = lax.dot_general(kpp, qpp, nt, preferred_element_type=F32)

    for n in range(SWA_STEP_BLOCKS):
        valid = first if n == 0 else band
        outs = []
        for kh in range(SWA_KV_HEADS):
            vta = jnp.concatenate([v_t[kh * dh:(kh + 1) * dh, n * w:(n + 2) * w], ones_row], axis=0)
            es, ms = [], []
            for g in range(SWA_GROUP):
                h = SWA_GROUP * kh + g
                s = jnp.where(valid, s_ref[n, :, h * w:(h + 1) * w], -jnp.inf)
                m = jnp.maximum(jnp.max(s, axis=0, keepdims=True), sink_ref[h][:, :1])
                es.append(jnp.exp(s - m).astype(BF16))
                ms.append(m)
            acc = jnp.dot(vta, jnp.concatenate(es, axis=1), preferred_element_type=F32)
            for g in range(SWA_GROUP):
                h = SWA_GROUP * kh + g
                denom = acc[dh:dh + 1, g * w:(g + 1) * w] + jnp.exp(sink_ref[h][:, :1] - ms[g])
                outs.append(acc[:dh, g * w:(g + 1) * w] * (1.0 / denom))
        o = jnp.concatenate(outs, axis=0).T
        z = z_ref[0, n * w:(n + 1) * w, :].astype(F32)
        o_ref[0, n * w:(n + 1) * w, :] = (o * _silu(z)).astype(BF16)


def _swa(qs, ks, vs_t, zs, sinks):
    bsz, seq, _ = qs.shape
    w = WINDOW
    n = SWA_STEP_BLOCKS
    sink_rows = jnp.broadcast_to(sinks.astype(F32).reshape(SWA_Q_HEADS, 1, 1), (SWA_Q_HEADS, 1, 128))
    kf, qf = _swa_features()
    qspec = pl.BlockSpec((1, n * w, SWA_WIDTH), lambda b, i: (b, i, 0))
    prev_block = lambda i: jnp.maximum(i * n - 1, 0)
    return pl.pallas_call(
        _swa_kernel,
        grid=(bsz, seq // (n * w)),
        in_specs=[
            qspec,
            pl.BlockSpec((1, w, SWA_KV_WIDTH), lambda b, i: (b, prev_block(i), 0)),
            pl.BlockSpec((1, n * w, SWA_KV_WIDTH), lambda b, i: (b, i, 0)),
            pl.BlockSpec((1, SWA_KV_WIDTH, w), lambda b, i: (b, 0, prev_block(i))),
            pl.BlockSpec((1, SWA_KV_WIDTH, n * w), lambda b, i: (b, 0, i)),
            qspec,
            pl.BlockSpec((SWA_Q_HEADS, 1, 128), lambda b, i: (0, 0, 0)),
            pl.BlockSpec((2 * w, 128), lambda b, i: (0, 0)),
            pl.BlockSpec((SWA_Q_HEADS, w, 128), lambda b, i: (0, 0, 0)),
        ],
        out_specs=qspec,
        out_shape=jax.ShapeDtypeStruct((bsz, seq, SWA_WIDTH), BF16),
        scratch_shapes=[pltpu.VMEM((n, 2 * w, SWA_Q_HEADS * w), F32)],
        compiler_params=pltpu.CompilerParams(
            dimension_semantics=("parallel", "parallel"), vmem_limit_bytes=VMEM_LIMIT),
        name="swa",
    )(qs, ks, ks, vs_t, vs_t, zs, sink_rows, kf, qf)


def _outproj_kernel(x_ref, ys_ref, yd_ref, yw_ref, mod_ref, w_ref, fg_ref, o_ref, *, final_norm):
    a = SSM_WIDTH
    b = SSM_WIDTH + DIFF_WIDTH
    y = (jnp.dot(ys_ref[0], w_ref[:a, :], preferred_element_type=F32)
         + lax.dot_general(yd_ref[0], w_ref[a:b, :], (((0,), (0,)), ((), ())),
                           preferred_element_type=F32)
         + jnp.dot(yw_ref[0], w_ref[b:, :], preferred_element_type=F32))
    out = x_ref[0] + mod_ref[0][2:3, :] * y
    if final_norm:
        out = out * lax.rsqrt(jnp.mean(out * out, axis=-1, keepdims=True) + RMS_EPS) * fg_ref[...]
    o_ref[0] = out


def _outproj(x, y_ssm, y_diff, y_swa, mod_l, w_out_bf16, final_gain, final_norm):
    bsz, seq, _ = x.shape
    tok = lambda n: pl.BlockSpec((1, TOK_TILE, n), lambda b, i: (b, i, 0))
    return pl.pallas_call(
        functools.partial(_outproj_kernel, final_norm=final_norm),
        grid=(bsz, seq // TOK_TILE),
        in_specs=[
            tok(D_MODEL), tok(SSM_WIDTH),
            pl.BlockSpec((1, DIFF_WIDTH, TOK_TILE), lambda b, i: (b, 0, i)),
            tok(SWA_WIDTH),
            pl.BlockSpec((1, 3, D_MODEL), lambda b, i: (b, 0, 0)),
            pl.BlockSpec((D_MODEL, D_MODEL), lambda b, i: (0, 0)),
            pl.BlockSpec((1, D_MODEL), lambda b, i: (0, 0)),
        ],
        out_specs=tok(D_MODEL),
        out_shape=jax.ShapeDtypeStruct((bsz, seq, D_MODEL), F32),
        compiler_params=pltpu.CompilerParams(
            dimension_semantics=("parallel", "parallel"), vmem_limit_bytes=VMEM_LIMIT),
        name="outproj",
    )(x, y_ssm, y_diff, y_swa, mod_l, w_out_bf16, final_gain.reshape(1, D_MODEL).astype(F32))


def kernel(x, c, norm_gain, ada_w, ada_b, w_in, w_out, ssm_lam_re, ssm_lam_im, ssm_log_step,
           ssm_b_re, ssm_b_im, ssm_c_re, ssm_c_im, ssm_d, glu_w, glu_b,
           diff_lq1, diff_lk1, diff_lq2, diff_lk2, diff_subln, swa_sinks, final_gain):
    bsz = x.shape[0]
    mod = _ada(c, ada_w, ada_b).reshape(DEPTH, bsz, 3, D_MODEL)
    for l in range(DEPTH):
        proj = dict(zip(PROJ_NAMES, _inproj(x, mod[l], norm_gain[l], w_in[l])))
        tables = _s5_tables(ssm_lam_re[l], ssm_lam_im[l], ssm_log_step[l],
                            ssm_b_re[l], ssm_b_im[l], ssm_c_re[l], ssm_c_im[l])
        y_ssm = _s5(proj["u"], proj["z_ssm"], tables, ssm_d[l], glu_w[l], glu_b[l])
        lam_init = 0.8 - 0.6 * math.exp(-0.3 * l)
        lam_params = jnp.stack([diff_lq1[l], diff_lk1[l], diff_lq2[l], diff_lk2[l]]).astype(F32)
        y_diff = _diff_attention(proj["qd"], proj["kd"], proj["vd"], proj["z_diff"],
                                 lam_params, diff_subln[l], lam_init)
        y_swa = _swa(proj["qs"], proj["ks"], proj["vs"], proj["z_swa"], swa_sinks[l])
        x = _outproj(x, y_ssm, y_diff, y_swa, mod[l], w_out[l].astype(BF16), final_gain,
                     final_norm=(l == DEPTH - 1))
    return x
```

```python
import functools
import math

import jax
import jax.numpy as jnp
import numpy as np
from jax import lax
from jax.experimental import pallas as pl
from jax.experimental.pallas import tpu as pltpu

F32 = jnp.float32
BF16 = jnp.bfloat16

D_MODEL = 1024
DEPTH = 2
SSM_GROUP = 16
SSM_GROUPS = 16
SSM_WIDTH = 256
SSM_STATE = 64
SSM_LANES = SSM_GROUPS * SSM_STATE
DIFF_HEADS = 4
DIFF_HEAD_DIM = 64
DIFF_WIDTH = 512
SWA_Q_HEADS = 4
SWA_KV_HEADS = 2
SWA_GROUP = 2
SWA_HEAD_DIM = 64
SWA_WIDTH = 256
SWA_KV_WIDTH = 128
WINDOW = 128
N_ATTN_HEADS = 8
RMS_EPS = 1e-6

PROJ_NAMES = ("u", "z_ssm", "qd", "kd", "vd", "z_diff", "qs", "ks", "vs", "z_swa")
PROJ_SIZES = (SSM_WIDTH, SSM_WIDTH, DIFF_WIDTH, DIFF_WIDTH, DIFF_WIDTH, DIFF_WIDTH,
              SWA_WIDTH, SWA_KV_WIDTH, SWA_KV_WIDTH, SWA_WIDTH)
IN_COLS = sum(PROJ_SIZES)
PROJ_TRANSPOSED = ("vd", "z_diff", "vs")
LOG2E = math.log2(math.e)

SUBLANES = 8
VMEM_LIMIT = 48 * 1024 * 1024

TOK_TILE = 512
OUT_TILE = 1024
S5_CHUNK = 256
S5_BLOCK = 32
S5_STREAMS = 2
ATT_TQ = 256
ATT_TK = 256
DIFF_STREAMS = 2


def _silu(x):
    return x * jax.nn.sigmoid(x)


def _alibi_slope(head_index):
    return 2.0 ** (-(head_index + 1) * (8.0 / N_ATTN_HEADS))


def _ada_kernel(c_ref, w_ref, b_ref, o_ref):
    cond = _silu(c_ref[...])
    o_ref[0] = jnp.dot(cond, w_ref[0], preferred_element_type=F32) + b_ref[0]


def _ada(c, ada_w, ada_b):
    bsz = c.shape[0]
    col = D_MODEL
    return pl.pallas_call(
        _ada_kernel,
        grid=(DEPTH, 3),
        in_specs=[
            pl.BlockSpec((bsz, D_MODEL), lambda l, j: (0, 0)),
            pl.BlockSpec((1, D_MODEL, col), lambda l, j: (l, 0, j)),
            pl.BlockSpec((1, 1, col), lambda l, j: (l, 0, j)),
        ],
        out_specs=pl.BlockSpec((1, bsz, col), lambda l, j: (l, 0, j)),
        out_shape=jax.ShapeDtypeStruct((DEPTH, bsz, 3 * D_MODEL), F32),
        compiler_params=pltpu.CompilerParams(
            dimension_semantics=("parallel", "parallel"), vmem_limit_bytes=VMEM_LIMIT),
        name="ada",
    )(c, ada_w, ada_b.reshape(DEPTH, 1, 3 * D_MODEL))


def _inproj_kernel(x_ref, mod_ref, g_ref, w_ref, wq_ref, wt_ref, *out_refs):
    xf = x_ref[0]
    y = xf * lax.rsqrt(jnp.mean(xf * xf, axis=-1, keepdims=True) + RMS_EPS) * g_ref[...]
    mod = mod_ref[0]
    h = (y * (1.0 + mod[1:2, :]) + mod[0:1, :]).astype(BF16)
    t_out = lax.dot_general(wt_ref[...], h, (((1,), (1,)), ((), ())),
                            preferred_element_type=F32).astype(BF16)
    start = 0
    t_start = 0
    for name, o_ref, n in zip(PROJ_NAMES, out_refs, PROJ_SIZES):
        if name in PROJ_TRANSPOSED:
            rows = t_out[t_start:t_start + n, :]
            t_start += n
            if name == "vd":
                for t in range(TOK_TILE // ATT_TK):
                    o_ref[0, t] = rows[:, t * ATT_TK:(t + 1) * ATT_TK]
            else:
                o_ref[0] = rows
        elif name == "qd":
            o_ref[0] = jnp.dot(h, wq_ref[:, :n], preferred_element_type=F32).astype(BF16)
        elif name == "qs":
            o_ref[0] = jnp.dot(h, wq_ref[:, DIFF_WIDTH:], preferred_element_type=F32).astype(BF16)
        else:
            o_ref[0] = jnp.dot(h, w_ref[:, start:start + n], preferred_element_type=F32).astype(BF16)
        start += n


def _proj_start(name):
    return sum(PROJ_SIZES[:PROJ_NAMES.index(name)])


def _inproj(x, mod_l, gain, w_in):
    bsz, seq, _ = x.shape
    grid = (bsz, seq // TOK_TILE)
    w_in_bf16 = w_in.astype(BF16)
    qd0, qs0 = _proj_start("qd"), _proj_start("qs")
    dh = SWA_HEAD_DIM
    w_q = jnp.concatenate(
        [w_in[:, qd0:qd0 + DIFF_WIDTH].astype(F32) * (DIFF_HEAD_DIM ** -0.5 * LOG2E)]
        + [w_in[:, qs0 + h * dh:qs0 + (h + 1) * dh].astype(F32) * dh ** -0.5 for h in (0, 2, 1, 3)],
        axis=1).astype(BF16)
    w_t = jnp.concatenate([w_in[:, _proj_start(name):_proj_start(name) + n]
                           for name, n in zip(PROJ_NAMES, PROJ_SIZES) if name in PROJ_TRANSPOSED],
                          axis=1).astype(BF16).T
    tiles_per_step = TOK_TILE // ATT_TK
    out_specs, out_shape = [], []
    for name, n in zip(PROJ_NAMES, PROJ_SIZES):
        if name == "vd":
            out_specs.append(pl.BlockSpec((1, tiles_per_step, n, ATT_TK), lambda b, i: (b, i, 0, 0)))
            out_shape.append(jax.ShapeDtypeStruct((bsz, seq // ATT_TK, n, ATT_TK), BF16))
        elif name in PROJ_TRANSPOSED:
            out_specs.append(pl.BlockSpec((1, n, TOK_TILE), lambda b, i: (b, 0, i)))
            out_shape.append(jax.ShapeDtypeStruct((bsz, n, seq), BF16))
        else:
            out_specs.append(pl.BlockSpec((1, TOK_TILE, n), lambda b, i: (b, i, 0)))
            out_shape.append(jax.ShapeDtypeStruct((bsz, seq, n), BF16))
    return pl.pallas_call(
        _inproj_kernel,
        grid=grid,
        in_specs=[
            pl.BlockSpec((1, TOK_TILE, D_MODEL), lambda b, i: (b, i, 0)),
            pl.BlockSpec((1, 3, D_MODEL), lambda b, i: (b, 0, 0)),
            pl.BlockSpec((1, D_MODEL), lambda b, i: (0, 0)),
            pl.BlockSpec((D_MODEL, IN_COLS), lambda b, i: (0, 0)),
            pl.BlockSpec(w_q.shape, lambda b, i: (0, 0)),
            pl.BlockSpec(w_t.shape, lambda b, i: (0, 0)),
        ],
        out_specs=out_specs,
        out_shape=out_shape,
        compiler_params=pltpu.CompilerParams(
            dimension_semantics=("parallel", "parallel"), vmem_limit_bytes=VMEM_LIMIT),
        name="inproj",
    )(x, mod_l, gain.reshape(1, D_MODEL), w_in_bf16, w_q, w_t)


def _s5_tables(lam_re, lam_im, log_step, b_re, b_im, c_re, c_im):
    g, p, h = SSM_GROUPS, SSM_STATE, SSM_GROUP
    step = jnp.exp(log_step.astype(F32))[:, None]
    lr = lam_re.astype(F32)
    li = lam_im.astype(F32)
    mag = jnp.exp(lr * step)
    ang = li * step
    ab_re = mag * jnp.cos(ang)
    ab_im = mag * jnp.sin(ang)
    den = lr * lr + li * li
    f_re = ((ab_re - 1.0) * lr + ab_im * li) / den
    f_im = (ab_im * lr - (ab_re - 1.0) * li) / den
    br = b_re.astype(F32)
    bi = b_im.astype(F32)
    bb_re = f_re[..., None] * br - f_im[..., None] * bi
    bb_im = f_re[..., None] * bi + f_im[..., None] * br
    eye = jnp.eye(g, dtype=F32)
    bbd_re = jnp.einsum("gph,gk->ghkp", bb_re, eye).reshape(g * h, g * p)
    bbd_im = jnp.einsum("gph,gk->ghkp", bb_im, eye).reshape(g * h, g * p)
    bbd = jnp.concatenate([bbd_re, bbd_im], axis=1).astype(BF16)
    cbd_re = jnp.einsum("ghp,gk->gpkh", c_re.astype(F32), eye).reshape(g * p, g * h)
    cbd_im = jnp.einsum("ghp,gk->gpkh", c_im.astype(F32), eye).reshape(g * p, g * h)
    cbd = jnp.concatenate([cbd_re, -cbd_im], axis=0).astype(BF16)

    def power(n):
        n = jnp.asarray(n, F32).reshape(-1, 1, 1)
        m = jnp.exp(lr * step * n)
        return ((m * jnp.cos(ang * n)).reshape(-1, g * p), (m * jnp.sin(ang * n)).reshape(-1, g * p))

    c = S5_BLOCK // 2
    rows = jnp.arange(S5_BLOCK, dtype=F32)
    pre = power(c - rows)
    post = power(rows - c)
    row_tabs = jnp.stack([pre[0], pre[1], post[0], post[1]])
    vec = [power(float(S5_BLOCK - 1 - c)),
           power(float(S5_BLOCK)),
           power(float(c + 1))]
    vec_tabs = jnp.concatenate([t for pair in vec for t in pair], axis=0)
    return bbd, cbd, row_tabs, vec_tabs


def _cmul(a_re, a_im, b_re, b_im):
    return a_re * b_re - a_im * b_im, a_re * b_im + a_im * b_re


def _s5_kernel(u_ref, z_ref, bbd_ref, cbd_ref, tri_ref, row_ref, vec_ref, d_ref, wg_ref, bg_ref,
               o_ref, xs_ref, carry_scr):
    chunk = u_ref.shape[1]
    nb = chunk // S5_BLOCK
    n = SSM_LANES

    @pl.when(pl.program_id(1) == 0)
    def _():
        carry_scr[...] = jnp.zeros_like(carry_scr)

    streams = range(u_ref.shape[0])
    us = [u_ref[s] for s in streams]
    zss = []
    for s in streams:
        bu = jnp.dot(us[s], bbd_ref[...], preferred_element_type=F32)
        bu = bu.reshape(nb, S5_BLOCK, 2 * n)
        z_re, z_im = _cmul(row_ref[0], row_ref[1], bu[:, :, :n], bu[:, :, n:])
        zss.append(jnp.concatenate([z_re, z_im], axis=-1).reshape(chunk, 2 * n).astype(BF16))
    for s in streams:
        xs_ref[s] = jnp.dot(tri_ref[...], zss[s], preferred_element_type=F32)

    sts = []
    for s in streams:
        ends = jnp.concatenate([xs_ref[s, (k + 1) * S5_BLOCK - 1:(k + 1) * S5_BLOCK, :]
                                for k in range(nb)], axis=0)
        e_re, e_im = _cmul(vec_ref[0:1], vec_ref[1:2], ends[:, :n], ends[:, n:])
        s_re = carry_scr[s, 0:1, :]
        s_im = carry_scr[s, 1:2, :]
        inj_re, inj_im = [], []
        for k in range(nb):
            g_re, g_im = _cmul(vec_ref[4:5], vec_ref[5:6], s_re, s_im)
            inj_re.append(g_re)
            inj_im.append(g_im)
            d_re, d_im = _cmul(vec_ref[2:3], vec_ref[3:4], s_re, s_im)
            s_re = e_re[k:k + 1] + d_re
            s_im = e_im[k:k + 1] + d_im
        carry_scr[s, 0:1, :] = s_re
        carry_scr[s, 1:2, :] = s_im
        inj_re = jnp.concatenate(inj_re, axis=0)[:, None, :]
        inj_im = jnp.concatenate(inj_im, axis=0)[:, None, :]
        xs = xs_ref[s].reshape(nb, S5_BLOCK, 2 * n)
        x_re, x_im = _cmul(row_ref[2], row_ref[3], xs[:, :, :n] + inj_re, xs[:, :, n:] + inj_im)
        sts.append(jnp.concatenate([x_re, x_im], axis=-1).reshape(chunk, 2 * n).astype(BF16))

    ys = []
    for s in streams:
        y = jnp.dot(sts[s], cbd_ref[...], preferred_element_type=F32)
        y = y + d_ref[...] * us[s].astype(F32)
        ys.append(jax.nn.gelu(y))
    for s in streams:
        gate = jnp.dot(ys[s].astype(BF16), wg_ref[...], preferred_element_type=F32) + bg_ref[...]
        y = ys[s] * jax.nn.sigmoid(gate)
        o_ref[s] = (y * _silu(z_ref[s].astype(F32))).astype(BF16)


def _s5(u, z, tables, d_skip, w_glu, b_glu):
    bsz, seq, _ = u.shape
    bbd, cbd, row_tabs, vec_tabs = tables
    t = np.arange(S5_CHUNK, dtype=np.int32)
    tri = np.logical_and(t[:, None] // S5_BLOCK == t[None, :] // S5_BLOCK,
                         t[:, None] >= t[None, :]).astype(BF16)
    full = lambda *shape: pl.BlockSpec(shape, lambda b, i: (0,) * len(shape))
    return pl.pallas_call(
        _s5_kernel,
        grid=(bsz // S5_STREAMS, seq // S5_CHUNK),
        in_specs=[
            pl.BlockSpec((S5_STREAMS, S5_CHUNK, SSM_WIDTH), lambda b, i: (b, i, 0)),
            pl.BlockSpec((S5_STREAMS, S5_CHUNK, SSM_WIDTH), lambda b, i: (b, i, 0)),
            full(SSM_WIDTH, 2 * SSM_LANES),
            full(2 * SSM_LANES, SSM_WIDTH),
            full(S5_CHUNK, S5_CHUNK),
            full(*row_tabs.shape),
            full(*vec_tabs.shape),
            full(1, SSM_WIDTH),
            full(SSM_WIDTH, SSM_WIDTH),
            full(1, SSM_WIDTH),
        ],
        out_specs=pl.BlockSpec((S5_STREAMS, S5_CHUNK, SSM_WIDTH), lambda b, i: (b, i, 0)),
        out_shape=jax.ShapeDtypeStruct((bsz, seq, SSM_WIDTH), BF16),
        scratch_shapes=[
            pltpu.VMEM((S5_STREAMS, S5_CHUNK, 2 * SSM_LANES), F32),
            pltpu.VMEM((S5_STREAMS, SUBLANES, SSM_LANES), F32),
        ],
        compiler_params=pltpu.CompilerParams(
            dimension_semantics=("parallel", "arbitrary"), vmem_limit_bytes=VMEM_LIMIT),
        name="s5",
    )(u, z, bbd, cbd, tri, row_tabs, vec_tabs, d_skip.reshape(1, SSM_WIDTH).astype(F32),
      w_glu.astype(BF16), b_glu.reshape(1, SSM_WIDTH).astype(F32))


ACC_ROWS = 2 * DIFF_HEAD_DIM + 16
POS_SPLIT = 64


LOG2E_PARTS = 3


def _diff_features(seq):
    parts, rest = [], LOG2E
    for _ in range(LOG2E_PARTS):
        part = float(np.float32(rest).astype(BF16))
        parts.append(part)
        rest -= part
    pos = np.arange(seq, dtype=np.int32)
    hi = ((pos // POS_SPLIT) * POS_SPLIT).astype(np.float32)
    lo = (pos % POS_SPLIT).astype(np.float32)
    lane = (np.arange(2 * DIFF_HEAD_DIM, dtype=np.int32) % DIFF_HEAD_DIM)[None, :]
    qf = sum(np.where(lane % LOG2E_PARTS == i, np.float32(part), np.float32(0)) for i, part in enumerate(parts))
    qf = np.where(lane < 2 * LOG2E_PARTS, qf, np.float32(0)).astype(BF16)
    tabs = []
    for h in range(DIFF_HEADS):
        slope = np.float32(_alibi_slope(SWA_Q_HEADS + h))
        tabs.append(np.where(lane < LOG2E_PARTS, slope * hi[:, None], np.float32(0))
                    + np.where(np.logical_and(lane >= LOG2E_PARTS, lane < 2 * LOG2E_PARTS),
                               slope * lo[:, None], np.float32(0)))
    return np.stack(tabs).astype(BF16), qf


def _diff_kernel(q_ref, k_ref, kf_ref, qf_ref, vt_ref, zt_ref, lam_ref, gain_ref, o_ref,
                 acc_ref, s_ref, qp_ref, m_ref, alpha_ref, *, lam_init):
    qi = pl.program_id(1)
    dh = DIFF_HEAD_DIM
    hw = 2 * dh
    lane_q = lax.broadcasted_iota(jnp.int32, (ATT_TQ, hw), 1)
    q_feat = jnp.broadcast_to(qf_ref[...], (ATT_TQ, hw))
    units = [(b, h) for b in range(q_ref.shape[0]) for h in range(DIFF_HEADS)]
    for u, (b, h) in enumerate(units):
        q = q_ref[b, :, h * hw:(h + 1) * hw]
        qp_ref[2 * u] = jnp.where(lane_q < dh, q, q_feat)
        qp_ref[2 * u + 1] = jnp.where(lane_q >= dh, q, q_feat)
    lane_k = lax.broadcasted_iota(jnp.int32, (ATT_TK, hw), 1)
    ones_row = jnp.where(lax.broadcasted_iota(jnp.int32, (ACC_ROWS - hw, ATT_TK), 0) == 0,
                         1.0, 0.0).astype(BF16)
    krow = lax.broadcasted_iota(jnp.int32, (ATT_TK, ATT_TQ), 0)
    qcol = lax.broadcasted_iota(jnp.int32, (ATT_TK, ATT_TQ), 1)
    nt = (((1,), (1,)), ((), ()))

    acc_ref[...] = jnp.zeros_like(acc_ref)
    m_ref[...] = jnp.full(m_ref.shape, -jnp.inf, F32)

    def kv_step(j, masked):
        k0 = pl.multiple_of(j * ATT_TK, ATT_TK)
        for u, (b, h) in enumerate(units):
            k = k_ref[b, pl.ds(k0, ATT_TK), h * hw:(h + 1) * hw]
            kf = kf_ref[h, pl.ds(k0, ATT_TK), :]
            for c, kp in ((2 * u, jnp.where(lane_k < dh, k, kf)),
                          (2 * u + 1, jnp.where(lane_k >= dh, k, kf))):
                s = lax.dot_general(kp, qp_ref[c], nt, preferred_element_type=F32)
                if masked:
                    s = jnp.where(krow <= qcol, s, -jnp.inf)
                s_ref[c] = s
                m_old = m_ref[c]
                m_new = jnp.maximum(m_old, jnp.max(s, axis=0, keepdims=True))
                m_ref[c] = m_new
                alpha_ref[c] = jnp.exp2(m_old - m_new)
        for u, (b, h) in enumerate(units):
            vta = jnp.concatenate([vt_ref[b, j, h * hw:(h + 1) * hw, :], ones_row], axis=0)
            for c in (2 * u, 2 * u + 1):
                p = jnp.exp2(s_ref[c] - m_ref[c]).astype(BF16)
                acc_ref[c] = alpha_ref[c] * acc_ref[c] + jnp.dot(vta, p, preferred_element_type=F32)

    def full_tile(j, carry):
        kv_step(j, False)
        return carry

    lax.fori_loop(0, qi, full_tile, 0)
    kv_step(qi, True)

    lp = lam_ref[...]
    lam = (jnp.exp(jnp.sum(lp[0:1] * lp[1:2], axis=-1, keepdims=True))
           - jnp.exp(jnp.sum(lp[2:3] * lp[3:4], axis=-1, keepdims=True)) + lam_init)
    gain = gain_ref[...] * (1.0 - lam_init)
    for u, (b, h) in enumerate(units):
        a1 = acc_ref[2 * u]
        a2 = acc_ref[2 * u + 1]
        o_t = (a1[:hw] * (1.0 / a1[hw:hw + 1]) - lam * (a2[:hw] * (1.0 / a2[hw:hw + 1])))
        o_t = o_t * lax.rsqrt(jnp.mean(o_t * o_t, axis=0, keepdims=True) + RMS_EPS) * gain
        z_t = zt_ref[b, h * hw:(h + 1) * hw, :].astype(F32)
        o_ref[b, h * hw:(h + 1) * hw, :] = (o_t * _silu(z_t)).astype(BF16)


def _diff_attention(qd, kd, vd_t, zd_t, lam_params, subln_gain, lam_init):
    bsz, seq, _ = qd.shape
    hw = 2 * DIFF_HEAD_DIM
    n_kv = seq // ATT_TK
    nb = DIFF_STREAMS
    chains = 2 * DIFF_HEADS * nb
    kf, qf = _diff_features(seq)
    gain_rows = jnp.broadcast_to(subln_gain.astype(F32).reshape(hw, 1), (hw, ATT_TQ))
    tile_t = pl.BlockSpec((nb, DIFF_WIDTH, ATT_TQ), lambda b, i: (b, 0, i))
    return pl.pallas_call(
        functools.partial(_diff_kernel, lam_init=lam_init),
        grid=(bsz // nb, seq // ATT_TQ),
        in_specs=[
            pl.BlockSpec((nb, ATT_TQ, DIFF_WIDTH), lambda b, i: (b, i, 0)),
            pl.BlockSpec((nb, seq, DIFF_WIDTH), lambda b, i: (b, 0, 0)),
            pl.BlockSpec((DIFF_HEADS, seq, hw), lambda b, i: (0, 0, 0)),
            pl.BlockSpec((1, hw), lambda b, i: (0, 0)),
            pl.BlockSpec((nb, n_kv, DIFF_WIDTH, ATT_TK), lambda b, i: (b, 0, 0, 0)),
            tile_t,
            pl.BlockSpec((4, DIFF_HEAD_DIM), lambda b, i: (0, 0)),
            pl.BlockSpec((hw, ATT_TQ), lambda b, i: (0, 0)),
        ],
        out_specs=tile_t,
        out_shape=jax.ShapeDtypeStruct((bsz, DIFF_WIDTH, seq), BF16),
        scratch_shapes=[pltpu.VMEM((chains, ACC_ROWS, ATT_TQ), F32),
                        pltpu.VMEM((chains, ATT_TK, ATT_TQ), F32),
                        pltpu.VMEM((chains, ATT_TQ, hw), BF16),
                        pltpu.VMEM((chains, 1, ATT_TQ), F32),
                        pltpu.VMEM((chains, 1, ATT_TQ), F32)],
        compiler_params=pltpu.CompilerParams(
            dimension_semantics=("parallel", "parallel"), vmem_limit_bytes=VMEM_LIMIT),
        name="diffattn",
    )(qd, kd, kf, qf, vd_t, zd_t, lam_params, gain_rows)


SWA_STEP_BLOCKS = 4
SWA_ACC_ROWS = SWA_HEAD_DIM + 16


def _swa_features():
    w = WINDOW
    zero = np.float32(0)
    lane = np.arange(128, dtype=np.int32)[None, :]
    r = np.arange(2 * w, dtype=np.float32)[:, None]
    kf = np.where(lane == 0, r, zero) + np.where(lane == 1, np.float32(1), zero)
    c = np.arange(w, dtype=np.float32)[:, None]
    qf = np.stack([np.where(lane == 0, np.float32(_alibi_slope(h)), zero)
                   + np.where(lane == 1, np.float32(-_alibi_slope(h)) * (w + c), zero)
                   for h in range(SWA_Q_HEADS)])
    return kf.astype(BF16), qf.astype(BF16)


def _swa_kernel(q_ref, kp_ref, kc_ref, vp_ref, vc_ref, z_ref, sink_ref, kf_ref, qf_ref, o_ref, s_ref):
    i = pl.program_id(1)
    w = WINDOW
    dh = SWA_HEAD_DIM
    nt = (((1,), (1,)), ((), ()))
    lane = lax.broadcasted_iota(jnp.int32, (w, 2 * dh), 1)
    keys = jnp.concatenate([kp_ref[0], kc_ref[0]], axis=0)
    v_t = jnp.concatenate([vp_ref[0], vc_ref[0]], axis=1)
    kf = kf_ref[...]
    ones_row = jnp.where(lax.broadcasted_iota(jnp.int32, (SWA_ACC_ROWS - dh, 2 * w), 0) == 0,
                         1.0, 0.0).astype(BF16)
    r = lax.broadcasted_iota(jnp.int32, (2 * w, w), 0)
    c = lax.broadcasted_iota(jnp.int32, (2 * w, w), 1)
    band = jnp.logical_and(r > c, r <= c + w)
    first = jnp.logical_and(band, jnp.logical_or(r >= w, i > 0))

    for n in range(SWA_STEP_BLOCKS):
        q = q_ref[0, n * w:(n + 1) * w, :]
        qa, qb = q[:, :2 * dh], q[:, 2 * dh:]
        zero = jnp.zeros_like(qa)
        heads = (jnp.where(lane < dh, qa, zero), jnp.where(lane < dh, qb, zero),
                 jnp.where(lane >= dh, qa, zero), jnp.where(lane >= dh, qb, zero))
        qpp = jnp.concatenate([jnp.concatenate([heads[h], qf_ref[h]], axis=1)
                               for h in range(SWA_Q_HEADS)], axis=0)
        kpp = jnp.concatenate([keys[n * w:(n + 2) * w], kf], axis=1)
        s_ref[n] = lax.dot_general(kpp, qpp, nt, preferred_element_type=F32)

    for n in range(SWA_STEP_BLOCKS):
        valid = first if n == 0 else band
        outs = []
        for kh in range(SWA_KV_HEADS):
            vta = jnp.concatenate([v_t[kh * dh:(kh + 1) * dh, n * w:(n + 2) * w], ones_row], axis=0)
            es, ms = [], []
            for g in range(SWA_GROUP):
                h = SWA_GROUP * kh + g
                s = jnp.where(valid, s_ref[n, :, h * w:(h + 1) * w], -jnp.inf)
                m = jnp.maximum(jnp.max(s, axis=0, keepdims=True), sink_ref[h][:, :1])
                es.append(jnp.exp(s - m).astype(BF16))
                ms.append(m)
            acc = jnp.dot(vta, jnp.concatenate(es, axis=1), preferred_element_type=F32)
            for g in range(SWA_GROUP):
                h = SWA_GROUP * kh + g
                denom = acc[dh:dh + 1, g * w:(g + 1) * w] + jnp.exp(sink_ref[h][:, :1] - ms[g])
                outs.append(acc[:dh, g * w:(g + 1) * w] * (1.0 / denom))
        o = jnp.concatenate(outs, axis=0).T
        z = z_ref[0, n * w:(n + 1) * w, :].astype(F32)
        o_ref[0, n * w:(n + 1) * w, :] = (o * _silu(z)).astype(BF16)


def _swa(qs, ks, vs_t, zs, sinks):
    bsz, seq, _ = qs.shape
    w = WINDOW
    n = SWA_STEP_BLOCKS
    sink_rows = jnp.broadcast_to(sinks.astype(F32).reshape(SWA_Q_HEADS, 1, 1), (SWA_Q_HEADS, 1, 128))
    kf, qf = _swa_features()
    qspec = pl.BlockSpec((1, n * w, SWA_WIDTH), lambda b, i: (b, i, 0))
    prev_block = lambda i: jnp.maximum(i * n - 1, 0)
    return pl.pallas_call(
        _swa_kernel,
        grid=(bsz, seq // (n * w)),
        in_specs=[
            qspec,
            pl.BlockSpec((1, w, SWA_KV_WIDTH), lambda b, i: (b, prev_block(i), 0)),
            pl.BlockSpec((1, n * w, SWA_KV_WIDTH), lambda b, i: (b, i, 0)),
            pl.BlockSpec((1, SWA_KV_WIDTH, w), lambda b, i: (b, 0, prev_block(i))),
            pl.BlockSpec((1, SWA_KV_WIDTH, n * w), lambda b, i: (b, 0, i)),
            qspec,
            pl.BlockSpec((SWA_Q_HEADS, 1, 128), lambda b, i: (0, 0, 0)),
            pl.BlockSpec((2 * w, 128), lambda b, i: (0, 0)),
            pl.BlockSpec((SWA_Q_HEADS, w, 128), lambda b, i: (0, 0, 0)),
        ],
        out_specs=qspec,
        out_shape=jax.ShapeDtypeStruct((bsz, seq, SWA_WIDTH), BF16),
        scratch_shapes=[pltpu.VMEM((n, 2 * w, SWA_Q_HEADS * w), F32)],
        compiler_params=pltpu.CompilerParams(
            dimension_semantics=("parallel", "parallel"), vmem_limit_bytes=VMEM_LIMIT),
        name="swa",
    )(qs, ks, ks, vs_t, vs_t, zs, sink_rows, kf, qf)


def _outproj_kernel(x_ref, ys_ref, yd_ref, yw_ref, mod_ref, w_ref, fg_ref, o_ref, *, final_norm):
    a = SSM_WIDTH
    b = SSM_WIDTH + DIFF_WIDTH
    y = (jnp.dot(ys_ref[0], w_ref[:a, :], preferred_element_type=F32)
         + lax.dot_general(yd_ref[0], w_ref[a:b, :], (((0,), (0,)), ((), ())),
                           preferred_element_type=F32)
         + jnp.dot(yw_ref[0], w_ref[b:, :], preferred_element_type=F32))
    out = x_ref[0] + mod_ref[0][2:3, :] * y
    if final_norm:
        out = out * lax.rsqrt(jnp.mean(out * out, axis=-1, keepdims=True) + RMS_EPS) * fg_ref[...]
    o_ref[0] = out


def _outproj(x, y_ssm, y_diff, y_swa, mod_l, w_out_bf16, final_gain, final_norm):
    bsz, seq, _ = x.shape
    tok = lambda n: pl.BlockSpec((1, OUT_TILE, n), lambda b, i: (b, i, 0))
    return pl.pallas_call(
        functools.partial(_outproj_kernel, final_norm=final_norm),
        grid=(bsz, seq // OUT_TILE),
        in_specs=[
            tok(D_MODEL), tok(SSM_WIDTH),
            pl.BlockSpec((1, DIFF_WIDTH, OUT_TILE), lambda b, i: (b, 0, i)),
            tok(SWA_WIDTH),
            pl.BlockSpec((1, 3, D_MODEL), lambda b, i: (b, 0, 0)),
            pl.BlockSpec((D_MODEL, D_MODEL), lambda b, i: (0, 0)),
            pl.BlockSpec((1, D_MODEL), lambda b, i: (0, 0)),
        ],
        out_specs=tok(D_MODEL),
        out_shape=jax.ShapeDtypeStruct((bsz, seq, D_MODEL), F32),
        compiler_params=pltpu.CompilerParams(
            dimension_semantics=("parallel", "parallel"), vmem_limit_bytes=VMEM_LIMIT),
        name="outproj",
    )(x, y_ssm, y_diff, y_swa, mod_l, w_out_bf16, final_gain.reshape(1, D_MODEL).astype(F32))


def kernel(x, c, norm_gain, ada_w, ada_b, w_in, w_out, ssm_lam_re, ssm_lam_im, ssm_log_step,
           ssm_b_re, ssm_b_im, ssm_c_re, ssm_c_im, ssm_d, glu_w, glu_b,
           diff_lq1, diff_lk1, diff_lq2, diff_lk2, diff_subln, swa_sinks, final_gain):
    bsz = x.shape[0]
    mod = _ada(c, ada_w, ada_b).reshape(DEPTH, bsz, 3, D_MODEL)
    s5_tables = jax.vmap(_s5_tables)(ssm_lam_re, ssm_lam_im, ssm_log_step,
                                     ssm_b_re, ssm_b_im, ssm_c_re, ssm_c_im)
    for l in range(DEPTH):
        proj = dict(zip(PROJ_NAMES, _inproj(x, mod[l], norm_gain[l], w_in[l])))
        tables = tuple(t[l] for t in s5_tables)
        y_ssm = _s5(proj["u"], proj["z_ssm"], tables, ssm_d[l], glu_w[l], glu_b[l])
        lam_init = 0.8 - 0.6 * math.exp(-0.3 * l)
        lam_params = jnp.stack([diff_lq1[l], diff_lk1[l], diff_lq2[l], diff_lk2[l]]).astype(F32)
        y_diff = _diff_attention(proj["qd"], proj["kd"], proj["vd"], proj["z_diff"],
                                 lam_params, diff_subln[l], lam_init)
        y_swa = _swa(proj["qs"], proj["ks"], proj["vs"], proj["z_swa"], swa_sinks[l])
        x = _outproj(x, y_ssm, y_diff, y_swa, mod[l], w_out[l].astype(BF16), final_gain,
                     final_norm=(l == DEPTH - 1))
    return x
```

```python
import functools
import math

import jax
import jax.numpy as jnp
import numpy as np
from jax import lax
from jax.experimental import pallas as pl
from jax.experimental.pallas import tpu as pltpu

F32 = jnp.float32
BF16 = jnp.bfloat16

D_MODEL = 1024
DEPTH = 2
SSM_GROUP = 16
SSM_GROUPS = 16
SSM_WIDTH = 256
SSM_STATE = 64
SSM_LANES = SSM_GROUPS * SSM_STATE
DIFF_HEADS = 4
DIFF_HEAD_DIM = 64
DIFF_WIDTH = 512
SWA_Q_HEADS = 4
SWA_KV_HEADS = 2
SWA_GROUP = 2
SWA_HEAD_DIM = 64
SWA_WIDTH = 256
SWA_KV_WIDTH = 128
WINDOW = 128
N_ATTN_HEADS = 8
RMS_EPS = 1e-6

PROJ_NAMES = ("u", "z_ssm", "qd", "kd", "vd", "z_diff", "qs", "ks", "vs", "z_swa")
PROJ_SIZES = (SSM_WIDTH, SSM_WIDTH, DIFF_WIDTH, DIFF_WIDTH, DIFF_WIDTH, DIFF_WIDTH,
              SWA_WIDTH, SWA_KV_WIDTH, SWA_KV_WIDTH, SWA_WIDTH)
IN_COLS = sum(PROJ_SIZES)
PROJ_TRANSPOSED = ("vd", "z_diff", "vs")
LOG2E = math.log2(math.e)

SUBLANES = 8
VMEM_LIMIT = 48 * 1024 * 1024

TOK_TILE = 512
OUT_TILE = 1024
S5_CHUNK = 256
S5_BLOCK = 32
S5_STREAMS = 2
ATT_TQ = 256
ATT_TK = 256
DIFF_STREAMS = 2


def _silu(x):
    return x * jax.nn.sigmoid(x)


def _alibi_slope(head_index):
    return 2.0 ** (-(head_index + 1) * (8.0 / N_ATTN_HEADS))


def _ada_kernel(c_ref, w_ref, b_ref, o_ref):
    cond = _silu(c_ref[...])
    o_ref[0] = jnp.dot(cond, w_ref[0], preferred_element_type=F32) + b_ref[0]


def _ada(c, ada_w, ada_b):
    bsz = c.shape[0]
    col = D_MODEL
    return pl.pallas_call(
        _ada_kernel,
        grid=(DEPTH, 3),
        in_specs=[
            pl.BlockSpec((bsz, D_MODEL), lambda l, j: (0, 0)),
            pl.BlockSpec((1, D_MODEL, col), lambda l, j: (l, 0, j)),
            pl.BlockSpec((1, 1, col), lambda l, j: (l, 0, j)),
        ],
        out_specs=pl.BlockSpec((1, bsz, col), lambda l, j: (l, 0, j)),
        out_shape=jax.ShapeDtypeStruct((DEPTH, bsz, 3 * D_MODEL), F32),
        compiler_params=pltpu.CompilerParams(
            dimension_semantics=("parallel", "parallel"), vmem_limit_bytes=VMEM_LIMIT),
        name="ada",
    )(c, ada_w, ada_b.reshape(DEPTH, 1, 3 * D_MODEL))


def _inproj_kernel(x_ref, mod_ref, g_ref, w_ref, wq_ref, wt_ref, *out_refs):
    xf = x_ref[0]
    y = xf * lax.rsqrt(jnp.mean(xf * xf, axis=-1, keepdims=True) + RMS_EPS) * g_ref[...]
    mod = mod_ref[0]
    h = (y * (1.0 + mod[1:2, :]) + mod[0:1, :]).astype(BF16)
    t_out = lax.dot_general(wt_ref[...], h, (((1,), (1,)), ((), ())),
                            preferred_element_type=F32).astype(BF16)
    start = 0
    t_start = 0
    for name, o_ref, n in zip(PROJ_NAMES, out_refs, PROJ_SIZES):
        if name in PROJ_TRANSPOSED:
            rows = t_out[t_start:t_start + n, :]
            t_start += n
            if name == "vd":
                for t in range(TOK_TILE // ATT_TK):
                    o_ref[0, t] = rows[:, t * ATT_TK:(t + 1) * ATT_TK]
            else:
                o_ref[0] = rows
        elif name == "qd":
            o_ref[0] = jnp.dot(h, wq_ref[:, :n], preferred_element_type=F32).astype(BF16)
        elif name == "qs":
            o_ref[0] = jnp.dot(h, wq_ref[:, DIFF_WIDTH:], preferred_element_type=F32).astype(BF16)
        else:
            o_ref[0] = jnp.dot(h, w_ref[:, start:start + n], preferred_element_type=F32).astype(BF16)
        start += n


def _proj_start(name):
    return sum(PROJ_SIZES[:PROJ_NAMES.index(name)])


def _inproj(x, mod_l, gain, w_in):
    bsz, seq, _ = x.shape
    grid = (bsz, seq // TOK_TILE)
    w_in_bf16 = w_in.astype(BF16)
    qd0, qs0 = _proj_start("qd"), _proj_start("qs")
    dh = SWA_HEAD_DIM
    w_q = jnp.concatenate(
        [w_in[:, qd0:qd0 + DIFF_WIDTH].astype(F32) * (DIFF_HEAD_DIM ** -0.5 * LOG2E)]
        + [w_in[:, qs0 + h * dh:qs0 + (h + 1) * dh].astype(F32) * dh ** -0.5 for h in (0, 2, 1, 3)],
        axis=1).astype(BF16)
    w_t = jnp.concatenate([w_in[:, _proj_start(name):_proj_start(name) + n]
                           for name, n in zip(PROJ_NAMES, PROJ_SIZES) if name in PROJ_TRANSPOSED],
                          axis=1).astype(BF16).T
    tiles_per_step = TOK_TILE // ATT_TK
    out_specs, out_shape = [], []
    for name, n in zip(PROJ_NAMES, PROJ_SIZES):
        if name == "vd":
            out_specs.append(pl.BlockSpec((1, tiles_per_step, n, ATT_TK), lambda b, i: (b, i, 0, 0)))
            out_shape.append(jax.ShapeDtypeStruct((bsz, seq // ATT_TK, n, ATT_TK), BF16))
        elif name in PROJ_TRANSPOSED:
            out_specs.append(pl.BlockSpec((1, n, TOK_TILE), lambda b, i: (b, 0, i)))
            out_shape.append(jax.ShapeDtypeStruct((bsz, n, seq), BF16))
        else:
            out_specs.append(pl.BlockSpec((1, TOK_TILE, n), lambda b, i: (b, i, 0)))
            out_shape.append(jax.ShapeDtypeStruct((bsz, seq, n), BF16))
    return pl.pallas_call(
        _inproj_kernel,
        grid=grid,
        in_specs=[
            pl.BlockSpec((1, TOK_TILE, D_MODEL), lambda b, i: (b, i, 0)),
            pl.BlockSpec((1, 3, D_MODEL), lambda b, i: (b, 0, 0)),
            pl.BlockSpec((1, D_MODEL), lambda b, i: (0, 0)),
            pl.BlockSpec((D_MODEL, IN_COLS), lambda b, i: (0, 0)),
            pl.BlockSpec(w_q.shape, lambda b, i: (0, 0)),
            pl.BlockSpec(w_t.shape, lambda b, i: (0, 0)),
        ],
        out_specs=out_specs,
        out_shape=out_shape,
        compiler_params=pltpu.CompilerParams(
            dimension_semantics=("parallel", "parallel"), vmem_limit_bytes=VMEM_LIMIT),
        name="inproj",
    )(x, mod_l, gain.reshape(1, D_MODEL), w_in_bf16, w_q, w_t)


def _s5_tables(lam_re, lam_im, log_step, b_re, b_im, c_re, c_im):
    g, p, h = SSM_GROUPS, SSM_STATE, SSM_GROUP
    step = jnp.exp(log_step.astype(F32))[:, None]
    lr = lam_re.astype(F32)
    li = lam_im.astype(F32)
    mag = jnp.exp(lr * step)
    ang = li * step
    ab_re = mag * jnp.cos(ang)
    ab_im = mag * jnp.sin(ang)
    den = lr * lr + li * li
    f_re = ((ab_re - 1.0) * lr + ab_im * li) / den
    f_im = (ab_im * lr - (ab_re - 1.0) * li) / den
    br = b_re.astype(F32)
    bi = b_im.astype(F32)
    bb_re = f_re[..., None] * br - f_im[..., None] * bi
    bb_im = f_re[..., None] * bi + f_im[..., None] * br
    eye = jnp.eye(g, dtype=F32)
    bbd_re = jnp.einsum("gph,gk->ghkp", bb_re, eye).reshape(g * h, g * p)
    bbd_im = jnp.einsum("gph,gk->ghkp", bb_im, eye).reshape(g * h, g * p)
    bbd = jnp.concatenate([bbd_re, bbd_im], axis=1).astype(BF16)
    cbd_re = jnp.einsum("ghp,gk->gpkh", c_re.astype(F32), eye).reshape(g * p, g * h)
    cbd_im = jnp.einsum("ghp,gk->gpkh", c_im.astype(F32), eye).reshape(g * p, g * h)
    cbd = jnp.concatenate([cbd_re, -cbd_im], axis=0).astype(BF16)

    def power(n):
        n = jnp.asarray(n, F32).reshape(-1, 1, 1)
        m = jnp.exp(lr * step * n)
        return ((m * jnp.cos(ang * n)).reshape(-1, g * p), (m * jnp.sin(ang * n)).reshape(-1, g * p))

    c = S5_BLOCK // 2
    rows = jnp.arange(S5_BLOCK, dtype=F32)
    pre = power(c - rows)
    post = power(rows - c)
    row_tabs = jnp.stack([pre[0], pre[1], post[0], post[1]])
    vec = [power(float(S5_BLOCK - 1 - c)),
           power(float(S5_BLOCK)),
           power(float(c + 1))]
    vec_tabs = jnp.concatenate([t for pair in vec for t in pair], axis=0)
    return bbd, cbd, row_tabs, vec_tabs


def _cmul(a_re, a_im, b_re, b_im):
    return a_re * b_re - a_im * b_im, a_re * b_im + a_im * b_re


def _s5_kernel(u_ref, z_ref, bbd_ref, cbd_ref, tri_ref, row_ref, vec_ref, d_ref, wg_ref, bg_ref,
               o_ref, xs_ref, carry_scr):
    chunk = u_ref.shape[1]
    nb = chunk // S5_BLOCK
    n = SSM_LANES

    @pl.when(pl.program_id(1) == 0)
    def _():
        carry_scr[...] = jnp.zeros_like(carry_scr)

    streams = range(u_ref.shape[0])
    us = [u_ref[s] for s in streams]
    zss = []
    for s in streams:
        bu = jnp.dot(us[s], bbd_ref[...], preferred_element_type=F32)
        bu = bu.reshape(nb, S5_BLOCK, 2 * n)
        z_re, z_im = _cmul(row_ref[0], row_ref[1], bu[:, :, :n], bu[:, :, n:])
        zss.append(jnp.concatenate([z_re, z_im], axis=-1).reshape(chunk, 2 * n).astype(BF16))
    for s in streams:
        xs_ref[s] = jnp.dot(tri_ref[...], zss[s], preferred_element_type=F32)

    sts = []
    for s in streams:
        ends = jnp.concatenate([xs_ref[s, (k + 1) * S5_BLOCK - 1:(k + 1) * S5_BLOCK, :]
                                for k in range(nb)], axis=0)
        e_re, e_im = _cmul(vec_ref[0:1], vec_ref[1:2], ends[:, :n], ends[:, n:])
        s_re = carry_scr[s, 0:1, :]
        s_im = carry_scr[s, 1:2, :]
        inj_re, inj_im = [], []
        for k in range(nb):
            g_re, g_im = _cmul(vec_ref[4:5], vec_ref[5:6], s_re, s_im)
            inj_re.append(g_re)
            inj_im.append(g_im)
            d_re, d_im = _cmul(vec_ref[2:3], vec_ref[3:4], s_re, s_im)
            s_re = e_re[k:k + 1] + d_re
            s_im = e_im[k:k + 1] + d_im
        carry_scr[s, 0:1, :] = s_re
        carry_scr[s, 1:2, :] = s_im
        inj_re = jnp.concatenate(inj_re, axis=0)[:, None, :]
        inj_im = jnp.concatenate(inj_im, axis=0)[:, None, :]
        xs = xs_ref[s].reshape(nb, S5_BLOCK, 2 * n)
        x_re, x_im = _cmul(row_ref[2], row_ref[3], xs[:, :, :n] + inj_re, xs[:, :, n:] + inj_im)
        sts.append(jnp.concatenate([x_re, x_im], axis=-1).reshape(chunk, 2 * n).astype(BF16))

    ys = []
    for s in streams:
        y = jnp.dot(sts[s], cbd_ref[...], preferred_element_type=F32)
        y = y + d_ref[...] * us[s].astype(F32)
        ys.append(jax.nn.gelu(y))
    for s in streams:
        gate = jnp.dot(ys[s].astype(BF16), wg_ref[...], preferred_element_type=F32) + bg_ref[...]
        y = ys[s] * jax.nn.sigmoid(gate)
        o_ref[s] = (y * _silu(z_ref[s].astype(F32))).astype(BF16)


def _s5(u, z, tables, d_skip, w_glu, b_glu):
    bsz, seq, _ = u.shape
    bbd, cbd, row_tabs, vec_tabs = tables
    t = np.arange(S5_CHUNK, dtype=np.int32)
    tri = np.logical_and(t[:, None] // S5_BLOCK == t[None, :] // S5_BLOCK,
                         t[:, None] >= t[None, :]).astype(BF16)
    full = lambda *shape: pl.BlockSpec(shape, lambda b, i: (0,) * len(shape))
    return pl.pallas_call(
        _s5_kernel,
        grid=(bsz // S5_STREAMS, seq // S5_CHUNK),
        in_specs=[
            pl.BlockSpec((S5_STREAMS, S5_CHUNK, SSM_WIDTH), lambda b, i: (b, i, 0)),
            pl.BlockSpec((S5_STREAMS, S5_CHUNK, SSM_WIDTH), lambda b, i: (b, i, 0)),
            full(SSM_WIDTH, 2 * SSM_LANES),
            full(2 * SSM_LANES, SSM_WIDTH),
            full(S5_CHUNK, S5_CHUNK),
            full(*row_tabs.shape),
            full(*vec_tabs.shape),
            full(1, SSM_WIDTH),
            full(SSM_WIDTH, SSM_WIDTH),
            full(1, SSM_WIDTH),
        ],
        out_specs=pl.BlockSpec((S5_STREAMS, S5_CHUNK, SSM_WIDTH), lambda b, i: (b, i, 0)),
        out_shape=jax.ShapeDtypeStruct((bsz, seq, SSM_WIDTH), BF16),
        scratch_shapes=[
            pltpu.VMEM((S5_STREAMS, S5_CHUNK, 2 * SSM_LANES), F32),
            pltpu.VMEM((S5_STREAMS, SUBLANES, SSM_LANES), F32),
        ],
        compiler_params=pltpu.CompilerParams(
            dimension_semantics=("parallel", "arbitrary"), vmem_limit_bytes=VMEM_LIMIT),
        name="s5",
    )(u, z, bbd, cbd, tri, row_tabs, vec_tabs, d_skip.reshape(1, SSM_WIDTH).astype(F32),
      w_glu.astype(BF16), b_glu.reshape(1, SSM_WIDTH).astype(F32))


ACC_ROWS = 2 * DIFF_HEAD_DIM + 16
POS_SPLIT = 64


LOG2E_PARTS = 3


def _diff_features(seq):
    parts, rest = [], LOG2E
    for _ in range(LOG2E_PARTS):
        part = float(np.float32(rest).astype(BF16))
        parts.append(part)
        rest -= part
    pos = np.arange(seq, dtype=np.int32)
    hi = ((pos // POS_SPLIT) * POS_SPLIT).astype(np.float32)
    lo = (pos % POS_SPLIT).astype(np.float32)
    lane = (np.arange(2 * DIFF_HEAD_DIM, dtype=np.int32) % DIFF_HEAD_DIM)[None, :]
    qf = sum(np.where(lane % LOG2E_PARTS == i, np.float32(part), np.float32(0)) for i, part in enumerate(parts))
    qf = np.where(lane < 2 * LOG2E_PARTS, qf, np.float32(0)).astype(BF16)
    tabs = []
    for h in range(DIFF_HEADS):
        slope = np.float32(_alibi_slope(SWA_Q_HEADS + h))
        tabs.append(np.where(lane < LOG2E_PARTS, slope * hi[:, None], np.float32(0))
                    + np.where(np.logical_and(lane >= LOG2E_PARTS, lane < 2 * LOG2E_PARTS),
                               slope * lo[:, None], np.float32(0)))
    return np.stack(tabs).astype(BF16), qf


def _diff_kernel(q_ref, k_ref, kf_ref, qf_ref, vt_ref, zt_ref, lam_ref, gain_ref, o_ref,
                 acc_ref, s_ref, qp_ref, m_ref, msub_ref, alpha_ref, *, lam_init):
    qi = pl.program_id(1)
    dh = DIFF_HEAD_DIM
    hw = 2 * dh
    lane_q = lax.broadcasted_iota(jnp.int32, (ATT_TQ, hw), 1)
    q_feat = jnp.broadcast_to(qf_ref[...], (ATT_TQ, hw))
    units = [(b, h) for b in range(q_ref.shape[0]) for h in range(DIFF_HEADS)]
    for u, (b, h) in enumerate(units):
        q = q_ref[b, :, h * hw:(h + 1) * hw]
        qp_ref[2 * u] = jnp.where(lane_q < dh, q, q_feat)
        qp_ref[2 * u + 1] = jnp.where(lane_q >= dh, q, q_feat)
    lane_k = lax.broadcasted_iota(jnp.int32, (ATT_TK, hw), 1)
    ones_row = jnp.where(lax.broadcasted_iota(jnp.int32, (ACC_ROWS - hw, ATT_TK), 0) == 0,
                         1.0, 0.0).astype(BF16)
    krow = lax.broadcasted_iota(jnp.int32, (ATT_TK, ATT_TQ), 0)
    qcol = lax.broadcasted_iota(jnp.int32, (ATT_TK, ATT_TQ), 1)
    nt = (((1,), (1,)), ((), ()))

    acc_ref[...] = jnp.zeros_like(acc_ref)
    m_ref[...] = jnp.full(m_ref.shape, -jnp.inf, F32)

    def scores(t, buf, masked):
        k0 = pl.multiple_of(t * ATT_TK, ATT_TK)
        for u, (b, h) in enumerate(units):
            k = k_ref[b, pl.ds(k0, ATT_TK), h * hw:(h + 1) * hw]
            kf = kf_ref[h, pl.ds(k0, ATT_TK), :]
            for c, kp in ((2 * u, jnp.where(lane_k < dh, k, kf)),
                          (2 * u + 1, jnp.where(lane_k >= dh, k, kf))):
                s = lax.dot_general(kp, qp_ref[c], nt, preferred_element_type=F32)
                if masked:
                    s = jnp.where(krow <= qcol, s, -jnp.inf)
                s_ref[buf, c] = s
                m_old = m_ref[c]
                m_new = jnp.maximum(m_old, jnp.max(s, axis=0, keepdims=True))
                m_ref[c] = m_new
                msub_ref[buf, c] = m_new
                alpha_ref[buf, c] = jnp.exp2(m_old - m_new)

    def values(t, buf):
        for u, (b, h) in enumerate(units):
            vta = jnp.concatenate([vt_ref[b, t, h * hw:(h + 1) * hw, :], ones_row], axis=0)
            for c in (2 * u, 2 * u + 1):
                p = jnp.exp2(s_ref[buf, c] - msub_ref[buf, c]).astype(BF16)
                acc_ref[c] = (alpha_ref[buf, c] * acc_ref[c]
                              + jnp.dot(vta, p, preferred_element_type=F32))

    pairs = jnp.maximum(qi - 1, 0) // 2
    rest = qi - 1 - 2 * pairs
    t0 = 2 * pairs

    @pl.when(qi == 0)
    def _():
        scores(0, 0, True)
        values(0, 0)

    @pl.when(qi >= 1)
    def _():
        scores(0, 0, False)

    def two_tiles(i, carry):
        t = 2 * i
        scores(t + 1, 1, False)
        values(t, 0)
        scores(t + 2, 0, False)
        values(t + 1, 1)
        return carry

    lax.fori_loop(0, pairs, two_tiles, 0)

    @pl.when(jnp.logical_and(qi >= 1, rest == 1))
    def _():
        scores(t0 + 1, 1, False)
        values(t0, 0)
        scores(t0 + 2, 0, True)
        values(t0 + 1, 1)
        values(t0 + 2, 0)

    @pl.when(jnp.logical_and(qi >= 1, rest == 0))
    def _():
        scores(t0 + 1, 1, True)
        values(t0, 0)
        values(t0 + 1, 1)

    lp = lam_ref[...]
    lam = (jnp.exp(jnp.sum(lp[0:1] * lp[1:2], axis=-1, keepdims=True))
           - jnp.exp(jnp.sum(lp[2:3] * lp[3:4], axis=-1, keepdims=True)) + lam_init)
    gain = gain_ref[...] * (1.0 - lam_init)
    for u, (b, h) in enumerate(units):
        a1 = acc_ref[2 * u]
        a2 = acc_ref[2 * u + 1]
        o_t = (a1[:hw] * (1.0 / a1[hw:hw + 1]) - lam * (a2[:hw] * (1.0 / a2[hw:hw + 1])))
        o_t = o_t * lax.rsqrt(jnp.mean(o_t * o_t, axis=0, keepdims=True) + RMS_EPS) * gain
        z_t = zt_ref[b, h * hw:(h + 1) * hw, :].astype(F32)
        o_ref[b, h * hw:(h + 1) * hw, :] = (o_t * _silu(z_t)).astype(BF16)


def _diff_attention(qd, kd, vd_t, zd_t, lam_params, subln_gain, lam_init):
    bsz, seq, _ = qd.shape
    hw = 2 * DIFF_HEAD_DIM
    n_kv = seq // ATT_TK
    nb = DIFF_STREAMS
    chains = 2 * DIFF_HEADS * nb
    kf, qf = _diff_features(seq)
    gain_rows = jnp.broadcast_to(subln_gain.astype(F32).reshape(hw, 1), (hw, ATT_TQ))
    tile_t = pl.BlockSpec((nb, DIFF_WIDTH, ATT_TQ), lambda b, i: (b, 0, i))
    return pl.pallas_call(
        functools.partial(_diff_kernel, lam_init=lam_init),
        grid=(bsz // nb, seq // ATT_TQ),
        in_specs=[
            pl.BlockSpec((nb, ATT_TQ, DIFF_WIDTH), lambda b, i: (b, i, 0)),
            pl.BlockSpec((nb, seq, DIFF_WIDTH), lambda b, i: (b, 0, 0)),
            pl.BlockSpec((DIFF_HEADS, seq, hw), lambda b, i: (0, 0, 0)),
            pl.BlockSpec((1, hw), lambda b, i: (0, 0)),
            pl.BlockSpec((nb, n_kv, DIFF_WIDTH, ATT_TK), lambda b, i: (b, 0, 0, 0)),
            tile_t,
            pl.BlockSpec((4, DIFF_HEAD_DIM), lambda b, i: (0, 0)),
            pl.BlockSpec((hw, ATT_TQ), lambda b, i: (0, 0)),
        ],
        out_specs=tile_t,
        out_shape=jax.ShapeDtypeStruct((bsz, DIFF_WIDTH, seq), BF16),
        scratch_shapes=[pltpu.VMEM((chains, ACC_ROWS, ATT_TQ), F32),
                        pltpu.VMEM((2, chains, ATT_TK, ATT_TQ), F32),
                        pltpu.VMEM((chains, ATT_TQ, hw), BF16),
                        pltpu.VMEM((chains, 1, ATT_TQ), F32),
                        pltpu.VMEM((2, chains, 1, ATT_TQ), F32),
                        pltpu.VMEM((2, chains, 1, ATT_TQ), F32)],
        compiler_params=pltpu.CompilerParams(
            dimension_semantics=("parallel", "parallel"), vmem_limit_bytes=VMEM_LIMIT),
        name="diffattn",
    )(qd, kd, kf, qf, vd_t, zd_t, lam_params, gain_rows)


SWA_STEP_BLOCKS = 4
SWA_ACC_ROWS = SWA_HEAD_DIM + 16


def _swa_features():
    w = WINDOW
    zero = np.float32(0)
    lane = np.arange(128, dtype=np.int32)[None, :]
    r = np.arange(2 * w, dtype=np.float32)[:, None]
    kf = np.where(lane == 0, r, zero) + np.where(lane == 1, np.float32(1), zero)
    c = np.arange(w, dtype=np.float32)[:, None]
    qf = np.stack([np.where(lane == 0, np.float32(_alibi_slope(h)), zero)
                   + np.where(lane == 1, np.float32(-_alibi_slope(h)) * (w + c), zero)
                   for h in range(SWA_Q_HEADS)])
    return kf.astype(BF16), qf.astype(BF16)


def _swa_kernel(q_ref, kp_ref, kc_ref, vp_ref, vc_ref, z_ref, sink_ref, kf_ref, qf_ref, o_ref, s_ref):
    i = pl.program_id(1)
    w = WINDOW
    dh = SWA_HEAD_DIM
    nt = (((1,), (1,)), ((), ()))
    lane = lax.broadcasted_iota(jnp.int32, (w, 2 * dh), 1)
    keys = jnp.concatenate([kp_ref[0], kc_ref[0]], axis=0)
    v_t = jnp.concatenate([vp_ref[0], vc_ref[0]], axis=1)
    kf = kf_ref[...]
    ones_row = jnp.where(lax.broadcasted_iota(jnp.int32, (SWA_ACC_ROWS - dh, 2 * w), 0) == 0,
                         1.0, 0.0).astype(BF16)
    r = lax.broadcasted_iota(jnp.int32, (2 * w, w), 0)
    c = lax.broadcasted_iota(jnp.int32, (2 * w, w), 1)
    band = jnp.logical_and(r > c, r <= c + w)
    first = jnp.logical_and(band, jnp.logical_or(r >= w, i > 0))

    for n in range(SWA_STEP_BLOCKS):
        q = q_ref[0, n * w:(n + 1) * w, :]
        qa, qb = q[:, :2 * dh], q[:, 2 * dh:]
        zero = jnp.zeros_like(qa)
        heads = (jnp.where(lane < dh, qa, zero), jnp.where(lane < dh, qb, zero),
                 jnp.where(lane >= dh, qa, zero), jnp.where(lane >= dh, qb, zero))
        qpp = jnp.concatenate([jnp.concatenate([heads[h], qf_ref[h]], axis=1)
                               for h in range(SWA_Q_HEADS)], axis=0)
        kpp = jnp.concatenate([keys[n * w:(n + 2) * w], kf], axis=1)
        s_ref[n] = lax.dot_general(kpp, qpp, nt, preferred_element_type=F32)

    for n in range(SWA_STEP_BLOCKS):
        valid = first if n == 0 else band
        outs = []
        for kh in range(SWA_KV_HEADS):
            vta = jnp.concatenate([v_t[kh * dh:(kh + 1) * dh, n * w:(n + 2) * w], ones_row], axis=0)
            es, ms = [], []
            for g in range(SWA_GROUP):
                h = SWA_GROUP * kh + g
                s = jnp.where(valid, s_ref[n, :, h * w:(h + 1) * w], -jnp.inf)
                m = jnp.maximum(jnp.max(s, axis=0, keepdims=True), sink_ref[h][:, :1])
                es.append(jnp.exp(s - m).astype(BF16))
                ms.append(m)
            acc = jnp.dot(vta, jnp.concatenate(es, axis=1), preferred_element_type=F32)
            for g in range(SWA_GROUP):
                h = SWA_GROUP * kh + g
                denom = acc[dh:dh + 1, g * w:(g + 1) * w] + jnp.exp(sink_ref[h][:, :1] - ms[g])
                outs.append(acc[:dh, g * w:(g + 1) * w] * (1.0 / denom))
        o = jnp.concatenate(outs, axis=0).T
        z = z_ref[0, n * w:(n + 1) * w, :].astype(F32)
        o_ref[0, n * w:(n + 1) * w, :] = (o * _silu(z)).astype(BF16)


def _swa(qs, ks, vs_t, zs, sinks):
    bsz, seq, _ = qs.shape
    w = WINDOW
    n = SWA_STEP_BLOCKS
    sink_rows = jnp.broadcast_to(sinks.astype(F32).reshape(SWA_Q_HEADS, 1, 1), (SWA_Q_HEADS, 1, 128))
    kf, qf = _swa_features()
    qspec = pl.BlockSpec((1, n * w, SWA_WIDTH), lambda b, i: (b, i, 0))
    prev_block = lambda i: jnp.maximum(i * n - 1, 0)
    return pl.pallas_call(
        _swa_kernel,
        grid=(bsz, seq // (n * w)),
        in_specs=[
            qspec,
            pl.BlockSpec((1, w, SWA_KV_WIDTH), lambda b, i: (b, prev_block(i), 0)),
            pl.BlockSpec((1, n * w, SWA_KV_WIDTH), lambda b, i: (b, i, 0)),
            pl.BlockSpec((1, SWA_KV_WIDTH, w), lambda b, i: (b, 0, prev_block(i))),
            pl.BlockSpec((1, SWA_KV_WIDTH, n * w), lambda b, i: (b, 0, i)),
            qspec,
            pl.BlockSpec((SWA_Q_HEADS, 1, 128), lambda b, i: (0, 0, 0)),
            pl.BlockSpec((2 * w, 128), lambda b, i: (0, 0)),
            pl.BlockSpec((SWA_Q_HEADS, w, 128), lambda b, i: (0, 0, 0)),
        ],
        out_specs=qspec,
        out_shape=jax.ShapeDtypeStruct((bsz, seq, SWA_WIDTH), BF16),
        scratch_shapes=[pltpu.VMEM((n, 2 * w, SWA_Q_HEADS * w), F32)],
        compiler_params=pltpu.CompilerParams(
            dimension_semantics=("parallel", "parallel"), vmem_limit_bytes=VMEM_LIMIT),
        name="swa",
    )(qs, ks, ks, vs_t, vs_t, zs, sink_rows, kf, qf)


def _outproj_kernel(x_ref, ys_ref, yd_ref, yw_ref, mod_ref, w_ref, fg_ref, o_ref, *, final_norm):
    a = SSM_WIDTH
    b = SSM_WIDTH + DIFF_WIDTH
    y = (jnp.dot(ys_ref[0], w_ref[:a, :], preferred_element_type=F32)
         + lax.dot_general(yd_ref[0], w_ref[a:b, :], (((0,), (0,)), ((), ())),
                           preferred_element_type=F32)
         + jnp.dot(yw_ref[0], w_ref[b:, :], preferred_element_type=F32))
    out = x_ref[0] + mod_ref[0][2:3, :] * y
    if final_norm:
        out = out * lax.rsqrt(jnp.mean(out * out, axis=-1, keepdims=True) + RMS_EPS) * fg_ref[...]
    o_ref[0] = out


def _outproj(x, y_ssm, y_diff, y_swa, mod_l, w_out_bf16, final_gain, final_norm):
    bsz, seq, _ = x.shape
    tok = lambda n: pl.BlockSpec((1, OUT_TILE, n), lambda b, i: (b, i, 0))
    return pl.pallas_call(
        functools.partial(_outproj_kernel, final_norm=final_norm),
        grid=(bsz, seq // OUT_TILE),
        in_specs=[
            tok(D_MODEL), tok(SSM_WIDTH),
            pl.BlockSpec((1, DIFF_WIDTH, OUT_TILE), lambda b, i: (b, 0, i)),
            tok(SWA_WIDTH),
            pl.BlockSpec((1, 3, D_MODEL), lambda b, i: (b, 0, 0)),
            pl.BlockSpec((D_MODEL, D_MODEL), lambda b, i: (0, 0)),
            pl.BlockSpec((1, D_MODEL), lambda b, i: (0, 0)),
        ],
        out_specs=tok(D_MODEL),
        out_shape=jax.ShapeDtypeStruct((bsz, seq, D_MODEL), F32),
        compiler_params=pltpu.CompilerParams(
            dimension_semantics=("parallel", "parallel"), vmem_limit_bytes=VMEM_LIMIT),
        name="outproj",
    )(x, y_ssm, y_diff, y_swa, mod_l, w_out_bf16, final_gain.reshape(1, D_MODEL).astype(F32))


def kernel(x, c, norm_gain, ada_w, ada_b, w_in, w_out, ssm_lam_re, ssm_lam_im, ssm_log_step,
           ssm_b_re, ssm_b_im, ssm_c_re, ssm_c_im, ssm_d, glu_w, glu_b,
           diff_lq1, diff_lk1, diff_lq2, diff_lk2, diff_subln, swa_sinks, final_gain):
    bsz = x.shape[0]
    mod = _ada(c, ada_w, ada_b).reshape(DEPTH, bsz, 3, D_MODEL)
    s5_tables = jax.vmap(_s5_tables)(ssm_lam_re, ssm_lam_im, ssm_log_step,
                                     ssm_b_re, ssm_b_im, ssm_c_re, ssm_c_im)
    for l in range(DEPTH):
        proj = dict(zip(PROJ_NAMES, _inproj(x, mod[l], norm_gain[l], w_in[l])))
        tables = tuple(t[l] for t in s5_tables)
        y_ssm = _s5(proj["u"], proj["z_ssm"], tables, ssm_d[l], glu_w[l], glu_b[l])
        lam_init = 0.8 - 0.6 * math.exp(-0.3 * l)
        lam_params = jnp.stack([diff_lq1[l], diff_lk1[l], diff_lq2[l], diff_lk2[l]]).astype(F32)
        y_diff = _diff_attention(proj["qd"], proj["kd"], proj["vd"], proj["z_diff"],
                                 lam_params, diff_subln[l], lam_init)
        y_swa = _swa(proj["qs"], proj["ks"], proj["vs"], proj["z_swa"], swa_sinks[l])
        x = _outproj(x, y_ssm, y_diff, y_swa, mod[l], w_out[l].astype(BF16), final_gain,
                     final_norm=(l == DEPTH - 1))
    return x
```

```python
import functools
import math

import jax
import jax.numpy as jnp
import numpy as np
from jax import lax
from jax.experimental import pallas as pl
from jax.experimental.pallas import tpu as pltpu

F32 = jnp.float32
BF16 = jnp.bfloat16

D_MODEL = 1024
DEPTH = 2
SSM_GROUP = 16
SSM_GROUPS = 16
SSM_WIDTH = 256
SSM_STATE = 64
SSM_LANES = SSM_GROUPS * SSM_STATE
DIFF_HEADS = 4
DIFF_HEAD_DIM = 64
DIFF_WIDTH = 512
SWA_Q_HEADS = 4
SWA_KV_HEADS = 2
SWA_GROUP = 2
SWA_HEAD_DIM = 64
SWA_WIDTH = 256
SWA_KV_WIDTH = 128
WINDOW = 128
N_ATTN_HEADS = 8
RMS_EPS = 1e-6

PROJ_NAMES = ("u", "z_ssm", "qd", "kd", "vd", "z_diff", "qs", "ks", "vs", "z_swa")
PROJ_SIZES = (SSM_WIDTH, SSM_WIDTH, DIFF_WIDTH, DIFF_WIDTH, DIFF_WIDTH, DIFF_WIDTH,
              SWA_WIDTH, SWA_KV_WIDTH, SWA_KV_WIDTH, SWA_WIDTH)
IN_COLS = sum(PROJ_SIZES)
PROJ_TRANSPOSED = ("vd", "z_diff", "vs")
LOG2E = math.log2(math.e)

SUBLANES = 8
LANES = 128
VMEM_LIMIT = 48 * 1024 * 1024

TOK_TILE = 512
OUT_TILE = 1024
S5_CHUNK = 256
S5_BLOCK = 32
S5_STREAMS = 2
ATT_TQ = 256
ATT_TK = 256
DIFF_STREAMS = 2


def _silu(x):
    return x * jax.nn.sigmoid(x)


def _alibi_slope(head_index):
    return 2.0 ** (-(head_index + 1) * (8.0 / N_ATTN_HEADS))


def _ada_kernel(c_ref, w_ref, b_ref, o_ref):
    cond = _silu(c_ref[...])
    o_ref[0] = jnp.dot(cond, w_ref[0], preferred_element_type=F32) + b_ref[0]


def _ada(c, ada_w, ada_b):
    bsz = c.shape[0]
    col = D_MODEL
    return pl.pallas_call(
        _ada_kernel,
        grid=(DEPTH, 3),
        in_specs=[
            pl.BlockSpec((bsz, D_MODEL), lambda l, j: (0, 0)),
            pl.BlockSpec((1, D_MODEL, col), lambda l, j: (l, 0, j)),
            pl.BlockSpec((1, 1, col), lambda l, j: (l, 0, j)),
        ],
        out_specs=pl.BlockSpec((1, bsz, col), lambda l, j: (l, 0, j)),
        out_shape=jax.ShapeDtypeStruct((DEPTH, bsz, 3 * D_MODEL), F32),
        compiler_params=pltpu.CompilerParams(
            dimension_semantics=("parallel", "parallel"), vmem_limit_bytes=VMEM_LIMIT),
        name="ada",
    )(c, ada_w, ada_b.reshape(DEPTH, 1, 3 * D_MODEL))


def _inproj_kernel(x_ref, mod_ref, g_ref, w_ref, wq_ref, wt_ref, *out_refs):
    xf = x_ref[0]
    y = xf * lax.rsqrt(jnp.mean(xf * xf, axis=-1, keepdims=True) + RMS_EPS) * g_ref[...]
    mod = mod_ref[0]
    h = (y * (1.0 + mod[1:2, :]) + mod[0:1, :]).astype(BF16)
    t_out = lax.dot_general(wt_ref[...], h, (((1,), (1,)), ((), ())),
                            preferred_element_type=F32).astype(BF16)
    start = 0
    t_start = 0
    for name, o_ref, n in zip(PROJ_NAMES, out_refs, PROJ_SIZES):
        if name in PROJ_TRANSPOSED:
            rows = t_out[t_start:t_start + n, :]
            t_start += n
            if name == "vd":
                for t in range(TOK_TILE // ATT_TK):
                    o_ref[0, t] = rows[:, t * ATT_TK:(t + 1) * ATT_TK]
            else:
                o_ref[0] = rows
        elif name == "qd":
            o_ref[0] = jnp.dot(h, wq_ref[:, :n], preferred_element_type=F32).astype(BF16)
        elif name == "qs":
            o_ref[0] = jnp.dot(h, wq_ref[:, DIFF_WIDTH:], preferred_element_type=F32).astype(BF16)
        else:
            o_ref[0] = jnp.dot(h, w_ref[:, start:start + n], preferred_element_type=F32).astype(BF16)
        start += n


def _proj_start(name):
    return sum(PROJ_SIZES[:PROJ_NAMES.index(name)])


def _inproj(x, mod_l, gain, w_in):
    bsz, seq, _ = x.shape
    grid = (bsz, seq // TOK_TILE)
    w_in_bf16 = w_in.astype(BF16)
    qd0, qs0 = _proj_start("qd"), _proj_start("qs")
    dh = SWA_HEAD_DIM
    w_q = jnp.concatenate(
        [w_in[:, qd0:qd0 + DIFF_WIDTH].astype(F32) * (DIFF_HEAD_DIM ** -0.5 * LOG2E)]
        + [w_in[:, qs0 + h * dh:qs0 + (h + 1) * dh].astype(F32) * dh ** -0.5 for h in (0, 2, 1, 3)],
        axis=1).astype(BF16)
    w_t = jnp.concatenate([w_in[:, _proj_start(name):_proj_start(name) + n]
                           for name, n in zip(PROJ_NAMES, PROJ_SIZES) if name in PROJ_TRANSPOSED],
                          axis=1).astype(BF16).T
    tiles_per_step = TOK_TILE // ATT_TK
    out_specs, out_shape = [], []
    for name, n in zip(PROJ_NAMES, PROJ_SIZES):
        if name == "vd":
            out_specs.append(pl.BlockSpec((1, tiles_per_step, n, ATT_TK), lambda b, i: (b, i, 0, 0)))
            out_shape.append(jax.ShapeDtypeStruct((bsz, seq // ATT_TK, n, ATT_TK), BF16))
        elif name in PROJ_TRANSPOSED:
            out_specs.append(pl.BlockSpec((1, n, TOK_TILE), lambda b, i: (b, 0, i)))
            out_shape.append(jax.ShapeDtypeStruct((bsz, n, seq), BF16))
        else:
            out_specs.append(pl.BlockSpec((1, TOK_TILE, n), lambda b, i: (b, i, 0)))
            out_shape.append(jax.ShapeDtypeStruct((bsz, seq, n), BF16))
    return pl.pallas_call(
        _inproj_kernel,
        grid=grid,
        in_specs=[
            pl.BlockSpec((1, TOK_TILE, D_MODEL), lambda b, i: (b, i, 0)),
            pl.BlockSpec((1, 3, D_MODEL), lambda b, i: (b, 0, 0)),
            pl.BlockSpec((1, D_MODEL), lambda b, i: (0, 0)),
            pl.BlockSpec((D_MODEL, IN_COLS), lambda b, i: (0, 0)),
            pl.BlockSpec(w_q.shape, lambda b, i: (0, 0)),
            pl.BlockSpec(w_t.shape, lambda b, i: (0, 0)),
        ],
        out_specs=out_specs,
        out_shape=out_shape,
        compiler_params=pltpu.CompilerParams(
            dimension_semantics=("parallel", "parallel"), vmem_limit_bytes=VMEM_LIMIT),
        name="inproj",
    )(x, mod_l, gain.reshape(1, D_MODEL), w_in_bf16, w_q, w_t)


def _s5_tables(lam_re, lam_im, log_step, b_re, b_im, c_re, c_im):
    g, p, h = SSM_GROUPS, SSM_STATE, SSM_GROUP
    step = jnp.exp(log_step.astype(F32))[:, None]
    lr = lam_re.astype(F32)
    li = lam_im.astype(F32)
    mag = jnp.exp(lr * step)
    ang = li * step
    ab_re = mag * jnp.cos(ang)
    ab_im = mag * jnp.sin(ang)
    den = lr * lr + li * li
    f_re = ((ab_re - 1.0) * lr + ab_im * li) / den
    f_im = (ab_im * lr - (ab_re - 1.0) * li) / den
    br = b_re.astype(F32)
    bi = b_im.astype(F32)
    bb_re = f_re[..., None] * br - f_im[..., None] * bi
    bb_im = f_re[..., None] * bi + f_im[..., None] * br
    eye = jnp.eye(g, dtype=F32)
    bbd_re = jnp.einsum("gph,gk->ghkp", bb_re, eye).reshape(g * h, g * p)
    bbd_im = jnp.einsum("gph,gk->ghkp", bb_im, eye).reshape(g * h, g * p)
    bbd = jnp.concatenate([bbd_re, bbd_im], axis=1).astype(BF16)
    cbd_re = jnp.einsum("ghp,gk->gpkh", c_re.astype(F32), eye).reshape(g * p, g * h)
    cbd_im = jnp.einsum("ghp,gk->gpkh", c_im.astype(F32), eye).reshape(g * p, g * h)
    cbd = jnp.concatenate([cbd_re, -cbd_im], axis=0).astype(BF16)

    def power(n):
        n = jnp.asarray(n, F32).reshape(-1, 1, 1)
        m = jnp.exp(lr * step * n)
        return ((m * jnp.cos(ang * n)).reshape(-1, g * p), (m * jnp.sin(ang * n)).reshape(-1, g * p))

    c = S5_BLOCK // 2
    rows = jnp.arange(S5_BLOCK, dtype=F32)
    pre = power(c - rows)
    post = power(rows - c)
    row_tabs = jnp.stack([pre[0], pre[1], post[0], post[1]]).astype(BF16)
    vec = [power(float(S5_BLOCK - 1 - c)),
           power(float(S5_BLOCK)),
           power(float(c + 1))]
    vec_tabs = jnp.concatenate([t for pair in vec for t in pair], axis=0)
    return bbd, cbd, row_tabs, vec_tabs


def _cmul(a_re, a_im, b_re, b_im):
    return a_re * b_re - a_im * b_im, a_re * b_im + a_im * b_re


def _s5_kernel(u_ref, z_ref, bbd_ref, cbd_ref, tri_ref, row_ref, vec_ref, d_ref, wg_ref, bg_ref,
               o_ref, xs_ref, carry_scr):
    chunk = u_ref.shape[1]
    nb = chunk // S5_BLOCK
    n = SSM_LANES

    @pl.when(pl.program_id(1) == 0)
    def _():
        carry_scr[...] = jnp.zeros_like(carry_scr)

    streams = range(u_ref.shape[0])
    us = [u_ref[s] for s in streams]
    zss = []
    for s in streams:
        bu = jnp.dot(us[s], bbd_ref[...], preferred_element_type=F32)
        bu = bu.astype(BF16).reshape(nb, S5_BLOCK, 2 * n)
        z_re, z_im = _cmul(row_ref[0], row_ref[1], bu[:, :, :n], bu[:, :, n:])
        zss.append(jnp.concatenate([z_re, z_im], axis=-1).reshape(chunk, 2 * n))
    for s in streams:
        xs_ref[s] = jnp.dot(tri_ref[...], zss[s], preferred_element_type=F32)

    sts = []
    for s in streams:
        ends = jnp.concatenate([xs_ref[s, (k + 1) * S5_BLOCK - 1:(k + 1) * S5_BLOCK, :]
                                for k in range(nb)], axis=0)
        e_re, e_im = _cmul(vec_ref[0:1], vec_ref[1:2], ends[:, :n], ends[:, n:])
        s_re = carry_scr[s, 0:1, :]
        s_im = carry_scr[s, 1:2, :]
        inj_re, inj_im = [], []
        for k in range(nb):
            g_re, g_im = _cmul(vec_ref[4:5], vec_ref[5:6], s_re, s_im)
            inj_re.append(g_re)
            inj_im.append(g_im)
            d_re, d_im = _cmul(vec_ref[2:3], vec_ref[3:4], s_re, s_im)
            s_re = e_re[k:k + 1] + d_re
            s_im = e_im[k:k + 1] + d_im
        carry_scr[s, 0:1, :] = s_re
        carry_scr[s, 1:2, :] = s_im
        inj_re = jnp.concatenate(inj_re, axis=0)[:, None, :]
        inj_im = jnp.concatenate(inj_im, axis=0)[:, None, :]
        xs = xs_ref[s].reshape(nb, S5_BLOCK, 2 * n)
        x_re, x_im = _cmul(row_ref[2], row_ref[3], (xs[:, :, :n] + inj_re).astype(BF16),
                           (xs[:, :, n:] + inj_im).astype(BF16))
        sts.append(jnp.concatenate([x_re, x_im], axis=-1).reshape(chunk, 2 * n))

    ys = []
    for s in streams:
        y = jnp.dot(sts[s], cbd_ref[...], preferred_element_type=F32)
        y = y + d_ref[...] * us[s].astype(F32)
        ys.append(jax.nn.gelu(y))
    for s in streams:
        gate = jnp.dot(ys[s].astype(BF16), wg_ref[...], preferred_element_type=F32) + bg_ref[...]
        y = ys[s] * jax.nn.sigmoid(gate)
        o_ref[s] = (y * _silu(z_ref[s].astype(F32))).astype(BF16)


def _s5(u, z, tables, d_skip, w_glu, b_glu):
    bsz, seq, _ = u.shape
    bbd, cbd, row_tabs, vec_tabs = tables
    t = np.arange(S5_CHUNK, dtype=np.int32)
    tri = np.logical_and(t[:, None] // S5_BLOCK == t[None, :] // S5_BLOCK,
                         t[:, None] >= t[None, :]).astype(BF16)
    full = lambda *shape: pl.BlockSpec(shape, lambda b, i: (0,) * len(shape))
    return pl.pallas_call(
        _s5_kernel,
        grid=(bsz // S5_STREAMS, seq // S5_CHUNK),
        in_specs=[
            pl.BlockSpec((S5_STREAMS, S5_CHUNK, SSM_WIDTH), lambda b, i: (b, i, 0)),
            pl.BlockSpec((S5_STREAMS, S5_CHUNK, SSM_WIDTH), lambda b, i: (b, i, 0)),
            full(SSM_WIDTH, 2 * SSM_LANES),
            full(2 * SSM_LANES, SSM_WIDTH),
            full(S5_CHUNK, S5_CHUNK),
            full(*row_tabs.shape),
            full(*vec_tabs.shape),
            full(1, SSM_WIDTH),
            full(SSM_WIDTH, SSM_WIDTH),
            full(1, SSM_WIDTH),
        ],
        out_specs=pl.BlockSpec((S5_STREAMS, S5_CHUNK, SSM_WIDTH), lambda b, i: (b, i, 0)),
        out_shape=jax.ShapeDtypeStruct((bsz, seq, SSM_WIDTH), BF16),
        scratch_shapes=[
            pltpu.VMEM((S5_STREAMS, S5_CHUNK, 2 * SSM_LANES), F32),
            pltpu.VMEM((S5_STREAMS, SUBLANES, SSM_LANES), F32),
        ],
        compiler_params=pltpu.CompilerParams(
            dimension_semantics=("parallel", "arbitrary"), vmem_limit_bytes=VMEM_LIMIT),
        name="s5",
    )(u, z, bbd, cbd, tri, row_tabs, vec_tabs, d_skip.reshape(1, SSM_WIDTH).astype(F32),
      w_glu.astype(BF16), b_glu.reshape(1, SSM_WIDTH).astype(F32))


ACC_ROWS = 2 * DIFF_HEAD_DIM + 16
POS_SPLIT = 64


LOG2E_PARTS = 3


def _diff_features(seq):
    parts, rest = [], LOG2E
    for _ in range(LOG2E_PARTS):
        part = float(np.float32(rest).astype(BF16))
        parts.append(part)
        rest -= part
    pos = np.arange(seq, dtype=np.int32)
    hi = ((pos // POS_SPLIT) * POS_SPLIT).astype(np.float32)
    lo = (pos % POS_SPLIT).astype(np.float32)
    lane = (np.arange(2 * DIFF_HEAD_DIM, dtype=np.int32) % DIFF_HEAD_DIM)[None, :]
    qf = sum(np.where(lane % LOG2E_PARTS == i, np.float32(part), np.float32(0)) for i, part in enumerate(parts))
    qf = np.where(lane < 2 * LOG2E_PARTS, qf, np.float32(0)).astype(BF16)
    tabs = []
    for h in range(DIFF_HEADS):
        slope = np.float32(_alibi_slope(SWA_Q_HEADS + h))
        tabs.append(np.where(lane < LOG2E_PARTS, slope * hi[:, None], np.float32(0))
                    + np.where(np.logical_and(lane >= LOG2E_PARTS, lane < 2 * LOG2E_PARTS),
                               slope * lo[:, None], np.float32(0)))
    return np.stack(tabs).astype(BF16), qf


def _diff_kernel(q_ref, k_ref, kf_ref, qf_ref, vt_ref, zt_ref, lam_ref, gain_ref, o_ref,
                 acc_ref, s_ref, qp_ref, m_ref, msub_ref, alpha_ref, *, lam_init):
    qi = pl.program_id(1)
    dh = DIFF_HEAD_DIM
    hw = 2 * dh
    lane_q = lax.broadcasted_iota(jnp.int32, (ATT_TQ, hw), 1)
    q_feat = jnp.broadcast_to(qf_ref[...], (ATT_TQ, hw))
    units = [(b, h) for b in range(q_ref.shape[0]) for h in range(DIFF_HEADS)]
    for u, (b, h) in enumerate(units):
        q = q_ref[b, :, h * hw:(h + 1) * hw]
        qp_ref[2 * u] = jnp.where(lane_q < dh, q, q_feat)
        qp_ref[2 * u + 1] = jnp.where(lane_q >= dh, q, q_feat)
    lane_k = lax.broadcasted_iota(jnp.int32, (ATT_TK, hw), 1)
    ones_row = jnp.where(lax.broadcasted_iota(jnp.int32, (ACC_ROWS - hw, ATT_TK), 0) == 0,
                         1.0, 0.0).astype(BF16)
    krow = lax.broadcasted_iota(jnp.int32, (ATT_TK, ATT_TQ), 0)
    qcol = lax.broadcasted_iota(jnp.int32, (ATT_TK, ATT_TQ), 1)
    nt = (((1,), (1,)), ((), ()))

    acc_ref[...] = jnp.zeros_like(acc_ref)
    m_ref[...] = jnp.full(m_ref.shape, -jnp.inf, F32)

    def scores(t, buf, masked):
        k0 = pl.multiple_of(t * ATT_TK, ATT_TK)
        for u, (b, h) in enumerate(units):
            k = k_ref[b, pl.ds(k0, ATT_TK), h * hw:(h + 1) * hw]
            kf = kf_ref[h, pl.ds(k0, ATT_TK), :]
            for c, kp in ((2 * u, jnp.where(lane_k < dh, k, kf)),
                          (2 * u + 1, jnp.where(lane_k >= dh, k, kf))):
                s = lax.dot_general(kp, qp_ref[c], nt, preferred_element_type=F32)
                if masked:
                    s = jnp.where(krow <= qcol, s, -jnp.inf)
                s_ref[buf, c] = s
                m_old = m_ref[c]
                m_new = jnp.maximum(m_old, jnp.max(s, axis=0, keepdims=True))
                m_ref[c] = m_new
                msub_ref[buf, c] = m_new
                alpha_ref[buf, c] = jnp.exp2(m_old - m_new)

    def values(t, buf):
        for u, (b, h) in enumerate(units):
            vta = jnp.concatenate([vt_ref[b, t, h * hw:(h + 1) * hw, :], ones_row], axis=0)
            for c in (2 * u, 2 * u + 1):
                p = jnp.exp2(s_ref[buf, c] - msub_ref[buf, c]).astype(BF16)
                acc_ref[c] = (alpha_ref[buf, c] * acc_ref[c]
                              + jnp.dot(vta, p, preferred_element_type=F32))

    pairs = jnp.maximum(qi - 1, 0) // 2
    rest = qi - 1 - 2 * pairs
    t0 = 2 * pairs

    @pl.when(qi == 0)
    def _():
        scores(0, 0, True)
        values(0, 0)

    @pl.when(qi >= 1)
    def _():
        scores(0, 0, False)

    def two_tiles(i, carry):
        t = 2 * i
        scores(t + 1, 1, False)
        values(t, 0)
        scores(t + 2, 0, False)
        values(t + 1, 1)
        return carry

    lax.fori_loop(0, pairs, two_tiles, 0)

    @pl.when(jnp.logical_and(qi >= 1, rest == 1))
    def _():
        scores(t0 + 1, 1, False)
        values(t0, 0)
        scores(t0 + 2, 0, True)
        values(t0 + 1, 1)
        values(t0 + 2, 0)

    @pl.when(jnp.logical_and(qi >= 1, rest == 0))
    def _():
        scores(t0 + 1, 1, True)
        values(t0, 0)
        values(t0 + 1, 1)

    lp = lam_ref[...]
    lam = (jnp.exp(jnp.sum(lp[0:1] * lp[1:2], axis=-1, keepdims=True))
           - jnp.exp(jnp.sum(lp[2:3] * lp[3:4], axis=-1, keepdims=True)) + lam_init)
    gain = gain_ref[...] * (1.0 - lam_init)
    for u, (b, h) in enumerate(units):
        a1 = acc_ref[2 * u]
        a2 = acc_ref[2 * u + 1]
        o_t = (a1[:hw] * (1.0 / a1[hw:hw + 1]) - lam * (a2[:hw] * (1.0 / a2[hw:hw + 1])))
        o_t = o_t * lax.rsqrt(jnp.mean(o_t * o_t, axis=0, keepdims=True) + RMS_EPS) * gain
        z_t = zt_ref[b, h * hw:(h + 1) * hw, :].astype(F32)
        o_ref[b, h * hw:(h + 1) * hw, :] = (o_t * _silu(z_t)).astype(BF16)


def _diff_attention(qd, kd, vd_t, zd_t, lam_params, subln_gain, lam_init):
    bsz, seq, _ = qd.shape
    hw = 2 * DIFF_HEAD_DIM
    n_kv = seq // ATT_TK
    nb = DIFF_STREAMS
    chains = 2 * DIFF_HEADS * nb
    kf, qf = _diff_features(seq)
    gain_rows = jnp.broadcast_to(subln_gain.astype(F32).reshape(hw, 1), (hw, ATT_TQ))
    tile_t = pl.BlockSpec((nb, DIFF_WIDTH, ATT_TQ), lambda b, i: (b, 0, i))
    return pl.pallas_call(
        functools.partial(_diff_kernel, lam_init=lam_init),
        grid=(bsz // nb, seq // ATT_TQ),
        in_specs=[
            pl.BlockSpec((nb, ATT_TQ, DIFF_WIDTH), lambda b, i: (b, i, 0)),
            pl.BlockSpec((nb, seq, DIFF_WIDTH), lambda b, i: (b, 0, 0)),
            pl.BlockSpec((DIFF_HEADS, seq, hw), lambda b, i: (0, 0, 0)),
            pl.BlockSpec((1, hw), lambda b, i: (0, 0)),
            pl.BlockSpec((nb, n_kv, DIFF_WIDTH, ATT_TK), lambda b, i: (b, 0, 0, 0)),
            tile_t,
            pl.BlockSpec((4, DIFF_HEAD_DIM), lambda b, i: (0, 0)),
            pl.BlockSpec((hw, ATT_TQ), lambda b, i: (0, 0)),
        ],
        out_specs=tile_t,
        out_shape=jax.ShapeDtypeStruct((bsz, DIFF_WIDTH, seq), BF16),
        scratch_shapes=[pltpu.VMEM((chains, ACC_ROWS, ATT_TQ), F32),
                        pltpu.VMEM((2, chains, ATT_TK, ATT_TQ), F32),
                        pltpu.VMEM((chains, ATT_TQ, hw), BF16),
                        pltpu.VMEM((chains, 1, ATT_TQ), F32),
                        pltpu.VMEM((2, chains, 1, ATT_TQ), F32),
                        pltpu.VMEM((2, chains, 1, ATT_TQ), F32)],
        compiler_params=pltpu.CompilerParams(
            dimension_semantics=("parallel", "parallel"), vmem_limit_bytes=VMEM_LIMIT),
        name="diffattn",
    )(qd, kd, kf, qf, vd_t, zd_t, lam_params, gain_rows)


SWA_STEP_BLOCKS = 4
SWA_ACC_ROWS = SWA_HEAD_DIM + 16


def _swa_features():
    w = WINDOW
    zero = np.float32(0)
    lane = np.arange(LANES, dtype=np.int32)[None, :]
    r = np.arange(2 * w, dtype=np.float32)[:, None]
    kf = np.where(lane == 0, r, zero) + np.where(lane == 1, np.float32(1), zero)
    c = np.arange(w, dtype=np.float32)[:, None]
    qf = np.stack([np.where(lane == 0, np.float32(_alibi_slope(h)), zero)
                   + np.where(lane == 1, np.float32(-_alibi_slope(h)) * (w + c), zero)
                   for h in range(SWA_Q_HEADS)])
    return kf.astype(BF16), qf.astype(BF16)


def _swa_kernel(q_ref, kp_ref, kc_ref, vp_ref, vc_ref, z_ref, sink_ref, kf_ref, qf_ref, o_ref, s_ref):
    i = pl.program_id(1)
    w = WINDOW
    dh = SWA_HEAD_DIM
    nt = (((1,), (1,)), ((), ()))
    lane = lax.broadcasted_iota(jnp.int32, (w, 2 * dh), 1)
    keys = jnp.concatenate([kp_ref[0], kc_ref[0]], axis=0)
    v_t = jnp.concatenate([vp_ref[0], vc_ref[0]], axis=1)
    kf = kf_ref[...]
    ones_row = jnp.where(lax.broadcasted_iota(jnp.int32, (SWA_ACC_ROWS - dh, 2 * w), 0) == 0,
                         1.0, 0.0).astype(BF16)
    r = lax.broadcasted_iota(jnp.int32, (2 * w, w), 0)
    c = lax.broadcasted_iota(jnp.int32, (2 * w, w), 1)
    band = jnp.logical_and(r > c, r <= c + w)
    first = jnp.logical_and(band, jnp.logical_or(r >= w, i > 0))

    for n in range(SWA_STEP_BLOCKS):
        q = q_ref[0, n * w:(n + 1) * w, :]
        qa, qb = q[:, :2 * dh], q[:, 2 * dh:]
        zero = jnp.zeros_like(qa)
        heads = (jnp.where(lane < dh, qa, zero), jnp.where(lane < dh, qb, zero),
                 jnp.where(lane >= dh, qa, zero), jnp.where(lane >= dh, qb, zero))
        qpp = jnp.concatenate([jnp.concatenate([heads[h], qf_ref[h]], axis=1)
                               for h in range(SWA_Q_HEADS)], axis=0)
        kpp = jnp.concatenate([keys[n * w:(n + 2) * w], kf], axis=1)
        s_ref[n] = lax.dot_general(kpp, qpp, nt, preferred_element_type=F32)

    for n in range(SWA_STEP_BLOCKS):
        valid = first if n == 0 else band
        outs = []
        for kh in range(SWA_KV_HEADS):
            vta = jnp.concatenate([v_t[kh * dh:(kh + 1) * dh, n * w:(n + 2) * w], ones_row], axis=0)
            es, ms = [], []
            for g in range(SWA_GROUP):
                h = SWA_GROUP * kh + g
                s = jnp.where(valid, s_ref[n, :, h * w:(h + 1) * w], -jnp.inf)
                m = jnp.maximum(jnp.max(s, axis=0, keepdims=True), sink_ref[h][:, :1])
                es.append(jnp.exp(s - m).astype(BF16))
                ms.append(m)
            acc = jnp.dot(vta, jnp.concatenate(es, axis=1), preferred_element_type=F32)
            for g in range(SWA_GROUP):
                h = SWA_GROUP * kh + g
                denom = acc[dh:dh + 1, g * w:(g + 1) * w] + jnp.exp(sink_ref[h][:, :1] - ms[g])
                outs.append(acc[:dh, g * w:(g + 1) * w] * (1.0 / denom))
        o = jnp.concatenate(outs, axis=0).T
        z = z_ref[0, n * w:(n + 1) * w, :].astype(F32)
        o_ref[0, n * w:(n + 1) * w, :] = (o * _silu(z)).astype(BF16)


def _swa(qs, ks, vs_t, zs, sinks):
    bsz, seq, _ = qs.shape
    w = WINDOW
    n = SWA_STEP_BLOCKS
    sink_rows = jnp.broadcast_to(sinks.astype(F32).reshape(SWA_Q_HEADS, 1, 1), (SWA_Q_HEADS, 1, LANES))
    kf, qf = _swa_features()
    qspec = pl.BlockSpec((1, n * w, SWA_WIDTH), lambda b, i: (b, i, 0))
    prev_block = lambda i: jnp.maximum(i * n - 1, 0)
    return pl.pallas_call(
        _swa_kernel,
        grid=(bsz, seq // (n * w)),
        in_specs=[
            qspec,
            pl.BlockSpec((1, w, SWA_KV_WIDTH), lambda b, i: (b, prev_block(i), 0)),
            pl.BlockSpec((1, n * w, SWA_KV_WIDTH), lambda b, i: (b, i, 0)),
            pl.BlockSpec((1, SWA_KV_WIDTH, w), lambda b, i: (b, 0, prev_block(i))),
            pl.BlockSpec((1, SWA_KV_WIDTH, n * w), lambda b, i: (b, 0, i)),
            qspec,
            pl.BlockSpec((SWA_Q_HEADS, 1, LANES), lambda b, i: (0, 0, 0)),
            pl.BlockSpec((2 * w, LANES), lambda b, i: (0, 0)),
            pl.BlockSpec((SWA_Q_HEADS, w, LANES), lambda b, i: (0, 0, 0)),
        ],
        out_specs=qspec,
        out_shape=jax.ShapeDtypeStruct((bsz, seq, SWA_WIDTH), BF16),
        scratch_shapes=[pltpu.VMEM((n, 2 * w, SWA_Q_HEADS * w), F32)],
        compiler_params=pltpu.CompilerParams(
            dimension_semantics=("parallel", "parallel"), vmem_limit_bytes=VMEM_LIMIT),
        name="swa",
    )(qs, ks, ks, vs_t, vs_t, zs, sink_rows, kf, qf)


def _outproj_kernel(x_ref, ys_ref, yd_ref, yw_ref, mod_ref, w_ref, fg_ref, o_ref, *, final_norm):
    a = SSM_WIDTH
    b = SSM_WIDTH + DIFF_WIDTH
    y = (jnp.dot(ys_ref[0], w_ref[:a, :], preferred_element_type=F32)
         + lax.dot_general(yd_ref[0], w_ref[a:b, :], (((0,), (0,)), ((), ())),
                           preferred_element_type=F32)
         + jnp.dot(yw_ref[0], w_ref[b:, :], preferred_element_type=F32))
    out = x_ref[0] + mod_ref[0][2:3, :] * y
    if final_norm:
        out = out * lax.rsqrt(jnp.mean(out * out, axis=-1, keepdims=True) + RMS_EPS) * fg_ref[...]
    o_ref[0] = out


def _outproj(x, y_ssm, y_diff, y_swa, mod_l, w_out_bf16, final_gain, final_norm):
    bsz, seq, _ = x.shape
    tok = lambda n: pl.BlockSpec((1, OUT_TILE, n), lambda b, i: (b, i, 0))
    return pl.pallas_call(
        functools.partial(_outproj_kernel, final_norm=final_norm),
        grid=(bsz, seq // OUT_TILE),
        in_specs=[
            tok(D_MODEL), tok(SSM_WIDTH),
            pl.BlockSpec((1, DIFF_WIDTH, OUT_TILE), lambda b, i: (b, 0, i)),
            tok(SWA_WIDTH),
            pl.BlockSpec((1, 3, D_MODEL), lambda b, i: (b, 0, 0)),
            pl.BlockSpec((D_MODEL, D_MODEL), lambda b, i: (0, 0)),
            pl.BlockSpec((1, D_MODEL), lambda b, i: (0, 0)),
        ],
        out_specs=tok(D_MODEL),
        out_shape=jax.ShapeDtypeStruct((bsz, seq, D_MODEL), F32),
        compiler_params=pltpu.CompilerParams(
            dimension_semantics=("parallel", "parallel"), vmem_limit_bytes=VMEM_LIMIT),
        name="outproj",
    )(x, y_ssm, y_diff, y_swa, mod_l, w_out_bf16, final_gain.reshape(1, D_MODEL).astype(F32))


def kernel(x, c, norm_gain, ada_w, ada_b, w_in, w_out, ssm_lam_re, ssm_lam_im, ssm_log_step,
           ssm_b_re, ssm_b_im, ssm_c_re, ssm_c_im, ssm_d, glu_w, glu_b,
           diff_lq1, diff_lk1, diff_lq2, diff_lk2, diff_subln, swa_sinks, final_gain):
    bsz = x.shape[0]
    mod = _ada(c, ada_w, ada_b).reshape(DEPTH, bsz, 3, D_MODEL)
    s5_tables = jax.vmap(_s5_tables)(ssm_lam_re, ssm_lam_im, ssm_log_step,
                                     ssm_b_re, ssm_b_im, ssm_c_re, ssm_c_im)
    for l in range(DEPTH):
        proj = dict(zip(PROJ_NAMES, _inproj(x, mod[l], norm_gain[l], w_in[l])))
        tables = tuple(t[l] for t in s5_tables)
        y_ssm = _s5(proj["u"], proj["z_ssm"], tables, ssm_d[l], glu_w[l], glu_b[l])
        lam_init = 0.8 - 0.6 * math.exp(-0.3 * l)
        lam_params = jnp.stack([diff_lq1[l], diff_lk1[l], diff_lq2[l], diff_lk2[l]]).astype(F32)
        y_diff = _diff_attention(proj["qd"], proj["kd"], proj["vd"], proj["z_diff"],
                                 lam_params, diff_subln[l], lam_init)
        y_swa = _swa(proj["qs"], proj["ks"], proj["vs"], proj["z_swa"], swa_sinks[l])
        x = _outproj(x, y_ssm, y_diff, y_swa, mod[l], w_out[l].astype(BF16), final_gain,
                     final_norm=(l == DEPTH - 1))
    return x
```

```python
import functools
import math

import jax
import jax.numpy as jnp
import numpy as np
from jax import lax
from jax.experimental import pallas as pl
from jax.experimental.pallas import tpu as pltpu

F32 = jnp.float32
BF16 = jnp.bfloat16

D_MODEL = 1024
DEPTH = 2
SSM_GROUP = 16
SSM_GROUPS = 16
SSM_WIDTH = 256
SSM_STATE = 64
SSM_LANES = SSM_GROUPS * SSM_STATE
DIFF_HEADS = 4
DIFF_HEAD_DIM = 64
DIFF_WIDTH = 512
SWA_Q_HEADS = 4
SWA_KV_HEADS = 2
SWA_GROUP = 2
SWA_HEAD_DIM = 64
SWA_WIDTH = 256
SWA_KV_WIDTH = 128
WINDOW = 128
N_ATTN_HEADS = 8
RMS_EPS = 1e-6

PROJ_NAMES = ("u", "z_ssm", "qd", "kd", "vd", "z_diff", "qs", "ks", "vs", "z_swa")
PROJ_SIZES = (SSM_WIDTH, SSM_WIDTH, DIFF_WIDTH, DIFF_WIDTH, DIFF_WIDTH, DIFF_WIDTH,
              SWA_WIDTH, SWA_KV_WIDTH, SWA_KV_WIDTH, SWA_WIDTH)
IN_COLS = sum(PROJ_SIZES)
PROJ_TRANSPOSED = ("vd", "z_diff", "vs")
INPROJ_OUTPUTS = ("y_ssm",) + tuple(n for n in PROJ_NAMES if n not in ("u", "z_ssm"))
LOG2E = math.log2(math.e)

SUBLANES = 8
LANES = 128
VMEM_LIMIT = 48 * 1024 * 1024

TOK_TILE = 512
OUT_TILE = 1024
S5_CHUNK = 256
S5_BLOCK = 32
ATT_TQ = 256
ATT_TK = 256
DIFF_STREAMS = 2


def _silu(x):
    return x * jax.nn.sigmoid(x)


def _alibi_slope(head_index):
    return 2.0 ** (-(head_index + 1) * (8.0 / N_ATTN_HEADS))


def _ada_kernel(c_ref, w_ref, b_ref, o_ref):
    cond = _silu(c_ref[...])
    o_ref[0] = jnp.dot(cond, w_ref[0], preferred_element_type=F32) + b_ref[0]


def _ada(c, ada_w, ada_b):
    bsz = c.shape[0]
    col = D_MODEL
    return pl.pallas_call(
        _ada_kernel,
        grid=(DEPTH, 3),
        in_specs=[
            pl.BlockSpec((bsz, D_MODEL), lambda l, j: (0, 0)),
            pl.BlockSpec((1, D_MODEL, col), lambda l, j: (l, 0, j)),
            pl.BlockSpec((1, 1, col), lambda l, j: (l, 0, j)),
        ],
        out_specs=pl.BlockSpec((1, bsz, col), lambda l, j: (l, 0, j)),
        out_shape=jax.ShapeDtypeStruct((DEPTH, bsz, 3 * D_MODEL), F32),
        compiler_params=pltpu.CompilerParams(
            dimension_semantics=("parallel", "parallel"), vmem_limit_bytes=VMEM_LIMIT),
        name="ada",
    )(c, ada_w, ada_b.reshape(DEPTH, 1, 3 * D_MODEL))


def _cmul(a_re, a_im, b_re, b_im):
    return a_re * b_re - a_im * b_im, a_re * b_im + a_im * b_re


def _proj_start(name):
    return sum(PROJ_SIZES[:PROJ_NAMES.index(name)])


def _proj_size(name):
    return PROJ_SIZES[PROJ_NAMES.index(name)]


def _inproj_kernel(x_ref, mod_ref, g_ref, w_ref, wq_ref, wt_ref,
                   bbd_ref, cbd_ref, tri_ref, row_ref, vec_ref, d_ref, wg_ref, bg_ref,
                   *refs):
    out_refs = dict(zip(INPROJ_OUTPUTS, refs[:len(INPROJ_OUTPUTS)]))
    uz_ref, xs_ref, carry_scr = refs[len(INPROJ_OUTPUTS):]
    n = SSM_LANES
    nb = S5_CHUNK // S5_BLOCK
    nt = (((1,), (1,)), ((), ()))

    @pl.when(pl.program_id(1) == 0)
    def _():
        carry_scr[...] = jnp.zeros_like(carry_scr)

    xf = x_ref[0]
    y = xf * lax.rsqrt(jnp.mean(xf * xf, axis=-1, keepdims=True) + RMS_EPS) * g_ref[...]
    mod = mod_ref[0]
    h = (y * (1.0 + mod[1:2, :]) + mod[0:1, :]).astype(BF16)

    def project(name):
        o_ref = out_refs[name]
        size = _proj_size(name)
        if name in PROJ_TRANSPOSED:
            r0 = sum(_proj_size(t) for t in PROJ_TRANSPOSED[:PROJ_TRANSPOSED.index(name)])
            rows = lax.dot_general(wt_ref[r0:r0 + size, :], h, nt,
                                   preferred_element_type=F32).astype(BF16)
            if name == "vd":
                for t in range(TOK_TILE // ATT_TK):
                    o_ref[0, t] = rows[:, t * ATT_TK:(t + 1) * ATT_TK]
            else:
                o_ref[0] = rows
        elif name == "qd":
            o_ref[0] = jnp.dot(h, wq_ref[:, :size], preferred_element_type=F32).astype(BF16)
        elif name == "qs":
            o_ref[0] = jnp.dot(h, wq_ref[:, DIFF_WIDTH:], preferred_element_type=F32).astype(BF16)
        else:
            c0 = _proj_start(name)
            o_ref[0] = jnp.dot(h, w_ref[:, c0:c0 + size], preferred_element_type=F32).astype(BF16)

    uz_ref[...] = jnp.dot(h, w_ref[:, :2 * SSM_WIDTH], preferred_element_type=F32)

    pieces = iter(("vd", "z_diff", "qd", "kd", "qs", "ks", "z_swa", "vs"))
    for ci in range(TOK_TILE // S5_CHUNK):
        rows = slice(ci * S5_CHUNK, (ci + 1) * S5_CHUNK)
        u = uz_ref[rows, :SSM_WIDTH].astype(BF16)
        bu = jnp.dot(u, bbd_ref[...], preferred_element_type=F32)
        bu = bu.astype(BF16).reshape(nb, S5_BLOCK, 2 * n)
        z_re, z_im = _cmul(row_ref[0], row_ref[1], bu[:, :, :n], bu[:, :, n:])
        zs = jnp.concatenate([z_re, z_im], axis=-1).reshape(S5_CHUNK, 2 * n)
        project(next(pieces))

        xs_ref[...] = jnp.dot(tri_ref[...], zs, preferred_element_type=F32)
        ends = jnp.concatenate([xs_ref[(k + 1) * S5_BLOCK - 1:(k + 1) * S5_BLOCK, :]
                                for k in range(nb)], axis=0)
        e_re, e_im = _cmul(vec_ref[0:1], vec_ref[1:2], ends[:, :n], ends[:, n:])
        s_re = carry_scr[0:1, :]
        s_im = carry_scr[1:2, :]
        inj_re, inj_im = [], []
        for k in range(nb):
            g_re, g_im = _cmul(vec_ref[4:5], vec_ref[5:6], s_re, s_im)
            inj_re.append(g_re)
            inj_im.append(g_im)
            d_re, d_im = _cmul(vec_ref[2:3], vec_ref[3:4], s_re, s_im)
            s_re = e_re[k:k + 1] + d_re
            s_im = e_im[k:k + 1] + d_im
        carry_scr[0:1, :] = s_re
        carry_scr[1:2, :] = s_im
        inj_re = jnp.concatenate(inj_re, axis=0)[:, None, :]
        inj_im = jnp.concatenate(inj_im, axis=0)[:, None, :]
        xs = xs_ref[...].reshape(nb, S5_BLOCK, 2 * n)
        x_re, x_im = _cmul(row_ref[2], row_ref[3], (xs[:, :, :n] + inj_re).astype(BF16),
                           (xs[:, :, n:] + inj_im).astype(BF16))
        st = jnp.concatenate([x_re, x_im], axis=-1).reshape(S5_CHUNK, 2 * n)
        project(next(pieces))

        ys = jnp.dot(st, cbd_ref[...], preferred_element_type=F32)
        ys = jax.nn.gelu(ys + d_ref[...] * u.astype(F32))
        project(next(pieces))

        gate = jnp.dot(ys.astype(BF16), wg_ref[...], preferred_element_type=F32) + bg_ref[...]
        ys = ys * jax.nn.sigmoid(gate)
        out_refs["y_ssm"][0, rows, :] = (ys * _silu(uz_ref[rows, SSM_WIDTH:])).astype(BF16)
        project(next(pieces))


def _inproj(x, mod_l, gain, w_in, s5_tables, d_skip, w_glu, b_glu):
    bsz, seq, _ = x.shape
    grid = (bsz, seq // TOK_TILE)
    w_in_bf16 = w_in.astype(BF16)
    qd0, qs0 = _proj_start("qd"), _proj_start("qs")
    dh = SWA_HEAD_DIM
    w_q = jnp.concatenate(
        [w_in[:, qd0:qd0 + DIFF_WIDTH].astype(F32) * (DIFF_HEAD_DIM ** -0.5 * LOG2E)]
        + [w_in[:, qs0 + h * dh:qs0 + (h + 1) * dh].astype(F32) * dh ** -0.5 for h in (0, 2, 1, 3)],
        axis=1).astype(BF16)
    w_t = jnp.concatenate([w_in[:, _proj_start(name):_proj_start(name) + n]
                           for name, n in zip(PROJ_NAMES, PROJ_SIZES) if name in PROJ_TRANSPOSED],
                          axis=1).astype(BF16).T
    bbd, cbd, row_tabs, vec_tabs = s5_tables
    t = np.arange(S5_CHUNK, dtype=np.int32)
    tri = np.logical_and(t[:, None] // S5_BLOCK == t[None, :] // S5_BLOCK,
                         t[:, None] >= t[None, :]).astype(BF16)
    tiles_per_step = TOK_TILE // ATT_TK
    out_specs, out_shape = [], []
    for name in INPROJ_OUTPUTS:
        n = SSM_WIDTH if name == "y_ssm" else _proj_size(name)
        if name == "vd":
            out_specs.append(pl.BlockSpec((1, tiles_per_step, n, ATT_TK), lambda b, i: (b, i, 0, 0)))
            out_shape.append(jax.ShapeDtypeStruct((bsz, seq // ATT_TK, n, ATT_TK), BF16))
        elif name in PROJ_TRANSPOSED:
            out_specs.append(pl.BlockSpec((1, n, TOK_TILE), lambda b, i: (b, 0, i)))
            out_shape.append(jax.ShapeDtypeStruct((bsz, n, seq), BF16))
        else:
            out_specs.append(pl.BlockSpec((1, TOK_TILE, n), lambda b, i: (b, i, 0)))
            out_shape.append(jax.ShapeDtypeStruct((bsz, seq, n), BF16))
    full = lambda a: pl.BlockSpec(a.shape, lambda b, i: (0,) * a.ndim)
    consts = (gain.reshape(1, D_MODEL), w_in_bf16, w_q, w_t, bbd, cbd, tri, row_tabs, vec_tabs,
              d_skip.reshape(1, SSM_WIDTH).astype(F32), w_glu.astype(BF16),
              b_glu.reshape(1, SSM_WIDTH).astype(F32))
    outs = pl.pallas_call(
        _inproj_kernel,
        grid=grid,
        in_specs=[
            pl.BlockSpec((1, TOK_TILE, D_MODEL), lambda b, i: (b, i, 0)),
            pl.BlockSpec((1, 3, D_MODEL), lambda b, i: (b, 0, 0)),
        ] + [full(a) for a in consts],
        out_specs=out_specs,
        out_shape=out_shape,
        scratch_shapes=[
            pltpu.VMEM((TOK_TILE, 2 * SSM_WIDTH), F32),
            pltpu.VMEM((S5_CHUNK, 2 * SSM_LANES), F32),
            pltpu.VMEM((SUBLANES, SSM_LANES), F32),
        ],
        compiler_params=pltpu.CompilerParams(
            dimension_semantics=("parallel", "arbitrary"), vmem_limit_bytes=VMEM_LIMIT),
        name="inproj",
    )(x, mod_l, *consts)
    return dict(zip(INPROJ_OUTPUTS, outs))


def _s5_tables(lam_re, lam_im, log_step, b_re, b_im, c_re, c_im):
    g, p, h = SSM_GROUPS, SSM_STATE, SSM_GROUP
    step = jnp.exp(log_step.astype(F32))[:, None]
    lr = lam_re.astype(F32)
    li = lam_im.astype(F32)
    mag = jnp.exp(lr * step)
    ang = li * step
    ab_re = mag * jnp.cos(ang)
    ab_im = mag * jnp.sin(ang)
    den = lr * lr + li * li
    f_re = ((ab_re - 1.0) * lr + ab_im * li) / den
    f_im = (ab_im * lr - (ab_re - 1.0) * li) / den
    br = b_re.astype(F32)
    bi = b_im.astype(F32)
    bb_re = f_re[..., None] * br - f_im[..., None] * bi
    bb_im = f_re[..., None] * bi + f_im[..., None] * br
    eye = jnp.eye(g, dtype=F32)
    bbd_re = jnp.einsum("gph,gk->ghkp", bb_re, eye).reshape(g * h, g * p)
    bbd_im = jnp.einsum("gph,gk->ghkp", bb_im, eye).reshape(g * h, g * p)
    bbd = jnp.concatenate([bbd_re, bbd_im], axis=1).astype(BF16)
    cbd_re = jnp.einsum("ghp,gk->gpkh", c_re.astype(F32), eye).reshape(g * p, g * h)
    cbd_im = jnp.einsum("ghp,gk->gpkh", c_im.astype(F32), eye).reshape(g * p, g * h)
    cbd = jnp.concatenate([cbd_re, -cbd_im], axis=0).astype(BF16)

    def power(n):
        n = jnp.asarray(n, F32).reshape(-1, 1, 1)
        m = jnp.exp(lr * step * n)
        return ((m * jnp.cos(ang * n)).reshape(-1, g * p), (m * jnp.sin(ang * n)).reshape(-1, g * p))

    c = S5_BLOCK // 2
    rows = jnp.arange(S5_BLOCK, dtype=F32)
    pre = power(c - rows)
    post = power(rows - c)
    row_tabs = jnp.stack([pre[0], pre[1], post[0], post[1]]).astype(BF16)
    vec = [power(float(S5_BLOCK - 1 - c)),
           power(float(S5_BLOCK)),
           power(float(c + 1))]
    vec_tabs = jnp.concatenate([t for pair in vec for t in pair], axis=0)
    return bbd, cbd, row_tabs, vec_tabs


ACC_ROWS = 2 * DIFF_HEAD_DIM + 16
POS_SPLIT = 64


LOG2E_PARTS = 3


def _diff_features(seq):
    parts, rest = [], LOG2E
    for _ in range(LOG2E_PARTS):
        part = float(np.float32(rest).astype(BF16))
        parts.append(part)
        rest -= part
    pos = np.arange(seq, dtype=np.int32)
    hi = ((pos // POS_SPLIT) * POS_SPLIT).astype(np.float32)
    lo = (pos % POS_SPLIT).astype(np.float32)
    lane = (np.arange(2 * DIFF_HEAD_DIM, dtype=np.int32) % DIFF_HEAD_DIM)[None, :]
    qf = sum(np.where(lane % LOG2E_PARTS == i, np.float32(part), np.float32(0)) for i, part in enumerate(parts))
    qf = np.where(lane < 2 * LOG2E_PARTS, qf, np.float32(0)).astype(BF16)
    tabs = []
    for h in range(DIFF_HEADS):
        slope = np.float32(_alibi_slope(SWA_Q_HEADS + h))
        tabs.append(np.where(lane < LOG2E_PARTS, slope * hi[:, None], np.float32(0))
                    + np.where(np.logical_and(lane >= LOG2E_PARTS, lane < 2 * LOG2E_PARTS),
                               slope * lo[:, None], np.float32(0)))
    return np.stack(tabs).astype(BF16), qf


def _diff_kernel(q_ref, k_ref, kf_ref, qf_ref, vt_ref, zt_ref, lam_ref, gain_ref, o_ref,
                 acc_ref, s_ref, qp_ref, m_ref, msub_ref, alpha_ref, *, lam_init):
    qi = pl.program_id(1)
    dh = DIFF_HEAD_DIM
    hw = 2 * dh
    lane_q = lax.broadcasted_iota(jnp.int32, (ATT_TQ, hw), 1)
    q_feat = jnp.broadcast_to(qf_ref[...], (ATT_TQ, hw))
    units = [(b, h) for b in range(q_ref.shape[0]) for h in range(DIFF_HEADS)]
    for u, (b, h) in enumerate(units):
        q = q_ref[b, :, h * hw:(h + 1) * hw]
        qp_ref[2 * u] = jnp.where(lane_q < dh, q, q_feat)
        qp_ref[2 * u + 1] = jnp.where(lane_q >= dh, q, q_feat)
    lane_k = lax.broadcasted_iota(jnp.int32, (ATT_TK, hw), 1)
    ones_row = jnp.where(lax.broadcasted_iota(jnp.int32, (ACC_ROWS - hw, ATT_TK), 0) == 0,
                         1.0, 0.0).astype(BF16)
    krow = lax.broadcasted_iota(jnp.int32, (ATT_TK, ATT_TQ), 0)
    qcol = lax.broadcasted_iota(jnp.int32, (ATT_TK, ATT_TQ), 1)
    nt = (((1,), (1,)), ((), ()))

    acc_ref[...] = jnp.zeros_like(acc_ref)
    m_ref[...] = jnp.full(m_ref.shape, -jnp.inf, F32)

    def scores(t, buf, masked):
        k0 = pl.multiple_of(t * ATT_TK, ATT_TK)
        for u, (b, h) in enumerate(units):
            k = k_ref[b, pl.ds(k0, ATT_TK), h * hw:(h + 1) * hw]
            kf = kf_ref[h, pl.ds(k0, ATT_TK), :]
            for c, kp in ((2 * u, jnp.where(lane_k < dh, k, kf)),
                          (2 * u + 1, jnp.where(lane_k >= dh, k, kf))):
                s = lax.dot_general(kp, qp_ref[c], nt, preferred_element_type=F32)
                if masked:
                    s = jnp.where(krow <= qcol, s, -jnp.inf)
                s_ref[buf, c] = s
                m_old = m_ref[c]
                m_new = jnp.maximum(m_old, jnp.max(s, axis=0, keepdims=True))
                m_ref[c] = m_new
                msub_ref[buf, c] = m_new
                alpha_ref[buf, c] = jnp.exp2(m_old - m_new)

    def values(t, buf):
        for u, (b, h) in enumerate(units):
            vta = jnp.concatenate([vt_ref[b, t, h * hw:(h + 1) * hw, :], ones_row], axis=0)
            for c in (2 * u, 2 * u + 1):
                p = jnp.exp2(s_ref[buf, c] - msub_ref[buf, c]).astype(BF16)
                acc_ref[c] = (alpha_ref[buf, c] * acc_ref[c]
                              + jnp.dot(vta, p, preferred_element_type=F32))

    pairs = jnp.maximum(qi - 1, 0) // 2
    rest = qi - 1 - 2 * pairs
    t0 = 2 * pairs

    @pl.when(qi == 0)
    def _():
        scores(0, 0, True)
        values(0, 0)

    @pl.when(qi >= 1)
    def _():
        scores(0, 0, False)

    def two_tiles(i, carry):
        t = 2 * i
        scores(t + 1, 1, False)
        values(t, 0)
        scores(t + 2, 0, False)
        values(t + 1, 1)
        return carry

    lax.fori_loop(0, pairs, two_tiles, 0)

    @pl.when(jnp.logical_and(qi >= 1, rest == 1))
    def _():
        scores(t0 + 1, 1, False)
        values(t0, 0)
        scores(t0 + 2, 0, True)
        values(t0 + 1, 1)
        values(t0 + 2, 0)

    @pl.when(jnp.logical_and(qi >= 1, rest == 0))
    def _():
        scores(t0 + 1, 1, True)
        values(t0, 0)
        values(t0 + 1, 1)

    lp = lam_ref[...]
    lam = (jnp.exp(jnp.sum(lp[0:1] * lp[1:2], axis=-1, keepdims=True))
           - jnp.exp(jnp.sum(lp[2:3] * lp[3:4], axis=-1, keepdims=True)) + lam_init)
    gain = gain_ref[...] * (1.0 - lam_init)
    for u, (b, h) in enumerate(units):
        a1 = acc_ref[2 * u]
        a2 = acc_ref[2 * u + 1]
        o_t = (a1[:hw] * (1.0 / a1[hw:hw + 1]) - lam * (a2[:hw] * (1.0 / a2[hw:hw + 1])))
        o_t = o_t * lax.rsqrt(jnp.mean(o_t * o_t, axis=0, keepdims=True) + RMS_EPS) * gain
        z_t = zt_ref[b, h * hw:(h + 1) * hw, :].astype(F32)
        o_ref[b, h * hw:(h + 1) * hw, :] = (o_t * _silu(z_t)).astype(BF16)


def _diff_attention(qd, kd, vd_t, zd_t, lam_params, subln_gain, lam_init):
    bsz, seq, _ = qd.shape
    hw = 2 * DIFF_HEAD_DIM
    n_kv = seq // ATT_TK
    nb = DIFF_STREAMS
    chains = 2 * DIFF_HEADS * nb
    kf, qf = _diff_features(seq)
    gain_rows = jnp.broadcast_to(subln_gain.astype(F32).reshape(hw, 1), (hw, ATT_TQ))
    tile_t = pl.BlockSpec((nb, DIFF_WIDTH, ATT_TQ), lambda b, i: (b, 0, i))
    return pl.pallas_call(
        functools.partial(_diff_kernel, lam_init=lam_init),
        grid=(bsz // nb, seq // ATT_TQ),
        in_specs=[
            pl.BlockSpec((nb, ATT_TQ, DIFF_WIDTH), lambda b, i: (b, i, 0)),
            pl.BlockSpec((nb, seq, DIFF_WIDTH), lambda b, i: (b, 0, 0)),
            pl.BlockSpec((DIFF_HEADS, seq, hw), lambda b, i: (0, 0, 0)),
            pl.BlockSpec((1, hw), lambda b, i: (0, 0)),
            pl.BlockSpec((nb, n_kv, DIFF_WIDTH, ATT_TK), lambda b, i: (b, 0, 0, 0)),
            tile_t,
            pl.BlockSpec((4, DIFF_HEAD_DIM), lambda b, i: (0, 0)),
            pl.BlockSpec((hw, ATT_TQ), lambda b, i: (0, 0)),
        ],
        out_specs=tile_t,
        out_shape=jax.ShapeDtypeStruct((bsz, DIFF_WIDTH, seq), BF16),
        scratch_shapes=[pltpu.VMEM((chains, ACC_ROWS, ATT_TQ), F32),
                        pltpu.VMEM((2, chains, ATT_TK, ATT_TQ), F32),
                        pltpu.VMEM((chains, ATT_TQ, hw), BF16),
                        pltpu.VMEM((chains, 1, ATT_TQ), F32),
                        pltpu.VMEM((2, chains, 1, ATT_TQ), F32),
                        pltpu.VMEM((2, chains, 1, ATT_TQ), F32)],
        compiler_params=pltpu.CompilerParams(
            dimension_semantics=("parallel", "parallel"), vmem_limit_bytes=VMEM_LIMIT),
        name="diffattn",
    )(qd, kd, kf, qf, vd_t, zd_t, lam_params, gain_rows)


SWA_STEP_BLOCKS = 4
SWA_ACC_ROWS = SWA_HEAD_DIM + 16


def _swa_features():
    w = WINDOW
    zero = np.float32(0)
    lane = np.arange(LANES, dtype=np.int32)[None, :]
    r = np.arange(2 * w, dtype=np.float32)[:, None]
    kf = np.where(lane == 0, r, zero) + np.where(lane == 1, np.float32(1), zero)
    c = np.arange(w, dtype=np.float32)[:, None]
    qf = np.stack([np.where(lane == 0, np.float32(_alibi_slope(h)), zero)
                   + np.where(lane == 1, np.float32(-_alibi_slope(h)) * (w + c), zero)
                   for h in range(SWA_Q_HEADS)])
    return kf.astype(BF16), qf.astype(BF16)


def _swa_kernel(q_ref, kp_ref, kc_ref, vp_ref, vc_ref, z_ref, sink_ref, kf_ref, qf_ref, o_ref, s_ref):
    i = pl.program_id(1)
    w = WINDOW
    dh = SWA_HEAD_DIM
    nt = (((1,), (1,)), ((), ()))
    lane = lax.broadcasted_iota(jnp.int32, (w, 2 * dh), 1)
    keys = jnp.concatenate([kp_ref[0], kc_ref[0]], axis=0)
    v_t = jnp.concatenate([vp_ref[0], vc_ref[0]], axis=1)
    kf = kf_ref[...]
    ones_row = jnp.where(lax.broadcasted_iota(jnp.int32, (SWA_ACC_ROWS - dh, 2 * w), 0) == 0,
                         1.0, 0.0).astype(BF16)
    r = lax.broadcasted_iota(jnp.int32, (2 * w, w), 0)
    c = lax.broadcasted_iota(jnp.int32, (2 * w, w), 1)
    band = jnp.logical_and(r > c, r <= c + w)
    first = jnp.logical_and(band, jnp.logical_or(r >= w, i > 0))

    for n in range(SWA_STEP_BLOCKS):
        q = q_ref[0, n * w:(n + 1) * w, :]
        qa, qb = q[:, :2 * dh], q[:, 2 * dh:]
        zero = jnp.zeros_like(qa)
        heads = (jnp.where(lane < dh, qa, zero), jnp.where(lane < dh, qb, zero),
                 jnp.where(lane >= dh, qa, zero), jnp.where(lane >= dh, qb, zero))
        qpp = jnp.concatenate([jnp.concatenate([heads[h], qf_ref[h]], axis=1)
                               for h in range(SWA_Q_HEADS)], axis=0)
        kpp = jnp.concatenate([keys[n * w:(n + 2) * w], kf], axis=1)
        s_ref[n] = lax.dot_general(kpp, qpp, nt, preferred_element_type=F32)

    for n in range(SWA_STEP_BLOCKS):
        valid = first if n == 0 else band
        outs = []
        for kh in range(SWA_KV_HEADS):
            vta = jnp.concatenate([v_t[kh * dh:(kh + 1) * dh, n * w:(n + 2) * w], ones_row], axis=0)
            es, ms = [], []
            for g in range(SWA_GROUP):
                h = SWA_GROUP * kh + g
                s = jnp.where(valid, s_ref[n, :, h * w:(h + 1) * w], -jnp.inf)
                m = jnp.maximum(jnp.max(s, axis=0, keepdims=True), sink_ref[h][:, :1])
                es.append(jnp.exp(s - m).astype(BF16))
                ms.append(m)
            acc = jnp.dot(vta, jnp.concatenate(es, axis=1), preferred_element_type=F32)
            for g in range(SWA_GROUP):
                h = SWA_GROUP * kh + g
                denom = acc[dh:dh + 1, g * w:(g + 1) * w] + jnp.exp(sink_ref[h][:, :1] - ms[g])
                outs.append(acc[:dh, g * w:(g + 1) * w] * (1.0 / denom))
        o = jnp.concatenate(outs, axis=0).T
        z = z_ref[0, n * w:(n + 1) * w, :].astype(F32)
        o_ref[0, n * w:(n + 1) * w, :] = (o * _silu(z)).astype(BF16)


def _swa(qs, ks, vs_t, zs, sinks):
    bsz, seq, _ = qs.shape
    w = WINDOW
    n = SWA_STEP_BLOCKS
    sink_rows = jnp.broadcast_to(sinks.astype(F32).reshape(SWA_Q_HEADS, 1, 1), (SWA_Q_HEADS, 1, LANES))
    kf, qf = _swa_features()
    qspec = pl.BlockSpec((1, n * w, SWA_WIDTH), lambda b, i: (b, i, 0))
    prev_block = lambda i: jnp.maximum(i * n - 1, 0)
    return pl.pallas_call(
        _swa_kernel,
        grid=(bsz, seq // (n * w)),
        in_specs=[
            qspec,
            pl.BlockSpec((1, w, SWA_KV_WIDTH), lambda b, i: (b, prev_block(i), 0)),
            pl.BlockSpec((1, n * w, SWA_KV_WIDTH), lambda b, i: (b, i, 0)),
            pl.BlockSpec((1, SWA_KV_WIDTH, w), lambda b, i: (b, 0, prev_block(i))),
            pl.BlockSpec((1, SWA_KV_WIDTH, n * w), lambda b, i: (b, 0, i)),
            qspec,
            pl.BlockSpec((SWA_Q_HEADS, 1, LANES), lambda b, i: (0, 0, 0)),
            pl.BlockSpec((2 * w, LANES), lambda b, i: (0, 0)),
            pl.BlockSpec((SWA_Q_HEADS, w, LANES), lambda b, i: (0, 0, 0)),
        ],
        out_specs=qspec,
        out_shape=jax.ShapeDtypeStruct((bsz, seq, SWA_WIDTH), BF16),
        scratch_shapes=[pltpu.VMEM((n, 2 * w, SWA_Q_HEADS * w), F32)],
        compiler_params=pltpu.CompilerParams(
            dimension_semantics=("parallel", "parallel"), vmem_limit_bytes=VMEM_LIMIT),
        name="swa",
    )(qs, ks, ks, vs_t, vs_t, zs, sink_rows, kf, qf)


def _outproj_kernel(x_ref, ys_ref, yd_ref, yw_ref, mod_ref, w_ref, fg_ref, o_ref, *, final_norm):
    a = SSM_WIDTH
    b = SSM_WIDTH + DIFF_WIDTH
    y = (jnp.dot(ys_ref[0], w_ref[:a, :], preferred_element_type=F32)
         + lax.dot_general(yd_ref[0], w_ref[a:b, :], (((0,), (0,)), ((), ())),
                           preferred_element_type=F32)
         + jnp.dot(yw_ref[0], w_ref[b:, :], preferred_element_type=F32))
    out = x_ref[0] + mod_ref[0][2:3, :] * y
    if final_norm:
        out = out * lax.rsqrt(jnp.mean(out * out, axis=-1, keepdims=True) + RMS_EPS) * fg_ref[...]
    o_ref[0] = out


def _outproj(x, y_ssm, y_diff, y_swa, mod_l, w_out_bf16, final_gain, final_norm):
    bsz, seq, _ = x.shape
    tok = lambda n: pl.BlockSpec((1, OUT_TILE, n), lambda b, i: (b, i, 0))
    return pl.pallas_call(
        functools.partial(_outproj_kernel, final_norm=final_norm),
        grid=(bsz, seq // OUT_TILE),
        in_specs=[
            tok(D_MODEL), tok(SSM_WIDTH),
            pl.BlockSpec((1, DIFF_WIDTH, OUT_TILE), lambda b, i: (b, 0, i)),
            tok(SWA_WIDTH),
            pl.BlockSpec((1, 3, D_MODEL), lambda b, i: (b, 0, 0)),
            pl.BlockSpec((D_MODEL, D_MODEL), lambda b, i: (0, 0)),
            pl.BlockSpec((1, D_MODEL), lambda b, i: (0, 0)),
        ],
        out_specs=tok(D_MODEL),
        out_shape=jax.ShapeDtypeStruct((bsz, seq, D_MODEL), F32),
        compiler_params=pltpu.CompilerParams(
            dimension_semantics=("parallel", "parallel"), vmem_limit_bytes=VMEM_LIMIT),
        name="outproj",
    )(x, y_ssm, y_diff, y_swa, mod_l, w_out_bf16, final_gain.reshape(1, D_MODEL).astype(F32))


def kernel(x, c, norm_gain, ada_w, ada_b, w_in, w_out, ssm_lam_re, ssm_lam_im, ssm_log_step,
           ssm_b_re, ssm_b_im, ssm_c_re, ssm_c_im, ssm_d, glu_w, glu_b,
           diff_lq1, diff_lk1, diff_lq2, diff_lk2, diff_subln, swa_sinks, final_gain):
    bsz = x.shape[0]
    mod = _ada(c, ada_w, ada_b).reshape(DEPTH, bsz, 3, D_MODEL)
    s5_tables = jax.vmap(_s5_tables)(ssm_lam_re, ssm_lam_im, ssm_log_step,
                                     ssm_b_re, ssm_b_im, ssm_c_re, ssm_c_im)
    for l in range(DEPTH):
        proj = _inproj(x, mod[l], norm_gain[l], w_in[l], tuple(t[l] for t in s5_tables),
                       ssm_d[l], glu_w[l], glu_b[l])
        y_ssm = proj["y_ssm"]
        lam_init = 0.8 - 0.6 * math.exp(-0.3 * l)
        lam_params = jnp.stack([diff_lq1[l], diff_lk1[l], diff_lq2[l], diff_lk2[l]]).astype(F32)
        y_diff = _diff_attention(proj["qd"], proj["kd"], proj["vd"], proj["z_diff"],
                                 lam_params, diff_subln[l], lam_init)
        y_swa = _swa(proj["qs"], proj["ks"], proj["vs"], proj["z_swa"], swa_sinks[l])
        x = _outproj(x, y_ssm, y_diff, y_swa, mod[l], w_out[l].astype(BF16), final_gain,
                     final_norm=(l == DEPTH - 1))
    return x
```

```python
import functools
import math

import jax
import jax.numpy as jnp
import numpy as np
from jax import lax
from jax.experimental import pallas as pl
from jax.experimental.pallas import tpu as pltpu

F32 = jnp.float32
BF16 = jnp.bfloat16

D_MODEL = 1024
DEPTH = 2
SSM_GROUP = 16
SSM_GROUPS = 16
SSM_WIDTH = 256
SSM_STATE = 64
SSM_LANES = SSM_GROUPS * SSM_STATE
DIFF_HEADS = 4
DIFF_HEAD_DIM = 64
DIFF_WIDTH = 512
SWA_Q_HEADS = 4
SWA_KV_HEADS = 2
SWA_GROUP = 2
SWA_HEAD_DIM = 64
SWA_WIDTH = 256
SWA_KV_WIDTH = 128
WINDOW = 128
N_ATTN_HEADS = 8
RMS_EPS = 1e-6

PROJ_NAMES = ("u", "z_ssm", "qd", "kd", "vd", "z_diff", "qs", "ks", "vs", "z_swa")
PROJ_SIZES = (SSM_WIDTH, SSM_WIDTH, DIFF_WIDTH, DIFF_WIDTH, DIFF_WIDTH, DIFF_WIDTH,
              SWA_WIDTH, SWA_KV_WIDTH, SWA_KV_WIDTH, SWA_WIDTH)
IN_COLS = sum(PROJ_SIZES)
PROJ_TRANSPOSED = ("vd", "z_diff", "vs")
INPROJ_OUTPUTS = ("y_ssm", "y_swa", "qd", "kd", "vd", "z_diff")
LOG2E = math.log2(math.e)

SUBLANES = 8
LANES = 128
VMEM_LIMIT = 48 * 1024 * 1024

TOK_TILE = 512
OUT_TILE = 1024
S5_CHUNK = 256
S5_BLOCK = 32
ATT_TQ = 256
ATT_TK = 256
DIFF_STREAMS = 2


def _silu(x):
    return x * jax.nn.sigmoid(x)


def _alibi_slope(head_index):
    return 2.0 ** (-(head_index + 1) * (8.0 / N_ATTN_HEADS))


def _ada_kernel(c_ref, w_ref, b_ref, o_ref):
    cond = _silu(c_ref[...])
    o_ref[0] = jnp.dot(cond, w_ref[0], preferred_element_type=F32) + b_ref[0]


def _ada(c, ada_w, ada_b):
    bsz = c.shape[0]
    col = D_MODEL
    return pl.pallas_call(
        _ada_kernel,
        grid=(DEPTH, 3),
        in_specs=[
            pl.BlockSpec((bsz, D_MODEL), lambda l, j: (0, 0)),
            pl.BlockSpec((1, D_MODEL, col), lambda l, j: (l, 0, j)),
            pl.BlockSpec((1, 1, col), lambda l, j: (l, 0, j)),
        ],
        out_specs=pl.BlockSpec((1, bsz, col), lambda l, j: (l, 0, j)),
        out_shape=jax.ShapeDtypeStruct((DEPTH, bsz, 3 * D_MODEL), F32),
        compiler_params=pltpu.CompilerParams(
            dimension_semantics=("parallel", "parallel"), vmem_limit_bytes=VMEM_LIMIT),
        name="ada",
    )(c, ada_w, ada_b.reshape(DEPTH, 1, 3 * D_MODEL))


def _cmul(a_re, a_im, b_re, b_im):
    return a_re * b_re - a_im * b_im, a_re * b_im + a_im * b_re


def _dot_row_halves(a, b):
    half = a.shape[0] // 2
    return jnp.concatenate([jnp.dot(a[:half], b, preferred_element_type=F32),
                            jnp.dot(a[half:], b, preferred_element_type=F32)], axis=0)


def _proj_start(name):
    return sum(PROJ_SIZES[:PROJ_NAMES.index(name)])


def _proj_size(name):
    return PROJ_SIZES[PROJ_NAMES.index(name)]


def _inproj_kernel(x_ref, mod_ref, g_ref, w_ref, wq_ref,
                   bbd_ref, cbd_ref, tri_ref, row_ref, vec_ref, d_ref, wg_ref, bg_ref,
                   sink_ref, skf_ref, sqf_ref, *refs):
    out_refs = dict(zip(INPROJ_OUTPUTS, refs[:len(INPROJ_OUTPUTS)]))
    (uz_ref, xs_ref, carry_scr,
     qs_scr, ks_scr, vst_scr, zsw_scr, ss_ref) = refs[len(INPROJ_OUTPUTS):]
    step = pl.program_id(1)
    n = SSM_LANES
    nb = S5_CHUNK // S5_BLOCK
    nt = (((1,), (1,)), ((), ()))
    w = WINDOW

    @pl.when(step == 0)
    def _():
        carry_scr[...] = jnp.zeros_like(carry_scr)
        ks_scr[:w, :] = jnp.zeros((w, SWA_KV_WIDTH), BF16)
        vst_scr[:, :w] = jnp.zeros((SWA_KV_WIDTH, w), BF16)

    xf = x_ref[0]
    y = xf * lax.rsqrt(jnp.mean(xf * xf, axis=-1, keepdims=True) + RMS_EPS) * g_ref[...]
    mod = mod_ref[0]
    h = (y * (1.0 + mod[1:2, :]) + mod[0:1, :]).astype(BF16)

    def project(name):
        size = _proj_size(name)
        if name in PROJ_TRANSPOSED:
            c0 = _proj_start(name)
            rows = lax.dot_general(w_ref[:, c0:c0 + size], h, (((0,), (1,)), ((), ())),
                                   preferred_element_type=F32).astype(BF16)
            if name == "vd":
                for t in range(TOK_TILE // ATT_TK):
                    out_refs[name][0, t] = rows[:, t * ATT_TK:(t + 1) * ATT_TK]
            elif name == "vs":
                vst_scr[:, w:] = rows
            else:
                out_refs[name][0] = rows
        elif name == "qd":
            out_refs[name][0] = jnp.dot(h, wq_ref[:, :size], preferred_element_type=F32).astype(BF16)
        elif name == "qs":
            qs_scr[...] = _dot_row_halves(h, wq_ref[:, DIFF_WIDTH:]).astype(BF16)
        else:
            c0 = _proj_start(name)
            res = _dot_row_halves(h, w_ref[:, c0:c0 + size])
            if name == "ks":
                ks_scr[w:, :] = res.astype(BF16)
            elif name == "z_swa":
                zsw_scr[...] = res
            else:
                out_refs[name][0] = res.astype(BF16)

    dh = SWA_HEAD_DIM
    lane = lax.broadcasted_iota(jnp.int32, (w, 2 * dh), 1)
    ones_row = jnp.where(lax.broadcasted_iota(jnp.int32, (SWA_ACC_ROWS - dh, 2 * w), 0) == 0,
                         1.0, 0.0).astype(BF16)
    r_io = lax.broadcasted_iota(jnp.int32, (2 * w, w), 0)
    c_io = lax.broadcasted_iota(jnp.int32, (2 * w, w), 1)
    band = jnp.logical_and(r_io > c_io, r_io <= c_io + w)
    first = jnp.logical_and(band, jnp.logical_or(r_io >= w, step > 0))

    def swa_scores(blk):
        q = qs_scr[blk * w:(blk + 1) * w, :]
        qa, qb = q[:, :2 * dh], q[:, 2 * dh:]
        zero = jnp.zeros_like(qa)
        heads = (jnp.where(lane < dh, qa, zero), jnp.where(lane < dh, qb, zero),
                 jnp.where(lane >= dh, qa, zero), jnp.where(lane >= dh, qb, zero))
        qpp = jnp.concatenate([jnp.concatenate([heads[hd], sqf_ref[hd]], axis=1)
                               for hd in range(SWA_Q_HEADS)], axis=0)
        kpp = jnp.concatenate([ks_scr[blk * w:(blk + 2) * w, :], skf_ref[...]], axis=1)
        ss_ref[blk] = lax.dot_general(kpp, qpp, nt, preferred_element_type=F32)

    swa_exp = {}

    def swa_softmax(blk):
        valid = first if blk == 0 else band
        es, ms = [], []
        for hd in range(SWA_Q_HEADS):
            s = jnp.where(valid, ss_ref[blk, :, hd * w:(hd + 1) * w], -jnp.inf)
            m = jnp.maximum(jnp.max(s, axis=0, keepdims=True), sink_ref[hd][:, :1])
            es.append(jnp.exp(s - m).astype(BF16))
            ms.append(m)
        swa_exp[blk] = (es, ms)

    def swa_values(blk):
        es, ms = swa_exp.pop(blk)
        outs = []
        for kh in range(SWA_KV_HEADS):
            vta = jnp.concatenate([vst_scr[kh * dh:(kh + 1) * dh, blk * w:(blk + 2) * w], ones_row],
                                  axis=0)
            pair = es[SWA_GROUP * kh:SWA_GROUP * (kh + 1)]
            acc = jnp.dot(vta, jnp.concatenate(pair, axis=1), preferred_element_type=F32)
            for g in range(SWA_GROUP):
                hd = SWA_GROUP * kh + g
                denom = acc[dh:dh + 1, g * w:(g + 1) * w] + jnp.exp(sink_ref[hd][:, :1] - ms[hd])
                outs.append(acc[:dh, g * w:(g + 1) * w] * (1.0 / denom))
        o = jnp.concatenate(outs, axis=0).T
        z = zsw_scr[blk * w:(blk + 1) * w, :]
        out_refs["y_swa"][0, blk * w:(blk + 1) * w, :] = (o * _silu(z)).astype(BF16)

    uz_ref[...] = jnp.dot(h, w_ref[:, :2 * SSM_WIDTH], preferred_element_type=F32)

    n_blk = TOK_TILE // w
    gaps = iter((
        [lambda: project("qs")],
        [lambda: project("ks"), lambda: project("vs")],
        [lambda: project("z_swa")],
        [lambda b=b: swa_scores(b) for b in range(n_blk)],
        ([lambda b=b: swa_softmax(b) for b in range(n_blk // 2)] + [lambda: project("vd")]
         + [lambda b=b: swa_values(b) for b in range(n_blk // 2)]),
        ([lambda b=b: swa_softmax(b) for b in range(n_blk // 2, n_blk)] + [lambda: project("z_diff")]
         + [lambda b=b: swa_values(b) for b in range(n_blk // 2, n_blk)]),
        [lambda: project("qd")],
        [lambda: project("kd")],
    ))

    def fill_gap():
        for job in next(gaps):
            job()

    for ci in range(TOK_TILE // S5_CHUNK):
        rows = slice(ci * S5_CHUNK, (ci + 1) * S5_CHUNK)
        u = uz_ref[rows, :SSM_WIDTH].astype(BF16)
        bu = jnp.dot(u, bbd_ref[...], preferred_element_type=F32)
        bu = bu.astype(BF16).reshape(nb, S5_BLOCK, 2 * n)
        z_re, z_im = _cmul(row_ref[0], row_ref[1], bu[:, :, :n], bu[:, :, n:])
        zs = jnp.concatenate([z_re, z_im], axis=-1).reshape(S5_CHUNK, 2 * n)
        fill_gap()

        xs_ref[...] = jnp.dot(tri_ref[...], zs, preferred_element_type=F32)
        ends = jnp.concatenate([xs_ref[(k + 1) * S5_BLOCK - 1:(k + 1) * S5_BLOCK, :]
                                for k in range(nb)], axis=0)
        e_re, e_im = _cmul(vec_ref[0:1], vec_ref[1:2], ends[:, :n], ends[:, n:])
        s_re = carry_scr[0:1, :]
        s_im = carry_scr[1:2, :]
        inj_re, inj_im = [], []
        for k in range(nb):
            g_re, g_im = _cmul(vec_ref[4:5], vec_ref[5:6], s_re, s_im)
            inj_re.append(g_re)
            inj_im.append(g_im)
            d_re, d_im = _cmul(vec_ref[2:3], vec_ref[3:4], s_re, s_im)
            s_re = e_re[k:k + 1] + d_re
            s_im = e_im[k:k + 1] + d_im
        carry_scr[0:1, :] = s_re
        carry_scr[1:2, :] = s_im
        inj_re = jnp.concatenate(inj_re, axis=0)[:, None, :]
        inj_im = jnp.concatenate(inj_im, axis=0)[:, None, :]
        xs = xs_ref[...].reshape(nb, S5_BLOCK, 2 * n)
        x_re, x_im = _cmul(row_ref[2], row_ref[3], (xs[:, :, :n] + inj_re).astype(BF16),
                           (xs[:, :, n:] + inj_im).astype(BF16))
        st = jnp.concatenate([x_re, x_im], axis=-1).reshape(S5_CHUNK, 2 * n)
        fill_gap()

        ys = _dot_row_halves(st, cbd_ref[...])
        ys = jax.nn.gelu(ys + d_ref[...] * u.astype(F32))
        fill_gap()

        gate = jnp.dot(ys.astype(BF16), wg_ref[...], preferred_element_type=F32) + bg_ref[...]
        ys = ys * jax.nn.sigmoid(gate)
        out_refs["y_ssm"][0, rows, :] = (ys * _silu(uz_ref[rows, SSM_WIDTH:])).astype(BF16)
        fill_gap()

    ks_scr[:w, :] = ks_scr[TOK_TILE:, :]
    vst_scr[:, :w] = vst_scr[:, TOK_TILE:]


def _inproj_params(norm_gain, w_in, s5_tables, ssm_d, glu_w, glu_b, swa_sinks):
    w_main = w_in.astype(BF16)
    qd0, qs0 = _proj_start("qd"), _proj_start("qs")
    dh = SWA_HEAD_DIM
    w_q = jnp.concatenate(
        [w_in[:, :, qd0:qd0 + DIFF_WIDTH].astype(F32) * (DIFF_HEAD_DIM ** -0.5 * LOG2E)]
        + [w_in[:, :, qs0 + h * dh:qs0 + (h + 1) * dh].astype(F32) * dh ** -0.5 for h in (0, 2, 1, 3)],
        axis=2).astype(BF16)
    bbd, cbd, row_tabs, vec_tabs = s5_tables
    depth = w_in.shape[0]
    sink_rows = jnp.broadcast_to(swa_sinks.astype(F32).reshape(depth, SWA_Q_HEADS, 1, 1),
                                 (depth, SWA_Q_HEADS, 1, LANES))
    return (norm_gain.astype(F32).reshape(depth, 1, D_MODEL), w_main, w_q,
            bbd, cbd, row_tabs, vec_tabs,
            ssm_d.astype(F32).reshape(depth, 1, SSM_WIDTH), glu_w.astype(BF16),
            glu_b.astype(F32).reshape(depth, 1, SSM_WIDTH), sink_rows)


def _inproj(layer, x, mod, params):
    bsz, seq, _ = x.shape
    grid = (bsz, seq // TOK_TILE)
    t = np.arange(S5_CHUNK, dtype=np.int32)
    tri = np.logical_and(t[:, None] // S5_BLOCK == t[None, :] // S5_BLOCK,
                         t[:, None] >= t[None, :]).astype(BF16)
    swa_kf, swa_qf = _swa_features()
    tiles_per_step = TOK_TILE // ATT_TK
    out_specs, out_shape = [], []
    for name in INPROJ_OUTPUTS:
        n = {"y_ssm": SSM_WIDTH, "y_swa": SWA_WIDTH}.get(name) or _proj_size(name)
        if name == "vd":
            out_specs.append(pl.BlockSpec((1, tiles_per_step, n, ATT_TK), lambda b, i: (b, i, 0, 0)))
            out_shape.append(jax.ShapeDtypeStruct((bsz, seq // ATT_TK, n, ATT_TK), BF16))
        elif name in PROJ_TRANSPOSED:
            out_specs.append(pl.BlockSpec((1, n, TOK_TILE), lambda b, i: (b, 0, i)))
            out_shape.append(jax.ShapeDtypeStruct((bsz, n, seq), BF16))
        else:
            out_specs.append(pl.BlockSpec((1, TOK_TILE, n), lambda b, i: (b, i, 0)))
            out_shape.append(jax.ShapeDtypeStruct((bsz, seq, n), BF16))
    full = lambda a: pl.BlockSpec(a.shape, lambda b, i: (0,) * a.ndim)
    of_layer = lambda a: pl.BlockSpec((None,) + a.shape[1:],
                                      lambda b, i: (layer,) + (0,) * (a.ndim - 1))
    (gain, w_main, w_q, bbd, cbd, row_tabs, vec_tabs, d_skip, w_glu, b_glu, sink_rows) = params
    operands = [(gain, of_layer), (w_main, of_layer), (w_q, of_layer),
                (bbd, of_layer), (cbd, of_layer), (tri, full), (row_tabs, of_layer),
                (vec_tabs, of_layer), (d_skip, of_layer), (w_glu, of_layer), (b_glu, of_layer),
                (sink_rows, of_layer), (swa_kf, full), (swa_qf, full)]
    outs = pl.pallas_call(
        _inproj_kernel,
        grid=grid,
        in_specs=[
            pl.BlockSpec((1, TOK_TILE, D_MODEL), lambda b, i: (b, i, 0)),
            pl.BlockSpec((None, 1, 3, D_MODEL), lambda b, i: (layer, b, 0, 0)),
        ] + [spec(a) for a, spec in operands],
        out_specs=out_specs,
        out_shape=out_shape,
        scratch_shapes=[
            pltpu.VMEM((TOK_TILE, 2 * SSM_WIDTH), F32),
            pltpu.VMEM((S5_CHUNK, 2 * SSM_LANES), F32),
            pltpu.VMEM((SUBLANES, SSM_LANES), F32),
            pltpu.VMEM((TOK_TILE, SWA_WIDTH), BF16),
            pltpu.VMEM((WINDOW + TOK_TILE, SWA_KV_WIDTH), BF16),
            pltpu.VMEM((SWA_KV_WIDTH, WINDOW + TOK_TILE), BF16),
            pltpu.VMEM((TOK_TILE, SWA_WIDTH), F32),
            pltpu.VMEM((TOK_TILE // WINDOW, 2 * WINDOW, SWA_Q_HEADS * WINDOW), F32),
        ],
        compiler_params=pltpu.CompilerParams(
            dimension_semantics=("parallel", "arbitrary"), vmem_limit_bytes=VMEM_LIMIT),
        name="inproj",
    )(x, mod, *[a for a, _ in operands])
    return dict(zip(INPROJ_OUTPUTS, outs))


def _s5_tables(lam_re, lam_im, log_step, b_re, b_im, c_re, c_im):
    g, p, h = SSM_GROUPS, SSM_STATE, SSM_GROUP
    step = jnp.exp(log_step.astype(F32))[:, None]
    lr = lam_re.astype(F32)
    li = lam_im.astype(F32)
    mag = jnp.exp(lr * step)
    ang = li * step
    ab_re = mag * jnp.cos(ang)
    ab_im = mag * jnp.sin(ang)
    den = lr * lr + li * li
    f_re = ((ab_re - 1.0) * lr + ab_im * li) / den
    f_im = (ab_im * lr - (ab_re - 1.0) * li) / den
    br = b_re.astype(F32)
    bi = b_im.astype(F32)
    bb_re = f_re[..., None] * br - f_im[..., None] * bi
    bb_im = f_re[..., None] * bi + f_im[..., None] * br
    eye = jnp.eye(g, dtype=F32)
    bbd_re = jnp.einsum("gph,gk->ghkp", bb_re, eye).reshape(g * h, g * p)
    bbd_im = jnp.einsum("gph,gk->ghkp", bb_im, eye).reshape(g * h, g * p)
    bbd = jnp.concatenate([bbd_re, bbd_im], axis=1).astype(BF16)
    cbd_re = jnp.einsum("ghp,gk->gpkh", c_re.astype(F32), eye).reshape(g * p, g * h)
    cbd_im = jnp.einsum("ghp,gk->gpkh", c_im.astype(F32), eye).reshape(g * p, g * h)
    cbd = jnp.concatenate([cbd_re, -cbd_im], axis=0).astype(BF16)

    def power(n):
        n = jnp.asarray(n, F32).reshape(-1, 1, 1)
        m = jnp.exp(lr * step * n)
        return ((m * jnp.cos(ang * n)).reshape(-1, g * p), (m * jnp.sin(ang * n)).reshape(-1, g * p))

    c = S5_BLOCK // 2
    rows = jnp.arange(S5_BLOCK, dtype=F32)
    pre = power(c - rows)
    post = power(rows - c)
    row_tabs = jnp.stack([pre[0], pre[1], post[0], post[1]]).astype(BF16)
    vec = [power(float(S5_BLOCK - 1 - c)),
           power(float(S5_BLOCK)),
           power(float(c + 1))]
    vec_tabs = jnp.concatenate([t for pair in vec for t in pair], axis=0)
    return bbd, cbd, row_tabs, vec_tabs


ACC_ROWS = 2 * DIFF_HEAD_DIM + 16
POS_SPLIT = 64


LOG2E_PARTS = 3


def _diff_features(seq):
    parts, rest = [], LOG2E
    for _ in range(LOG2E_PARTS):
        part = float(np.float32(rest).astype(BF16))
        parts.append(part)
        rest -= part
    pos = np.arange(seq, dtype=np.int32)
    hi = ((pos // POS_SPLIT) * POS_SPLIT).astype(np.float32)
    lo = (pos % POS_SPLIT).astype(np.float32)
    lane = (np.arange(2 * DIFF_HEAD_DIM, dtype=np.int32) % DIFF_HEAD_DIM)[None, :]
    qf = sum(np.where(lane % LOG2E_PARTS == i, np.float32(part), np.float32(0)) for i, part in enumerate(parts))
    qf = np.where(lane < 2 * LOG2E_PARTS, qf, np.float32(0)).astype(BF16)
    tabs = []
    for h in range(DIFF_HEADS):
        slope = np.float32(_alibi_slope(SWA_Q_HEADS + h))
        tabs.append(np.where(lane < LOG2E_PARTS, slope * hi[:, None], np.float32(0))
                    + np.where(np.logical_and(lane >= LOG2E_PARTS, lane < 2 * LOG2E_PARTS),
                               slope * lo[:, None], np.float32(0)))
    return np.stack(tabs).astype(BF16), qf


def _diff_kernel(q_ref, k_ref, kf_ref, qf_ref, vt_ref, zt_ref, lam_ref, gain_ref, o_ref,
                 acc_ref, s_ref, qp_ref, m_ref, msub_ref, alpha_ref, *, lam_init):
    qi = pl.program_id(1)
    dh = DIFF_HEAD_DIM
    hw = 2 * dh
    lane_q = lax.broadcasted_iota(jnp.int32, (ATT_TQ, hw), 1)
    q_feat = jnp.broadcast_to(qf_ref[...], (ATT_TQ, hw))
    units = [(b, h) for b in range(q_ref.shape[0]) for h in range(DIFF_HEADS)]
    for u, (b, h) in enumerate(units):
        q = q_ref[b, :, h * hw:(h + 1) * hw]
        qp_ref[2 * u] = jnp.where(lane_q < dh, q, q_feat)
        qp_ref[2 * u + 1] = jnp.where(lane_q >= dh, q, q_feat)
    lane_k = lax.broadcasted_iota(jnp.int32, (ATT_TK, hw), 1)
    ones_row = jnp.where(lax.broadcasted_iota(jnp.int32, (ACC_ROWS - hw, ATT_TK), 0) == 0,
                         1.0, 0.0).astype(BF16)
    krow = lax.broadcasted_iota(jnp.int32, (ATT_TK, ATT_TQ), 0)
    qcol = lax.broadcasted_iota(jnp.int32, (ATT_TK, ATT_TQ), 1)
    nt = (((1,), (1,)), ((), ()))

    acc_ref[...] = jnp.zeros_like(acc_ref)
    m_ref[...] = jnp.full(m_ref.shape, -jnp.inf, F32)

    def scores(t, buf, masked):
        k0 = pl.multiple_of(t * ATT_TK, ATT_TK)
        for u, (b, h) in enumerate(units):
            k = k_ref[b, pl.ds(k0, ATT_TK), h * hw:(h + 1) * hw]
            kf = kf_ref[h, pl.ds(k0, ATT_TK), :]
            for c, kp in ((2 * u, jnp.where(lane_k < dh, k, kf)),
                          (2 * u + 1, jnp.where(lane_k >= dh, k, kf))):
                s = lax.dot_general(kp, qp_ref[c], nt, preferred_element_type=F32)
                if masked:
                    s = jnp.where(krow <= qcol, s, -jnp.inf)
                s_ref[buf, c] = s
                m_old = m_ref[c]
                m_new = jnp.maximum(m_old, jnp.max(s, axis=0, keepdims=True))
                m_ref[c] = m_new
                msub_ref[buf, c] = m_new
                alpha_ref[buf, c] = jnp.exp2(m_old - m_new)

    def values(t, buf):
        for u, (b, h) in enumerate(units):
            vta = jnp.concatenate([vt_ref[b, t, h * hw:(h + 1) * hw, :], ones_row], axis=0)
            for c in (2 * u, 2 * u + 1):
                p = jnp.exp2(s_ref[buf, c] - msub_ref[buf, c]).astype(BF16)
                acc_ref[c] = (alpha_ref[buf, c] * acc_ref[c]
                              + jnp.dot(vta, p, preferred_element_type=F32))

    pairs = jnp.maximum(qi - 1, 0) // 2
    rest = qi - 1 - 2 * pairs
    t0 = 2 * pairs

    @pl.when(qi == 0)
    def _():
        scores(0, 0, True)
        values(0, 0)

    @pl.when(qi >= 1)
    def _():
        scores(0, 0, False)

    def two_tiles(i, carry):
        t = 2 * i
        scores(t + 1, 1, False)
        values(t, 0)
        scores(t + 2, 0, False)
        values(t + 1, 1)
        return carry

    lax.fori_loop(0, pairs, two_tiles, 0)

    @pl.when(jnp.logical_and(qi >= 1, rest == 1))
    def _():
        scores(t0 + 1, 1, False)
        values(t0, 0)
        scores(t0 + 2, 0, True)
        values(t0 + 1, 1)
        values(t0 + 2, 0)

    @pl.when(jnp.logical_and(qi >= 1, rest == 0))
    def _():
        scores(t0 + 1, 1, True)
        values(t0, 0)
        values(t0 + 1, 1)

    lp = lam_ref[...]
    lam = (jnp.exp(jnp.sum(lp[0:1] * lp[1:2], axis=-1, keepdims=True))
           - jnp.exp(jnp.sum(lp[2:3] * lp[3:4], axis=-1, keepdims=True)) + lam_init)
    gain = gain_ref[...] * (1.0 - lam_init)
    for u, (b, h) in enumerate(units):
        a1 = acc_ref[2 * u]
        a2 = acc_ref[2 * u + 1]
        o_t = (a1[:hw] * (1.0 / a1[hw:hw + 1]) - lam * (a2[:hw] * (1.0 / a2[hw:hw + 1])))
        o_t = o_t * lax.rsqrt(jnp.mean(o_t * o_t, axis=0, keepdims=True) + RMS_EPS) * gain
        z_t = zt_ref[b, h * hw:(h + 1) * hw, :].astype(F32)
        o_ref[b, h * hw:(h + 1) * hw, :] = (o_t * _silu(z_t)).astype(BF16)


def _diff_attention(qd, kd, vd_t, zd_t, lam_params, subln_gain, lam_init):
    bsz, seq, _ = qd.shape
    hw = 2 * DIFF_HEAD_DIM
    n_kv = seq // ATT_TK
    nb = DIFF_STREAMS
    chains = 2 * DIFF_HEADS * nb
    kf, qf = _diff_features(seq)
    gain_rows = jnp.broadcast_to(subln_gain.astype(F32).reshape(hw, 1), (hw, ATT_TQ))
    tile_t = pl.BlockSpec((nb, DIFF_WIDTH, ATT_TQ), lambda b, i: (b, 0, i))
    return pl.pallas_call(
        functools.partial(_diff_kernel, lam_init=lam_init),
        grid=(bsz // nb, seq // ATT_TQ),
        in_specs=[
            pl.BlockSpec((nb, ATT_TQ, DIFF_WIDTH), lambda b, i: (b, i, 0)),
            pl.BlockSpec((nb, seq, DIFF_WIDTH), lambda b, i: (b, 0, 0)),
            pl.BlockSpec((DIFF_HEADS, seq, hw), lambda b, i: (0, 0, 0)),
            pl.BlockSpec((1, hw), lambda b, i: (0, 0)),
            pl.BlockSpec((nb, n_kv, DIFF_WIDTH, ATT_TK), lambda b, i: (b, 0, 0, 0)),
            tile_t,
            pl.BlockSpec((4, DIFF_HEAD_DIM), lambda b, i: (0, 0)),
            pl.BlockSpec((hw, ATT_TQ), lambda b, i: (0, 0)),
        ],
        out_specs=tile_t,
        out_shape=jax.ShapeDtypeStruct((bsz, DIFF_WIDTH, seq), BF16),
        scratch_shapes=[pltpu.VMEM((chains, ACC_ROWS, ATT_TQ), F32),
                        pltpu.VMEM((2, chains, ATT_TK, ATT_TQ), F32),
                        pltpu.VMEM((chains, ATT_TQ, hw), BF16),
                        pltpu.VMEM((chains, 1, ATT_TQ), F32),
                        pltpu.VMEM((2, chains, 1, ATT_TQ), F32),
                        pltpu.VMEM((2, chains, 1, ATT_TQ), F32)],
        compiler_params=pltpu.CompilerParams(
            dimension_semantics=("parallel", "parallel"), vmem_limit_bytes=VMEM_LIMIT),
        name="diffattn",
    )(qd, kd, kf, qf, vd_t, zd_t, lam_params, gain_rows)


SWA_ACC_ROWS = SWA_HEAD_DIM + 16


def _swa_features():
    w = WINDOW
    zero = np.float32(0)
    lane = np.arange(LANES, dtype=np.int32)[None, :]
    r = np.arange(2 * w, dtype=np.float32)[:, None]
    kf = np.where(lane == 0, r, zero) + np.where(lane == 1, np.float32(1), zero)
    c = np.arange(w, dtype=np.float32)[:, None]
    qf = np.stack([np.where(lane == 0, np.float32(_alibi_slope(h)), zero)
                   + np.where(lane == 1, np.float32(-_alibi_slope(h)) * (w + c), zero)
                   for h in range(SWA_Q_HEADS)])
    return kf.astype(BF16), qf.astype(BF16)


def _outproj_kernel(x_ref, ys_ref, yd_ref, yw_ref, mod_ref, w_ref, fg_ref, o_ref, *, final_norm):
    a = SSM_WIDTH
    b = SSM_WIDTH + DIFF_WIDTH
    y = (jnp.dot(ys_ref[0], w_ref[:a, :], preferred_element_type=F32)
         + lax.dot_general(yd_ref[0], w_ref[a:b, :], (((0,), (0,)), ((), ())),
                           preferred_element_type=F32)
         + jnp.dot(yw_ref[0], w_ref[b:, :], preferred_element_type=F32))
    out = x_ref[0] + mod_ref[0][2:3, :] * y
    if final_norm:
        out = out * lax.rsqrt(jnp.mean(out * out, axis=-1, keepdims=True) + RMS_EPS) * fg_ref[...]
    o_ref[0] = out


def _outproj(layer, x, y_ssm, y_diff, y_swa, mod, w_out_bf16, final_gain, final_norm):
    bsz, seq, _ = x.shape
    tok = lambda n: pl.BlockSpec((1, OUT_TILE, n), lambda b, i: (b, i, 0))
    return pl.pallas_call(
        functools.partial(_outproj_kernel, final_norm=final_norm),
        grid=(bsz, seq // OUT_TILE),
        in_specs=[
            tok(D_MODEL), tok(SSM_WIDTH),
            pl.BlockSpec((1, DIFF_WIDTH, OUT_TILE), lambda b, i: (b, 0, i)),
            tok(SWA_WIDTH),
            pl.BlockSpec((None, 1, 3, D_MODEL), lambda b, i: (layer, b, 0, 0)),
            pl.BlockSpec((None, D_MODEL, D_MODEL), lambda b, i: (layer, 0, 0)),
            pl.BlockSpec((1, D_MODEL), lambda b, i: (0, 0)),
        ],
        out_specs=tok(D_MODEL),
        out_shape=jax.ShapeDtypeStruct((bsz, seq, D_MODEL), F32),
        compiler_params=pltpu.CompilerParams(
            dimension_semantics=("parallel", "parallel"), vmem_limit_bytes=VMEM_LIMIT),
        name="outproj",
    )(x, y_ssm, y_diff, y_swa, mod, w_out_bf16, final_gain.reshape(1, D_MODEL).astype(F32))


def kernel(x, c, norm_gain, ada_w, ada_b, w_in, w_out, ssm_lam_re, ssm_lam_im, ssm_log_step,
           ssm_b_re, ssm_b_im, ssm_c_re, ssm_c_im, ssm_d, glu_w, glu_b,
           diff_lq1, diff_lk1, diff_lq2, diff_lk2, diff_subln, swa_sinks, final_gain):
    bsz = x.shape[0]
    mod = _ada(c, ada_w, ada_b).reshape(DEPTH, bsz, 3, D_MODEL)
    s5_tables = jax.vmap(_s5_tables)(ssm_lam_re, ssm_lam_im, ssm_log_step,
                                     ssm_b_re, ssm_b_im, ssm_c_re, ssm_c_im)
    params = _inproj_params(norm_gain, w_in, s5_tables, ssm_d, glu_w, glu_b, swa_sinks)
    w_out_bf16 = w_out.astype(BF16)
    for l in range(DEPTH):
        proj = _inproj(l, x, mod, params)
        lam_init = 0.8 - 0.6 * math.exp(-0.3 * l)
        lam_params = jnp.stack([diff_lq1[l], diff_lk1[l], diff_lq2[l], diff_lk2[l]]).astype(F32)
        y_diff = _diff_attention(proj["qd"], proj["kd"], proj["vd"], proj["z_diff"],
                                 lam_params, diff_subln[l], lam_init)
        x = _outproj(l, x, proj["y_ssm"], y_diff, proj["y_swa"], mod, w_out_bf16,
                     final_gain, final_norm=(l == DEPTH - 1))
    return x
```

```python
import functools
import math

import jax
import jax.numpy as jnp
import numpy as np
from jax import lax
from jax.experimental import pallas as pl
from jax.experimental.pallas import tpu as pltpu

F32 = jnp.float32
BF16 = jnp.bfloat16

D_MODEL = 1024
DEPTH = 2
SSM_GROUP = 16
SSM_GROUPS = 16
SSM_WIDTH = 256
SSM_STATE = 64
SSM_LANES = SSM_GROUPS * SSM_STATE
DIFF_HEADS = 4
DIFF_HEAD_DIM = 64
DIFF_WIDTH = 512
SWA_Q_HEADS = 4
SWA_KV_HEADS = 2
SWA_GROUP = 2
SWA_HEAD_DIM = 64
SWA_WIDTH = 256
SWA_KV_WIDTH = 128
WINDOW = 128
N_ATTN_HEADS = 8
RMS_EPS = 1e-6

PROJ_NAMES = ("u", "z_ssm", "qd", "kd", "vd", "z_diff", "qs", "ks", "vs", "z_swa")
PROJ_SIZES = (SSM_WIDTH, SSM_WIDTH, DIFF_WIDTH, DIFF_WIDTH, DIFF_WIDTH, DIFF_WIDTH,
              SWA_WIDTH, SWA_KV_WIDTH, SWA_KV_WIDTH, SWA_WIDTH)
IN_COLS = sum(PROJ_SIZES)
PROJ_TRANSPOSED = ("vd", "z_diff", "vs")
INPROJ_OUTPUTS = ("y_ssm", "y_swa", "qd", "kd", "vd", "z_diff")
LOG2E = math.log2(math.e)

SUBLANES = 8
LANES = 128
VMEM_LIMIT = 48 * 1024 * 1024
OUTPROJ_VMEM_LIMIT = 56 * 1024 * 1024

TOK_TILE = 512
OUT_TILE = 2048
S5_CHUNK = 256
S5_BLOCK = 32
ATT_TQ = 256
ATT_TK = 256
DIFF_STREAMS = 2


def _silu(x):
    return x * jax.nn.sigmoid(x)


def _alibi_slope(head_index):
    return 2.0 ** (-(head_index + 1) * (8.0 / N_ATTN_HEADS))


def _ada_kernel(c_ref, w_ref, b_ref, o_ref):
    cond = _silu(c_ref[...])
    o_ref[0] = jnp.dot(cond, w_ref[0], preferred_element_type=F32) + b_ref[0]


def _ada(c, ada_w, ada_b):
    bsz = c.shape[0]
    col = D_MODEL
    return pl.pallas_call(
        _ada_kernel,
        grid=(DEPTH, 3),
        in_specs=[
            pl.BlockSpec((bsz, D_MODEL), lambda l, j: (0, 0)),
            pl.BlockSpec((1, D_MODEL, col), lambda l, j: (l, 0, j)),
            pl.BlockSpec((1, 1, col), lambda l, j: (l, 0, j)),
        ],
        out_specs=pl.BlockSpec((1, bsz, col), lambda l, j: (l, 0, j)),
        out_shape=jax.ShapeDtypeStruct((DEPTH, bsz, 3 * D_MODEL), F32),
        compiler_params=pltpu.CompilerParams(
            dimension_semantics=("parallel", "parallel"), vmem_limit_bytes=VMEM_LIMIT),
        name="ada",
    )(c, ada_w, ada_b.reshape(DEPTH, 1, 3 * D_MODEL))


def _cmul(a_re, a_im, b_re, b_im):
    return a_re * b_re - a_im * b_im, a_re * b_im + a_im * b_re


def _dot_row_halves(a, b):
    half = a.shape[0] // 2
    return jnp.concatenate([jnp.dot(a[:half], b, preferred_element_type=F32),
                            jnp.dot(a[half:], b, preferred_element_type=F32)], axis=0)


def _proj_start(name):
    return sum(PROJ_SIZES[:PROJ_NAMES.index(name)])


def _proj_size(name):
    return PROJ_SIZES[PROJ_NAMES.index(name)]


def _inproj_kernel(x_ref, mod_ref, g_ref, w_ref, wq_ref,
                   bbd_ref, cbd_ref, tri_ref, row_ref, vec_ref, d_ref, wg_ref, bg_ref,
                   sink_ref, skf_ref, sqf_ref, *refs):
    out_refs = dict(zip(INPROJ_OUTPUTS, refs[:len(INPROJ_OUTPUTS)]))
    (uz_ref, xs_ref, carry_scr,
     qs_scr, ks_scr, vst_scr, zsw_scr, ss_ref) = refs[len(INPROJ_OUTPUTS):]
    step = pl.program_id(1)
    n = SSM_LANES
    nb = S5_CHUNK // S5_BLOCK
    nt = (((1,), (1,)), ((), ()))
    w = WINDOW

    @pl.when(step == 0)
    def _():
        carry_scr[...] = jnp.zeros_like(carry_scr)
        ks_scr[:w, :] = jnp.zeros((w, SWA_KV_WIDTH), BF16)
        vst_scr[:, :w] = jnp.zeros((SWA_KV_WIDTH, w), BF16)

    xf = x_ref[0]
    y = xf * lax.rsqrt(jnp.mean(xf * xf, axis=-1, keepdims=True) + RMS_EPS) * g_ref[...]
    mod = mod_ref[0]
    h = (y * (1.0 + mod[1:2, :]) + mod[0:1, :]).astype(BF16)

    def project(name):
        size = _proj_size(name)
        if name in PROJ_TRANSPOSED:
            c0 = _proj_start(name)
            rows = lax.dot_general(w_ref[:, c0:c0 + size], h, (((0,), (1,)), ((), ())),
                                   preferred_element_type=F32).astype(BF16)
            if name == "vd":
                for t in range(TOK_TILE // ATT_TK):
                    out_refs[name][0, t] = rows[:, t * ATT_TK:(t + 1) * ATT_TK]
            elif name == "vs":
                vst_scr[:, w:] = rows
            else:
                out_refs[name][0] = rows
        elif name == "qd":
            out_refs[name][0] = jnp.dot(h, wq_ref[:, :size], preferred_element_type=F32).astype(BF16)
        elif name == "qs":
            qs_scr[...] = _dot_row_halves(h, wq_ref[:, DIFF_WIDTH:]).astype(BF16)
        else:
            c0 = _proj_start(name)
            res = _dot_row_halves(h, w_ref[:, c0:c0 + size])
            if name == "ks":
                ks_scr[w:, :] = res.astype(BF16)
            elif name == "z_swa":
                zsw_scr[...] = res
            else:
                out_refs[name][0] = res.astype(BF16)

    dh = SWA_HEAD_DIM
    lane = lax.broadcasted_iota(jnp.int32, (w, 2 * dh), 1)
    ones_row = jnp.where(lax.broadcasted_iota(jnp.int32, (SWA_ACC_ROWS - dh, 2 * w), 0) == 0,
                         1.0, 0.0).astype(BF16)
    r_io = lax.broadcasted_iota(jnp.int32, (2 * w, w), 0)
    c_io = lax.broadcasted_iota(jnp.int32, (2 * w, w), 1)
    band = jnp.logical_and(r_io > c_io, r_io <= c_io + w)
    first = jnp.logical_and(band, jnp.logical_or(r_io >= w, step > 0))

    def swa_scores(blk):
        q = qs_scr[blk * w:(blk + 1) * w, :]
        qa, qb = q[:, :2 * dh], q[:, 2 * dh:]
        zero = jnp.zeros_like(qa)
        heads = (jnp.where(lane < dh, qa, zero), jnp.where(lane < dh, qb, zero),
                 jnp.where(lane >= dh, qa, zero), jnp.where(lane >= dh, qb, zero))
        qpp = jnp.concatenate([jnp.concatenate([heads[hd], sqf_ref[hd]], axis=1)
                               for hd in range(SWA_Q_HEADS)], axis=0)
        kpp = jnp.concatenate([ks_scr[blk * w:(blk + 2) * w, :], skf_ref[...]], axis=1)
        ss_ref[blk] = lax.dot_general(kpp, qpp, nt, preferred_element_type=F32)

    swa_exp = {}

    def swa_softmax(blk):
        valid = first if blk == 0 else band
        es, ms = [], []
        for hd in range(SWA_Q_HEADS):
            s = jnp.where(valid, ss_ref[blk, :, hd * w:(hd + 1) * w], -jnp.inf)
            m = jnp.maximum(jnp.max(s, axis=0, keepdims=True), sink_ref[hd][:, :1])
            es.append(jnp.exp(s - m).astype(BF16))
            ms.append(m)
        swa_exp[blk] = (es, ms)

    def swa_values(blk):
        es, ms = swa_exp.pop(blk)
        outs = []
        for kh in range(SWA_KV_HEADS):
            vta = jnp.concatenate([vst_scr[kh * dh:(kh + 1) * dh, blk * w:(blk + 2) * w], ones_row],
                                  axis=0)
            pair = es[SWA_GROUP * kh:SWA_GROUP * (kh + 1)]
            acc = jnp.dot(vta, jnp.concatenate(pair, axis=1), preferred_element_type=F32)
            for g in range(SWA_GROUP):
                hd = SWA_GROUP * kh + g
                denom = acc[dh:dh + 1, g * w:(g + 1) * w] + jnp.exp(sink_ref[hd][:, :1] - ms[hd])
                outs.append(acc[:dh, g * w:(g + 1) * w] * (1.0 / denom))
        o = jnp.concatenate(outs, axis=0).T
        z = zsw_scr[blk * w:(blk + 1) * w, :]
        out_refs["y_swa"][0, blk * w:(blk + 1) * w, :] = (o * _silu(z)).astype(BF16)

    uz_ref[...] = jnp.dot(h, w_ref[:, :2 * SSM_WIDTH], preferred_element_type=F32)

    n_blk = TOK_TILE // w
    gaps = iter((
        [lambda: project("qs")],
        [lambda: project("ks"), lambda: project("vs")],
        [lambda: project("z_swa")],
        [lambda b=b: swa_scores(b) for b in range(n_blk)],
        ([lambda b=b: swa_softmax(b) for b in range(n_blk // 2)] + [lambda: project("vd")]
         + [lambda b=b: swa_values(b) for b in range(n_blk // 2)]),
        ([lambda b=b: swa_softmax(b) for b in range(n_blk // 2, n_blk)] + [lambda: project("z_diff")]
         + [lambda b=b: swa_values(b) for b in range(n_blk // 2, n_blk)]),
        [lambda: project("qd")],
        [lambda: project("kd")],
    ))

    def fill_gap():
        for job in next(gaps):
            job()

    for ci in range(TOK_TILE // S5_CHUNK):
        rows = slice(ci * S5_CHUNK, (ci + 1) * S5_CHUNK)
        u = uz_ref[rows, :SSM_WIDTH].astype(BF16)
        bu = jnp.dot(u, bbd_ref[...], preferred_element_type=F32)
        bu = bu.astype(BF16).reshape(nb, S5_BLOCK, 2 * n)
        z_re, z_im = _cmul(row_ref[0], row_ref[1], bu[:, :, :n], bu[:, :, n:])
        zs = jnp.concatenate([z_re, z_im], axis=-1).reshape(S5_CHUNK, 2 * n)
        fill_gap()

        xs_ref[...] = jnp.dot(tri_ref[...], zs, preferred_element_type=F32)
        ends = jnp.concatenate([xs_ref[(k + 1) * S5_BLOCK - 1:(k + 1) * S5_BLOCK, :]
                                for k in range(nb)], axis=0)
        e_re, e_im = _cmul(vec_ref[0:1], vec_ref[1:2], ends[:, :n], ends[:, n:])
        s_re = carry_scr[0:1, :]
        s_im = carry_scr[1:2, :]
        inj_re, inj_im = [], []
        for k in range(nb):
            g_re, g_im = _cmul(vec_ref[4:5], vec_ref[5:6], s_re, s_im)
            inj_re.append(g_re)
            inj_im.append(g_im)
            d_re, d_im = _cmul(vec_ref[2:3], vec_ref[3:4], s_re, s_im)
            s_re = e_re[k:k + 1] + d_re
            s_im = e_im[k:k + 1] + d_im
        carry_scr[0:1, :] = s_re
        carry_scr[1:2, :] = s_im
        inj_re = jnp.concatenate(inj_re, axis=0)[:, None, :]
        inj_im = jnp.concatenate(inj_im, axis=0)[:, None, :]
        xs = xs_ref[...].reshape(nb, S5_BLOCK, 2 * n)
        x_re, x_im = _cmul(row_ref[2], row_ref[3], (xs[:, :, :n] + inj_re).astype(BF16),
                           (xs[:, :, n:] + inj_im).astype(BF16))
        st = jnp.concatenate([x_re, x_im], axis=-1).reshape(S5_CHUNK, 2 * n)
        fill_gap()

        ys = _dot_row_halves(st, cbd_ref[...])
        ys = jax.nn.gelu(ys + d_ref[...] * u.astype(F32))
        fill_gap()

        gate = jnp.dot(ys.astype(BF16), wg_ref[...], preferred_element_type=F32) + bg_ref[...]
        ys = ys * jax.nn.sigmoid(gate)
        out_refs["y_ssm"][0, rows, :] = (ys * _silu(uz_ref[rows, SSM_WIDTH:])).astype(BF16)
        fill_gap()

    ks_scr[:w, :] = ks_scr[TOK_TILE:, :]
    vst_scr[:, :w] = vst_scr[:, TOK_TILE:]


def _inproj_params(norm_gain, w_in, s5_tables, ssm_d, glu_w, glu_b, swa_sinks):
    w_main = w_in.astype(BF16)
    qd0, qs0 = _proj_start("qd"), _proj_start("qs")
    dh = SWA_HEAD_DIM
    w_q = jnp.concatenate(
        [w_in[:, :, qd0:qd0 + DIFF_WIDTH].astype(F32) * (DIFF_HEAD_DIM ** -0.5 * LOG2E)]
        + [w_in[:, :, qs0 + h * dh:qs0 + (h + 1) * dh].astype(F32) * dh ** -0.5 for h in (0, 2, 1, 3)],
        axis=2).astype(BF16)
    bbd, cbd, row_tabs, vec_tabs = s5_tables
    depth = w_in.shape[0]
    sink_rows = jnp.broadcast_to(swa_sinks.astype(F32).reshape(depth, SWA_Q_HEADS, 1, 1),
                                 (depth, SWA_Q_HEADS, 1, LANES))
    return (norm_gain.astype(F32).reshape(depth, 1, D_MODEL), w_main, w_q,
            bbd, cbd, row_tabs, vec_tabs,
            ssm_d.astype(F32).reshape(depth, 1, SSM_WIDTH), glu_w.astype(BF16),
            glu_b.astype(F32).reshape(depth, 1, SSM_WIDTH), sink_rows)


def _inproj(layer, x, mod, params):
    bsz, seq, _ = x.shape
    grid = (bsz, seq // TOK_TILE)
    t = np.arange(S5_CHUNK, dtype=np.int32)
    tri = np.logical_and(t[:, None] // S5_BLOCK == t[None, :] // S5_BLOCK,
                         t[:, None] >= t[None, :]).astype(BF16)
    swa_kf, swa_qf = _swa_features()
    tiles_per_step = TOK_TILE // ATT_TK
    out_specs, out_shape = [], []
    for name in INPROJ_OUTPUTS:
        n = {"y_ssm": SSM_WIDTH, "y_swa": SWA_WIDTH}.get(name) or _proj_size(name)
        if name == "vd":
            out_specs.append(pl.BlockSpec((1, tiles_per_step, n, ATT_TK), lambda b, i: (b, i, 0, 0)))
            out_shape.append(jax.ShapeDtypeStruct((bsz, seq // ATT_TK, n, ATT_TK), BF16))
        elif name in PROJ_TRANSPOSED:
            out_specs.append(pl.BlockSpec((1, n, TOK_TILE), lambda b, i: (b, 0, i)))
            out_shape.append(jax.ShapeDtypeStruct((bsz, n, seq), BF16))
        else:
            out_specs.append(pl.BlockSpec((1, TOK_TILE, n), lambda b, i: (b, i, 0)))
            out_shape.append(jax.ShapeDtypeStruct((bsz, seq, n), BF16))
    full = lambda a: pl.BlockSpec(a.shape, lambda b, i: (0,) * a.ndim)
    of_layer = lambda a: pl.BlockSpec((None,) + a.shape[1:],
                                      lambda b, i: (layer,) + (0,) * (a.ndim - 1))
    (gain, w_main, w_q, bbd, cbd, row_tabs, vec_tabs, d_skip, w_glu, b_glu, sink_rows) = params
    operands = [(gain, of_layer), (w_main, of_layer), (w_q, of_layer),
                (bbd, of_layer), (cbd, of_layer), (tri, full), (row_tabs, of_layer),
                (vec_tabs, of_layer), (d_skip, of_layer), (w_glu, of_layer), (b_glu, of_layer),
                (sink_rows, of_layer), (swa_kf, full), (swa_qf, full)]
    outs = pl.pallas_call(
        _inproj_kernel,
        grid=grid,
        in_specs=[
            pl.BlockSpec((1, TOK_TILE, D_MODEL), lambda b, i: (b, i, 0)),
            pl.BlockSpec((None, 1, 3, D_MODEL), lambda b, i: (layer, b, 0, 0)),
        ] + [spec(a) for a, spec in operands],
        out_specs=out_specs,
        out_shape=out_shape,
        scratch_shapes=[
            pltpu.VMEM((TOK_TILE, 2 * SSM_WIDTH), F32),
            pltpu.VMEM((S5_CHUNK, 2 * SSM_LANES), F32),
            pltpu.VMEM((SUBLANES, SSM_LANES), F32),
            pltpu.VMEM((TOK_TILE, SWA_WIDTH), BF16),
            pltpu.VMEM((WINDOW + TOK_TILE, SWA_KV_WIDTH), BF16),
            pltpu.VMEM((SWA_KV_WIDTH, WINDOW + TOK_TILE), BF16),
            pltpu.VMEM((TOK_TILE, SWA_WIDTH), F32),
            pltpu.VMEM((TOK_TILE // WINDOW, 2 * WINDOW, SWA_Q_HEADS * WINDOW), F32),
        ],
        compiler_params=pltpu.CompilerParams(
            dimension_semantics=("parallel", "arbitrary"), vmem_limit_bytes=VMEM_LIMIT),
        name="inproj",
    )(x, mod, *[a for a, _ in operands])
    return dict(zip(INPROJ_OUTPUTS, outs))


def _s5_tables(lam_re, lam_im, log_step, b_re, b_im, c_re, c_im):
    g, p, h = SSM_GROUPS, SSM_STATE, SSM_GROUP
    step = jnp.exp(log_step.astype(F32))[:, None]
    lr = lam_re.astype(F32)
    li = lam_im.astype(F32)
    mag = jnp.exp(lr * step)
    ang = li * step
    ab_re = mag * jnp.cos(ang)
    ab_im = mag * jnp.sin(ang)
    den = lr * lr + li * li
    f_re = ((ab_re - 1.0) * lr + ab_im * li) / den
    f_im = (ab_im * lr - (ab_re - 1.0) * li) / den
    br = b_re.astype(F32)
    bi = b_im.astype(F32)
    bb_re = f_re[..., None] * br - f_im[..., None] * bi
    bb_im = f_re[..., None] * bi + f_im[..., None] * br
    eye = jnp.eye(g, dtype=F32)
    bbd_re = jnp.einsum("gph,gk->ghkp", bb_re, eye).reshape(g * h, g * p)
    bbd_im = jnp.einsum("gph,gk->ghkp", bb_im, eye).reshape(g * h, g * p)
    bbd = jnp.concatenate([bbd_re, bbd_im], axis=1).astype(BF16)
    cbd_re = jnp.einsum("ghp,gk->gpkh", c_re.astype(F32), eye).reshape(g * p, g * h)
    cbd_im = jnp.einsum("ghp,gk->gpkh", c_im.astype(F32), eye).reshape(g * p, g * h)
    cbd = jnp.concatenate([cbd_re, -cbd_im], axis=0).astype(BF16)

    def power(n):
        n = jnp.asarray(n, F32).reshape(-1, 1, 1)
        m = jnp.exp(lr * step * n)
        return ((m * jnp.cos(ang * n)).reshape(-1, g * p), (m * jnp.sin(ang * n)).reshape(-1, g * p))

    c = S5_BLOCK // 2
    rows = jnp.arange(S5_BLOCK, dtype=F32)
    pre = power(c - rows)
    post = power(rows - c)
    row_tabs = jnp.stack([pre[0], pre[1], post[0], post[1]]).astype(BF16)
    vec = [power(float(S5_BLOCK - 1 - c)),
           power(float(S5_BLOCK)),
           power(float(c + 1))]
    vec_tabs = jnp.concatenate([t for pair in vec for t in pair], axis=0)
    return bbd, cbd, row_tabs, vec_tabs


ACC_ROWS = 2 * DIFF_HEAD_DIM + 16
POS_SPLIT = 64


LOG2E_PARTS = 3


def _diff_features(seq):
    parts, rest = [], LOG2E
    for _ in range(LOG2E_PARTS):
        part = float(np.float32(rest).astype(BF16))
        parts.append(part)
        rest -= part
    pos = np.arange(seq, dtype=np.int32)
    hi = ((pos // POS_SPLIT) * POS_SPLIT).astype(np.float32)
    lo = (pos % POS_SPLIT).astype(np.float32)
    lane = (np.arange(2 * DIFF_HEAD_DIM, dtype=np.int32) % DIFF_HEAD_DIM)[None, :]
    qf = sum(np.where(lane % LOG2E_PARTS == i, np.float32(part), np.float32(0)) for i, part in enumerate(parts))
    qf = np.where(lane < 2 * LOG2E_PARTS, qf, np.float32(0)).astype(BF16)
    tabs = []
    for h in range(DIFF_HEADS):
        slope = np.float32(_alibi_slope(SWA_Q_HEADS + h))
        tabs.append(np.where(lane < LOG2E_PARTS, slope * hi[:, None], np.float32(0))
                    + np.where(np.logical_and(lane >= LOG2E_PARTS, lane < 2 * LOG2E_PARTS),
                               slope * lo[:, None], np.float32(0)))
    return np.stack(tabs).astype(BF16), qf


def _diff_kernel(q_ref, k_ref, kf_ref, qf_ref, vt_ref, zt_ref, lam_ref, gain_ref, o_ref,
                 acc_ref, s_ref, qp_ref, m_ref, msub_ref, alpha_ref, *, lam_init):
    qi = pl.program_id(1)
    dh = DIFF_HEAD_DIM
    hw = 2 * dh
    lane_q = lax.broadcasted_iota(jnp.int32, (ATT_TQ, hw), 1)
    q_feat = jnp.broadcast_to(qf_ref[...], (ATT_TQ, hw))
    units = [(b, h) for b in range(q_ref.shape[0]) for h in range(DIFF_HEADS)]
    for u, (b, h) in enumerate(units):
        q = q_ref[b, :, h * hw:(h + 1) * hw]
        qp_ref[2 * u] = jnp.where(lane_q < dh, q, q_feat)
        qp_ref[2 * u + 1] = jnp.where(lane_q >= dh, q, q_feat)
    lane_k = lax.broadcasted_iota(jnp.int32, (ATT_TK, hw), 1)
    ones_row = jnp.where(lax.broadcasted_iota(jnp.int32, (ACC_ROWS - hw, ATT_TK), 0) == 0,
                         1.0, 0.0).astype(BF16)
    krow = lax.broadcasted_iota(jnp.int32, (ATT_TK, ATT_TQ), 0)
    qcol = lax.broadcasted_iota(jnp.int32, (ATT_TK, ATT_TQ), 1)
    nt = (((1,), (1,)), ((), ()))

    acc_ref[...] = jnp.zeros_like(acc_ref)
    m_ref[...] = jnp.full(m_ref.shape, -jnp.inf, F32)

    def scores(t, buf, masked):
        k0 = pl.multiple_of(t * ATT_TK, ATT_TK)
        for u, (b, h) in enumerate(units):
            k = k_ref[b, pl.ds(k0, ATT_TK), h * hw:(h + 1) * hw]
            kf = kf_ref[h, pl.ds(k0, ATT_TK), :]
            for c, kp in ((2 * u, jnp.where(lane_k < dh, k, kf)),
                          (2 * u + 1, jnp.where(lane_k >= dh, k, kf))):
                s = lax.dot_general(kp, qp_ref[c], nt, preferred_element_type=F32)
                if masked:
                    s = jnp.where(krow <= qcol, s, -jnp.inf)
                s_ref[buf, c] = s
                m_old = m_ref[c]
                m_new = jnp.maximum(m_old, jnp.max(s, axis=0, keepdims=True))
                m_ref[c] = m_new
                msub_ref[buf, c] = m_new
                alpha_ref[buf, c] = jnp.exp2(m_old - m_new)

    def values(t, buf):
        for u, (b, h) in enumerate(units):
            vta = jnp.concatenate([vt_ref[b, t, h * hw:(h + 1) * hw, :], ones_row], axis=0)
            for c in (2 * u, 2 * u + 1):
                p = jnp.exp2(s_ref[buf, c] - msub_ref[buf, c]).astype(BF16)
                acc_ref[c] = (alpha_ref[buf, c] * acc_ref[c]
                              + jnp.dot(vta, p, preferred_element_type=F32))

    pairs = jnp.maximum(qi - 1, 0) // 2
    rest = qi - 1 - 2 * pairs
    t0 = 2 * pairs

    @pl.when(qi == 0)
    def _():
        scores(0, 0, True)
        values(0, 0)

    @pl.when(qi >= 1)
    def _():
        scores(0, 0, False)

    def two_tiles(i, carry):
        t = 2 * i
        scores(t + 1, 1, False)
        values(t, 0)
        scores(t + 2, 0, False)
        values(t + 1, 1)
        return carry

    lax.fori_loop(0, pairs, two_tiles, 0)

    @pl.when(jnp.logical_and(qi >= 1, rest == 1))
    def _():
        scores(t0 + 1, 1, False)
        values(t0, 0)
        scores(t0 + 2, 0, True)
        values(t0 + 1, 1)
        values(t0 + 2, 0)

    @pl.when(jnp.logical_and(qi >= 1, rest == 0))
    def _():
        scores(t0 + 1, 1, True)
        values(t0, 0)
        values(t0 + 1, 1)

    lp = lam_ref[...]
    lam = (jnp.exp(jnp.sum(lp[0:1] * lp[1:2], axis=-1, keepdims=True))
           - jnp.exp(jnp.sum(lp[2:3] * lp[3:4], axis=-1, keepdims=True)) + lam_init)
    gain = gain_ref[...] * (1.0 - lam_init)
    for u, (b, h) in enumerate(units):
        a1 = acc_ref[2 * u]
        a2 = acc_ref[2 * u + 1]
        o_t = (a1[:hw] * (1.0 / a1[hw:hw + 1]) - lam * (a2[:hw] * (1.0 / a2[hw:hw + 1])))
        o_t = o_t * lax.rsqrt(jnp.mean(o_t * o_t, axis=0, keepdims=True) + RMS_EPS) * gain
        z_t = zt_ref[b, h * hw:(h + 1) * hw, :].astype(F32)
        o_ref[b, h * hw:(h + 1) * hw, :] = (o_t * _silu(z_t)).astype(BF16)


def _diff_attention(qd, kd, vd_t, zd_t, lam_params, subln_gain, lam_init):
    bsz, seq, _ = qd.shape
    hw = 2 * DIFF_HEAD_DIM
    n_kv = seq // ATT_TK
    nb = DIFF_STREAMS
    chains = 2 * DIFF_HEADS * nb
    kf, qf = _diff_features(seq)
    gain_rows = jnp.broadcast_to(subln_gain.astype(F32).reshape(hw, 1), (hw, ATT_TQ))
    tile_t = pl.BlockSpec((nb, DIFF_WIDTH, ATT_TQ), lambda b, i: (b, 0, i))
    return pl.pallas_call(
        functools.partial(_diff_kernel, lam_init=lam_init),
        grid=(bsz // nb, seq // ATT_TQ),
        in_specs=[
            pl.BlockSpec((nb, ATT_TQ, DIFF_WIDTH), lambda b, i: (b, i, 0)),
            pl.BlockSpec((nb, seq, DIFF_WIDTH), lambda b, i: (b, 0, 0)),
            pl.BlockSpec((DIFF_HEADS, seq, hw), lambda b, i: (0, 0, 0)),
            pl.BlockSpec((1, hw), lambda b, i: (0, 0)),
            pl.BlockSpec((nb, n_kv, DIFF_WIDTH, ATT_TK), lambda b, i: (b, 0, 0, 0)),
            tile_t,
            pl.BlockSpec((4, DIFF_HEAD_DIM), lambda b, i: (0, 0)),
            pl.BlockSpec((hw, ATT_TQ), lambda b, i: (0, 0)),
        ],
        out_specs=tile_t,
        out_shape=jax.ShapeDtypeStruct((bsz, DIFF_WIDTH, seq), BF16),
        scratch_shapes=[pltpu.VMEM((chains, ACC_ROWS, ATT_TQ), F32),
                        pltpu.VMEM((2, chains, ATT_TK, ATT_TQ), F32),
                        pltpu.VMEM((chains, ATT_TQ, hw), BF16),
                        pltpu.VMEM((chains, 1, ATT_TQ), F32),
                        pltpu.VMEM((2, chains, 1, ATT_TQ), F32),
                        pltpu.VMEM((2, chains, 1, ATT_TQ), F32)],
        compiler_params=pltpu.CompilerParams(
            dimension_semantics=("parallel", "parallel"), vmem_limit_bytes=VMEM_LIMIT),
        name="diffattn",
    )(qd, kd, kf, qf, vd_t, zd_t, lam_params, gain_rows)


SWA_ACC_ROWS = SWA_HEAD_DIM + 16


def _swa_features():
    w = WINDOW
    zero = np.float32(0)
    lane = np.arange(LANES, dtype=np.int32)[None, :]
    r = np.arange(2 * w, dtype=np.float32)[:, None]
    kf = np.where(lane == 0, r, zero) + np.where(lane == 1, np.float32(1), zero)
    c = np.arange(w, dtype=np.float32)[:, None]
    qf = np.stack([np.where(lane == 0, np.float32(_alibi_slope(h)), zero)
                   + np.where(lane == 1, np.float32(-_alibi_slope(h)) * (w + c), zero)
                   for h in range(SWA_Q_HEADS)])
    return kf.astype(BF16), qf.astype(BF16)


def _outproj_kernel(x_ref, ys_ref, yd_ref, yw_ref, mod_ref, w_ref, fg_ref, o_ref, *, final_norm):
    a = SSM_WIDTH
    b = SSM_WIDTH + DIFF_WIDTH
    y = (jnp.dot(ys_ref[0], w_ref[:a, :], preferred_element_type=F32)
         + lax.dot_general(yd_ref[0], w_ref[a:b, :], (((0,), (0,)), ((), ())),
                           preferred_element_type=F32)
         + jnp.dot(yw_ref[0], w_ref[b:, :], preferred_element_type=F32))
    out = x_ref[0] + mod_ref[0][2:3, :] * y
    if final_norm:
        out = out * lax.rsqrt(jnp.mean(out * out, axis=-1, keepdims=True) + RMS_EPS) * fg_ref[...]
    o_ref[0] = out


def _outproj(layer, x, y_ssm, y_diff, y_swa, mod, w_out_bf16, final_gain, final_norm):
    bsz, seq, _ = x.shape
    tok = lambda n: pl.BlockSpec((1, OUT_TILE, n), lambda b, i: (b, i, 0))
    return pl.pallas_call(
        functools.partial(_outproj_kernel, final_norm=final_norm),
        grid=(bsz, seq // OUT_TILE),
        in_specs=[
            tok(D_MODEL), tok(SSM_WIDTH),
            pl.BlockSpec((1, DIFF_WIDTH, OUT_TILE), lambda b, i: (b, 0, i)),
            tok(SWA_WIDTH),
            pl.BlockSpec((None, 1, 3, D_MODEL), lambda b, i: (layer, b, 0, 0)),
            pl.BlockSpec((None, D_MODEL, D_MODEL), lambda b, i: (layer, 0, 0)),
            pl.BlockSpec((1, D_MODEL), lambda b, i: (0, 0)),
        ],
        out_specs=tok(D_MODEL),
        out_shape=jax.ShapeDtypeStruct((bsz, seq, D_MODEL), F32),
        compiler_params=pltpu.CompilerParams(
            dimension_semantics=("parallel", "parallel"), vmem_limit_bytes=OUTPROJ_VMEM_LIMIT),
        name="outproj",
    )(x, y_ssm, y_diff, y_swa, mod, w_out_bf16, final_gain.reshape(1, D_MODEL).astype(F32))


def kernel(x, c, norm_gain, ada_w, ada_b, w_in, w_out, ssm_lam_re, ssm_lam_im, ssm_log_step,
           ssm_b_re, ssm_b_im, ssm_c_re, ssm_c_im, ssm_d, glu_w, glu_b,
           diff_lq1, diff_lk1, diff_lq2, diff_lk2, diff_subln, swa_sinks, final_gain):
    bsz = x.shape[0]
    mod = _ada(c, ada_w, ada_b).reshape(DEPTH, bsz, 3, D_MODEL)
    s5_tables = jax.vmap(_s5_tables)(ssm_lam_re, ssm_lam_im, ssm_log_step,
                                     ssm_b_re, ssm_b_im, ssm_c_re, ssm_c_im)
    params = _inproj_params(norm_gain, w_in, s5_tables, ssm_d, glu_w, glu_b, swa_sinks)
    w_out_bf16 = w_out.astype(BF16)
    for l in range(DEPTH):
        proj = _inproj(l, x, mod, params)
        lam_init = 0.8 - 0.6 * math.exp(-0.3 * l)
        lam_params = jnp.stack([diff_lq1[l], diff_lk1[l], diff_lq2[l], diff_lk2[l]]).astype(F32)
        y_diff = _diff_attention(proj["qd"], proj["kd"], proj["vd"], proj["z_diff"],
                                 lam_params, diff_subln[l], lam_init)
        x = _outproj(l, x, proj["y_ssm"], y_diff, proj["y_swa"], mod, w_out_bf16,
                     final_gain, final_norm=(l == DEPTH - 1))
    return x
```

```python
import functools
import math

import jax
import jax.numpy as jnp
import numpy as np
from jax import lax
from jax.experimental import pallas as pl
from jax.experimental.pallas import tpu as pltpu

F32 = jnp.float32
BF16 = jnp.bfloat16

D_MODEL = 1024
DEPTH = 2
SSM_GROUP = 16
SSM_GROUPS = 16
SSM_WIDTH = 256
SSM_STATE = 64
SSM_LANES = SSM_GROUPS * SSM_STATE
DIFF_HEADS = 4
DIFF_HEAD_DIM = 64
DIFF_WIDTH = 512
SWA_Q_HEADS = 4
SWA_KV_HEADS = 2
SWA_GROUP = 2
SWA_HEAD_DIM = 64
SWA_WIDTH = 256
SWA_KV_WIDTH = 128
WINDOW = 128
N_ATTN_HEADS = 8
RMS_EPS = 1e-6

PROJ_NAMES = ("u", "z_ssm", "qd", "kd", "vd", "z_diff", "qs", "ks", "vs", "z_swa")
PROJ_SIZES = (SSM_WIDTH, SSM_WIDTH, DIFF_WIDTH, DIFF_WIDTH, DIFF_WIDTH, DIFF_WIDTH,
              SWA_WIDTH, SWA_KV_WIDTH, SWA_KV_WIDTH, SWA_WIDTH)
IN_COLS = sum(PROJ_SIZES)
PROJ_TRANSPOSED = ("vd", "z_diff", "vs")
INPROJ_OUTPUTS = ("y_ssm", "y_swa", "qd", "kd", "vd", "z_diff")
LOG2E = math.log2(math.e)

SUBLANES = 8
LANES = 128
VMEM_LIMIT = 48 * 1024 * 1024
OUTPROJ_VMEM_LIMIT = 56 * 1024 * 1024

TOK_TILE = 512
OUT_TILE = 1024
S5_CHUNK = 256
S5_BLOCK = 32
ATT_TQ = 256
ATT_TK = 256
DIFF_STREAMS = 2


def _silu(x):
    return x * jax.nn.sigmoid(x)


def _alibi_slope(head_index):
    return 2.0 ** (-(head_index + 1) * (8.0 / N_ATTN_HEADS))


def _ada_kernel(c_ref, w_ref, b_ref, o_ref):
    cond = _silu(c_ref[...])
    o_ref[0] = jnp.dot(cond, w_ref[0], preferred_element_type=F32) + b_ref[0]


def _ada(c, ada_w, ada_b):
    bsz = c.shape[0]
    col = D_MODEL
    return pl.pallas_call(
        _ada_kernel,
        grid=(DEPTH, 3),
        in_specs=[
            pl.BlockSpec((bsz, D_MODEL), lambda l, j: (0, 0)),
            pl.BlockSpec((1, D_MODEL, col), lambda l, j: (l, 0, j)),
            pl.BlockSpec((1, 1, col), lambda l, j: (l, 0, j)),
        ],
        out_specs=pl.BlockSpec((1, bsz, col), lambda l, j: (l, 0, j)),
        out_shape=jax.ShapeDtypeStruct((DEPTH, bsz, 3 * D_MODEL), F32),
        compiler_params=pltpu.CompilerParams(
            dimension_semantics=("parallel", "parallel"), vmem_limit_bytes=VMEM_LIMIT),
        name="ada",
    )(c, ada_w, ada_b.reshape(DEPTH, 1, 3 * D_MODEL))


def _cmul(a_re, a_im, b_re, b_im):
    return a_re * b_re - a_im * b_im, a_re * b_im + a_im * b_re


def _dot_row_halves(a, b):
    half = a.shape[0] // 2
    return jnp.concatenate([jnp.dot(a[:half], b, preferred_element_type=F32),
                            jnp.dot(a[half:], b, preferred_element_type=F32)], axis=0)


def _proj_start(name):
    return sum(PROJ_SIZES[:PROJ_NAMES.index(name)])


def _proj_size(name):
    return PROJ_SIZES[PROJ_NAMES.index(name)]


def _inproj_kernel(x_ref, mod_ref, g_ref, w_ref, wq_ref,
                   bbd_ref, cbd_ref, tri_ref, row_ref, vec_ref, d_ref, wg_ref, bg_ref,
                   sink_ref, skf_ref, sqf_ref, *refs):
    out_refs = dict(zip(INPROJ_OUTPUTS, refs[:len(INPROJ_OUTPUTS)]))
    (uz_ref, xs_ref, carry_scr,
     qs_scr, ks_scr, vst_scr, zsw_scr, ss_ref) = refs[len(INPROJ_OUTPUTS):]
    step = pl.program_id(1)
    n = SSM_LANES
    nb = S5_CHUNK // S5_BLOCK
    nt = (((1,), (1,)), ((), ()))
    w = WINDOW

    @pl.when(step == 0)
    def _():
        carry_scr[...] = jnp.zeros_like(carry_scr)
        ks_scr[:w, :] = jnp.zeros((w, SWA_KV_WIDTH), BF16)
        vst_scr[:, :w] = jnp.zeros((SWA_KV_WIDTH, w), BF16)

    mod = mod_ref[0]
    gain = g_ref[...] * (1.0 + mod[1:2, :])

    def normed(rows):
        xf = x_ref[0, rows, :]
        rstd = lax.rsqrt(jnp.mean(xf * xf, axis=-1, keepdims=True) + RMS_EPS)
        return (xf * rstd * gain + mod[0:1, :]).astype(BF16)

    h_halves = [normed(slice(0, TOK_TILE // 2)), normed(slice(TOK_TILE // 2, TOK_TILE))]
    h = jnp.concatenate(h_halves, axis=0)

    def project(name):
        size = _proj_size(name)
        if name in PROJ_TRANSPOSED:
            c0 = _proj_start(name)
            rows = lax.dot_general(w_ref[:, c0:c0 + size], h, (((0,), (1,)), ((), ())),
                                   preferred_element_type=F32).astype(BF16)
            if name == "vd":
                for t in range(TOK_TILE // ATT_TK):
                    out_refs[name][0, t] = rows[:, t * ATT_TK:(t + 1) * ATT_TK]
            elif name == "vs":
                vst_scr[:, w:] = rows
            else:
                out_refs[name][0] = rows
        elif name == "qd":
            out_refs[name][0] = jnp.dot(h, wq_ref[:, :size], preferred_element_type=F32).astype(BF16)
        elif name == "qs":
            qs_scr[...] = _dot_row_halves(h, wq_ref[:, DIFF_WIDTH:]).astype(BF16)
        else:
            c0 = _proj_start(name)
            res = _dot_row_halves(h, w_ref[:, c0:c0 + size])
            if name == "ks":
                ks_scr[w:, :] = res.astype(BF16)
            elif name == "z_swa":
                zsw_scr[...] = res
            else:
                out_refs[name][0] = res.astype(BF16)

    dh = SWA_HEAD_DIM
    lane = lax.broadcasted_iota(jnp.int32, (w, 2 * dh), 1)
    ones_row = jnp.where(lax.broadcasted_iota(jnp.int32, (SWA_ACC_ROWS - dh, 2 * w), 0) == 0,
                         1.0, 0.0).astype(BF16)
    r_io = lax.broadcasted_iota(jnp.int32, (2 * w, w), 0)
    c_io = lax.broadcasted_iota(jnp.int32, (2 * w, w), 1)
    band = jnp.logical_and(r_io > c_io, r_io <= c_io + w)
    first = jnp.logical_and(band, jnp.logical_or(r_io >= w, step > 0))

    def swa_scores(blk):
        q = qs_scr[blk * w:(blk + 1) * w, :]
        qa, qb = q[:, :2 * dh], q[:, 2 * dh:]
        zero = jnp.zeros_like(qa)
        heads = (jnp.where(lane < dh, qa, zero), jnp.where(lane < dh, qb, zero),
                 jnp.where(lane >= dh, qa, zero), jnp.where(lane >= dh, qb, zero))
        qpp = jnp.concatenate([jnp.concatenate([heads[hd], sqf_ref[hd]], axis=1)
                               for hd in range(SWA_Q_HEADS)], axis=0)
        kpp = jnp.concatenate([ks_scr[blk * w:(blk + 2) * w, :], skf_ref[...]], axis=1)
        ss_ref[blk] = lax.dot_general(kpp, qpp, nt, preferred_element_type=F32)

    swa_exp = {}

    def swa_softmax(blk):
        valid = first if blk == 0 else band
        es, ms = [], []
        for hd in range(SWA_Q_HEADS):
            s = jnp.where(valid, ss_ref[blk, :, hd * w:(hd + 1) * w], -jnp.inf)
            m = jnp.maximum(jnp.max(s, axis=0, keepdims=True), sink_ref[hd][:, :1])
            es.append(jnp.exp(s - m).astype(BF16))
            ms.append(m)
        swa_exp[blk] = (es, ms)

    def swa_values(blk):
        es, ms = swa_exp.pop(blk)
        outs = []
        for kh in range(SWA_KV_HEADS):
            vta = jnp.concatenate([vst_scr[kh * dh:(kh + 1) * dh, blk * w:(blk + 2) * w], ones_row],
                                  axis=0)
            pair = es[SWA_GROUP * kh:SWA_GROUP * (kh + 1)]
            acc = jnp.dot(vta, jnp.concatenate(pair, axis=1), preferred_element_type=F32)
            for g in range(SWA_GROUP):
                hd = SWA_GROUP * kh + g
                denom = acc[dh:dh + 1, g * w:(g + 1) * w] + jnp.exp(sink_ref[hd][:, :1] - ms[hd])
                outs.append(acc[:dh, g * w:(g + 1) * w] * (1.0 / denom))
        o = jnp.concatenate(outs, axis=0).T
        z = zsw_scr[blk * w:(blk + 1) * w, :]
        out_refs["y_swa"][0, blk * w:(blk + 1) * w, :] = (o * _silu(z)).astype(BF16)

    for half, h_half in enumerate(h_halves):
        uz_ref[half * (TOK_TILE // 2):(half + 1) * (TOK_TILE // 2), :] = jnp.dot(
            h_half, w_ref[:, :2 * SSM_WIDTH], preferred_element_type=F32)

    n_blk = TOK_TILE // w
    gaps = iter((
        [lambda: project("qs")],
        [lambda: project("ks"), lambda: project("vs")],
        [lambda: project("z_swa")],
        [lambda b=b: swa_scores(b) for b in range(n_blk)],
        ([lambda b=b: swa_softmax(b) for b in range(n_blk // 2)] + [lambda: project("vd")]
         + [lambda b=b: swa_values(b) for b in range(n_blk // 2)]),
        ([lambda b=b: swa_softmax(b) for b in range(n_blk // 2, n_blk)] + [lambda: project("z_diff")]
         + [lambda b=b: swa_values(b) for b in range(n_blk // 2, n_blk)]),
        [lambda: project("qd")],
        [lambda: project("kd")],
    ))

    def fill_gap():
        for job in next(gaps):
            job()

    for ci in range(TOK_TILE // S5_CHUNK):
        rows = slice(ci * S5_CHUNK, (ci + 1) * S5_CHUNK)
        u = uz_ref[rows, :SSM_WIDTH].astype(BF16)
        bu = jnp.dot(u, bbd_ref[...], preferred_element_type=F32)
        bu = bu.astype(BF16).reshape(nb, S5_BLOCK, 2 * n)
        z_re, z_im = _cmul(row_ref[0], row_ref[1], bu[:, :, :n], bu[:, :, n:])
        zs = jnp.concatenate([z_re, z_im], axis=-1).reshape(S5_CHUNK, 2 * n)
        fill_gap()

        xs_ref[...] = jnp.dot(tri_ref[...], zs, preferred_element_type=F32)
        ends = jnp.concatenate([xs_ref[(k + 1) * S5_BLOCK - 1:(k + 1) * S5_BLOCK, :]
                                for k in range(nb)], axis=0)
        e_re, e_im = _cmul(vec_ref[0:1], vec_ref[1:2], ends[:, :n], ends[:, n:])
        s_re = carry_scr[0:1, :]
        s_im = carry_scr[1:2, :]
        inj_re, inj_im = [], []
        for k in range(nb):
            g_re, g_im = _cmul(vec_ref[4:5], vec_ref[5:6], s_re, s_im)
            inj_re.append(g_re)
            inj_im.append(g_im)
            d_re, d_im = _cmul(vec_ref[2:3], vec_ref[3:4], s_re, s_im)
            s_re = e_re[k:k + 1] + d_re
            s_im = e_im[k:k + 1] + d_im
        carry_scr[0:1, :] = s_re
        carry_scr[1:2, :] = s_im
        inj_re = jnp.concatenate(inj_re, axis=0)[:, None, :]
        inj_im = jnp.concatenate(inj_im, axis=0)[:, None, :]
        xs = xs_ref[...].reshape(nb, S5_BLOCK, 2 * n)
        x_re, x_im = _cmul(row_ref[2], row_ref[3], (xs[:, :, :n] + inj_re).astype(BF16),
                           (xs[:, :, n:] + inj_im).astype(BF16))
        st = jnp.concatenate([x_re, x_im], axis=-1).reshape(S5_CHUNK, 2 * n)
        fill_gap()

        ys = _dot_row_halves(st, cbd_ref[...])
        ys = jax.nn.gelu(ys + d_ref[...] * u.astype(F32))
        fill_gap()

        gate = jnp.dot(ys.astype(BF16), wg_ref[...], preferred_element_type=F32) + bg_ref[...]
        ys = ys * jax.nn.sigmoid(gate)
        out_refs["y_ssm"][0, rows, :] = (ys * _silu(uz_ref[rows, SSM_WIDTH:])).astype(BF16)
        fill_gap()

    ks_scr[:w, :] = ks_scr[TOK_TILE:, :]
    vst_scr[:, :w] = vst_scr[:, TOK_TILE:]


def _inproj_params(norm_gain, w_in, s5_tables, ssm_d, glu_w, glu_b, swa_sinks):
    w_main = w_in.astype(BF16)
    qd0, qs0 = _proj_start("qd"), _proj_start("qs")
    dh = SWA_HEAD_DIM
    w_q = jnp.concatenate(
        [w_in[:, :, qd0:qd0 + DIFF_WIDTH].astype(F32) * (DIFF_HEAD_DIM ** -0.5 * LOG2E)]
        + [w_in[:, :, qs0 + h * dh:qs0 + (h + 1) * dh].astype(F32) * dh ** -0.5 for h in (0, 2, 1, 3)],
        axis=2).astype(BF16)
    bbd, cbd, row_tabs, vec_tabs = s5_tables
    depth = w_in.shape[0]
    sink_rows = jnp.broadcast_to(swa_sinks.astype(F32).reshape(depth, SWA_Q_HEADS, 1, 1),
                                 (depth, SWA_Q_HEADS, 1, LANES))
    return (norm_gain.astype(F32).reshape(depth, 1, D_MODEL), w_main, w_q,
            bbd, cbd, row_tabs, vec_tabs,
            ssm_d.astype(F32).reshape(depth, 1, SSM_WIDTH), glu_w.astype(BF16),
            glu_b.astype(F32).reshape(depth, 1, SSM_WIDTH), sink_rows)


def _inproj(layer, x, mod, params):
    bsz, seq, _ = x.shape
    grid = (bsz, seq // TOK_TILE)
    t = np.arange(S5_CHUNK, dtype=np.int32)
    tri = np.logical_and(t[:, None] // S5_BLOCK == t[None, :] // S5_BLOCK,
                         t[:, None] >= t[None, :]).astype(BF16)
    swa_kf, swa_qf = _swa_features()
    tiles_per_step = TOK_TILE // ATT_TK
    out_specs, out_shape = [], []
    for name in INPROJ_OUTPUTS:
        n = {"y_ssm": SSM_WIDTH, "y_swa": SWA_WIDTH}.get(name) or _proj_size(name)
        if name == "vd":
            out_specs.append(pl.BlockSpec((1, tiles_per_step, n, ATT_TK), lambda b, i: (b, i, 0, 0)))
            out_shape.append(jax.ShapeDtypeStruct((bsz, seq // ATT_TK, n, ATT_TK), BF16))
        elif name in PROJ_TRANSPOSED:
            out_specs.append(pl.BlockSpec((1, n, TOK_TILE), lambda b, i: (b, 0, i)))
            out_shape.append(jax.ShapeDtypeStruct((bsz, n, seq), BF16))
        else:
            out_specs.append(pl.BlockSpec((1, TOK_TILE, n), lambda b, i: (b, i, 0)))
            out_shape.append(jax.ShapeDtypeStruct((bsz, seq, n), BF16))
    full = lambda a: pl.BlockSpec(a.shape, lambda b, i: (0,) * a.ndim)
    of_layer = lambda a: pl.BlockSpec((None,) + a.shape[1:],
                                      lambda b, i: (layer,) + (0,) * (a.ndim - 1))
    (gain, w_main, w_q, bbd, cbd, row_tabs, vec_tabs, d_skip, w_glu, b_glu, sink_rows) = params
    operands = [(gain, of_layer), (w_main, of_layer), (w_q, of_layer),
                (bbd, of_layer), (cbd, of_layer), (tri, full), (row_tabs, of_layer),
                (vec_tabs, of_layer), (d_skip, of_layer), (w_glu, of_layer), (b_glu, of_layer),
                (sink_rows, of_layer), (swa_kf, full), (swa_qf, full)]
    outs = pl.pallas_call(
        _inproj_kernel,
        grid=grid,
        in_specs=[
            pl.BlockSpec((1, TOK_TILE, D_MODEL), lambda b, i: (b, i, 0)),
            pl.BlockSpec((None, 1, 3, D_MODEL), lambda b, i: (layer, b, 0, 0)),
        ] + [spec(a) for a, spec in operands],
        out_specs=out_specs,
        out_shape=out_shape,
        scratch_shapes=[
            pltpu.VMEM((TOK_TILE, 2 * SSM_WIDTH), F32),
            pltpu.VMEM((S5_CHUNK, 2 * SSM_LANES), F32),
            pltpu.VMEM((SUBLANES, SSM_LANES), F32),
            pltpu.VMEM((TOK_TILE, SWA_WIDTH), BF16),
            pltpu.VMEM((WINDOW + TOK_TILE, SWA_KV_WIDTH), BF16),
            pltpu.VMEM((SWA_KV_WIDTH, WINDOW + TOK_TILE), BF16),
            pltpu.VMEM((TOK_TILE, SWA_WIDTH), F32),
            pltpu.VMEM((TOK_TILE // WINDOW, 2 * WINDOW, SWA_Q_HEADS * WINDOW), F32),
        ],
        compiler_params=pltpu.CompilerParams(
            dimension_semantics=("parallel", "arbitrary"), vmem_limit_bytes=VMEM_LIMIT),
        name="inproj",
    )(x, mod, *[a for a, _ in operands])
    return dict(zip(INPROJ_OUTPUTS, outs))


def _s5_tables(lam_re, lam_im, log_step, b_re, b_im, c_re, c_im):
    g, p, h = SSM_GROUPS, SSM_STATE, SSM_GROUP
    step = jnp.exp(log_step.astype(F32))[:, None]
    lr = lam_re.astype(F32)
    li = lam_im.astype(F32)
    mag = jnp.exp(lr * step)
    ang = li * step
    ab_re = mag * jnp.cos(ang)
    ab_im = mag * jnp.sin(ang)
    den = lr * lr + li * li
    f_re = ((ab_re - 1.0) * lr + ab_im * li) / den
    f_im = (ab_im * lr - (ab_re - 1.0) * li) / den
    br = b_re.astype(F32)
    bi = b_im.astype(F32)
    bb_re = f_re[..., None] * br - f_im[..., None] * bi
    bb_im = f_re[..., None] * bi + f_im[..., None] * br
    eye = jnp.eye(g, dtype=F32)
    bbd_re = jnp.einsum("gph,gk->ghkp", bb_re, eye).reshape(g * h, g * p)
    bbd_im = jnp.einsum("gph,gk->ghkp", bb_im, eye).reshape(g * h, g * p)
    bbd = jnp.concatenate([bbd_re, bbd_im], axis=1).astype(BF16)
    cbd_re = jnp.einsum("ghp,gk->gpkh", c_re.astype(F32), eye).reshape(g * p, g * h)
    cbd_im = jnp.einsum("ghp,gk->gpkh", c_im.astype(F32), eye).reshape(g * p, g * h)
    cbd = jnp.concatenate([cbd_re, -cbd_im], axis=0).astype(BF16)

    def power(n):
        n = jnp.asarray(n, F32).reshape(-1, 1, 1)
        m = jnp.exp(lr * step * n)
        return ((m * jnp.cos(ang * n)).reshape(-1, g * p), (m * jnp.sin(ang * n)).reshape(-1, g * p))

    c = S5_BLOCK // 2
    rows = jnp.arange(S5_BLOCK, dtype=F32)
    pre = power(c - rows)
    post = power(rows - c)
    row_tabs = jnp.stack([pre[0], pre[1], post[0], post[1]]).astype(BF16)
    vec = [power(float(S5_BLOCK - 1 - c)),
           power(float(S5_BLOCK)),
           power(float(c + 1))]
    vec_tabs = jnp.concatenate([t for pair in vec for t in pair], axis=0)
    return bbd, cbd, row_tabs, vec_tabs


ACC_ROWS = 2 * DIFF_HEAD_DIM + 16
POS_SPLIT = 64


LOG2E_PARTS = 3


def _diff_features(seq):
    parts, rest = [], LOG2E
    for _ in range(LOG2E_PARTS):
        part = float(np.float32(rest).astype(BF16))
        parts.append(part)
        rest -= part
    pos = np.arange(seq, dtype=np.int32)
    hi = ((pos // POS_SPLIT) * POS_SPLIT).astype(np.float32)
    lo = (pos % POS_SPLIT).astype(np.float32)
    lane = (np.arange(2 * DIFF_HEAD_DIM, dtype=np.int32) % DIFF_HEAD_DIM)[None, :]
    qf = sum(np.where(lane % LOG2E_PARTS == i, np.float32(part), np.float32(0)) for i, part in enumerate(parts))
    qf = np.where(lane < 2 * LOG2E_PARTS, qf, np.float32(0)).astype(BF16)
    tabs = []
    for h in range(DIFF_HEADS):
        slope = np.float32(_alibi_slope(SWA_Q_HEADS + h))
        tabs.append(np.where(lane < LOG2E_PARTS, slope * hi[:, None], np.float32(0))
                    + np.where(np.logical_and(lane >= LOG2E_PARTS, lane < 2 * LOG2E_PARTS),
                               slope * lo[:, None], np.float32(0)))
    return np.stack(tabs).astype(BF16), qf


def _diff_kernel(q_ref, k_ref, kf_ref, qf_ref, vt_ref, lam_ref, o_ref,
                 acc_ref, s_ref, qp_ref, m_ref, msub_ref, alpha_ref, *, lam_init):
    qi = pl.program_id(1)
    dh = DIFF_HEAD_DIM
    hw = 2 * dh
    lane_q = lax.broadcasted_iota(jnp.int32, (ATT_TQ, hw), 1)
    q_feat = jnp.broadcast_to(qf_ref[...], (ATT_TQ, hw))
    units = [(b, h) for b in range(q_ref.shape[0]) for h in range(DIFF_HEADS)]
    for u, (b, h) in enumerate(units):
        q = q_ref[b, :, h * hw:(h + 1) * hw]
        qp_ref[2 * u] = jnp.where(lane_q < dh, q, q_feat)
        qp_ref[2 * u + 1] = jnp.where(lane_q >= dh, q, q_feat)
    lane_k = lax.broadcasted_iota(jnp.int32, (ATT_TK, hw), 1)
    ones_row = jnp.where(lax.broadcasted_iota(jnp.int32, (ACC_ROWS - hw, ATT_TK), 0) == 0,
                         1.0, 0.0).astype(BF16)
    krow = lax.broadcasted_iota(jnp.int32, (ATT_TK, ATT_TQ), 0)
    qcol = lax.broadcasted_iota(jnp.int32, (ATT_TK, ATT_TQ), 1)
    nt = (((1,), (1,)), ((), ()))

    acc_ref[...] = jnp.zeros_like(acc_ref)
    m_ref[...] = jnp.full(m_ref.shape, -jnp.inf, F32)

    def scores(t, buf, masked):
        k0 = pl.multiple_of(t * ATT_TK, ATT_TK)
        for u, (b, h) in enumerate(units):
            k = k_ref[b, pl.ds(k0, ATT_TK), h * hw:(h + 1) * hw]
            kf = kf_ref[h, pl.ds(k0, ATT_TK), :]
            for c, kp in ((2 * u, jnp.where(lane_k < dh, k, kf)),
                          (2 * u + 1, jnp.where(lane_k >= dh, k, kf))):
                s = lax.dot_general(kp, qp_ref[c], nt, preferred_element_type=F32)
                if masked:
                    s = jnp.where(krow <= qcol, s, -jnp.inf)
                s_ref[buf, c] = s
                m_old = m_ref[c]
                m_new = jnp.maximum(m_old, jnp.max(s, axis=0, keepdims=True))
                m_ref[c] = m_new
                msub_ref[buf, c] = m_new
                alpha_ref[buf, c] = jnp.exp2(m_old - m_new)

    def values(t, buf):
        for u, (b, h) in enumerate(units):
            vta = jnp.concatenate([vt_ref[b, t, h * hw:(h + 1) * hw, :], ones_row], axis=0)
            for c in (2 * u, 2 * u + 1):
                p = jnp.exp2(s_ref[buf, c] - msub_ref[buf, c]).astype(BF16)
                acc_ref[c] = (alpha_ref[buf, c] * acc_ref[c]
                              + jnp.dot(vta, p, preferred_element_type=F32))

    pairs = jnp.maximum(qi - 1, 0) // 2
    rest = qi - 1 - 2 * pairs
    t0 = 2 * pairs

    @pl.when(qi == 0)
    def _():
        scores(0, 0, True)
        values(0, 0)

    @pl.when(qi >= 1)
    def _():
        scores(0, 0, False)

    def two_tiles(i, carry):
        t = 2 * i
        scores(t + 1, 1, False)
        values(t, 0)
        scores(t + 2, 0, False)
        values(t + 1, 1)
        return carry

    lax.fori_loop(0, pairs, two_tiles, 0)

    @pl.when(jnp.logical_and(qi >= 1, rest == 1))
    def _():
        scores(t0 + 1, 1, False)
        values(t0, 0)
        scores(t0 + 2, 0, True)
        values(t0 + 1, 1)
        values(t0 + 2, 0)

    @pl.when(jnp.logical_and(qi >= 1, rest == 0))
    def _():
        scores(t0 + 1, 1, True)
        values(t0, 0)
        values(t0 + 1, 1)

    lp = lam_ref[...]
    lam = (jnp.exp(jnp.sum(lp[0:1] * lp[1:2], axis=-1, keepdims=True))
           - jnp.exp(jnp.sum(lp[2:3] * lp[3:4], axis=-1, keepdims=True)) + lam_init)
    for u, (b, h) in enumerate(units):
        a1 = acc_ref[2 * u]
        a2 = acc_ref[2 * u + 1]
        o_t = (a1[:hw] * (1.0 / a1[hw:hw + 1]) - lam * (a2[:hw] * (1.0 / a2[hw:hw + 1])))
        o_ref[b, h * hw:(h + 1) * hw, :] = o_t.astype(BF16)


def _diff_attention(qd, kd, vd_t, lam_params, lam_init):
    bsz, seq, _ = qd.shape
    hw = 2 * DIFF_HEAD_DIM
    n_kv = seq // ATT_TK
    nb = DIFF_STREAMS
    chains = 2 * DIFF_HEADS * nb
    kf, qf = _diff_features(seq)
    tile_t = pl.BlockSpec((nb, DIFF_WIDTH, ATT_TQ), lambda b, i: (b, 0, i))
    return pl.pallas_call(
        functools.partial(_diff_kernel, lam_init=lam_init),
        grid=(bsz // nb, seq // ATT_TQ),
        in_specs=[
            pl.BlockSpec((nb, ATT_TQ, DIFF_WIDTH), lambda b, i: (b, i, 0)),
            pl.BlockSpec((nb, seq, DIFF_WIDTH), lambda b, i: (b, 0, 0)),
            pl.BlockSpec((DIFF_HEADS, seq, hw), lambda b, i: (0, 0, 0)),
            pl.BlockSpec((1, hw), lambda b, i: (0, 0)),
            pl.BlockSpec((nb, n_kv, DIFF_WIDTH, ATT_TK), lambda b, i: (b, 0, 0, 0)),
            pl.BlockSpec((4, DIFF_HEAD_DIM), lambda b, i: (0, 0)),
        ],
        out_specs=tile_t,
        out_shape=jax.ShapeDtypeStruct((bsz, DIFF_WIDTH, seq), BF16),
        scratch_shapes=[pltpu.VMEM((chains, ACC_ROWS, ATT_TQ), F32),
                        pltpu.VMEM((2, chains, ATT_TK, ATT_TQ), F32),
                        pltpu.VMEM((chains, ATT_TQ, hw), BF16),
                        pltpu.VMEM((chains, 1, ATT_TQ), F32),
                        pltpu.VMEM((2, chains, 1, ATT_TQ), F32),
                        pltpu.VMEM((2, chains, 1, ATT_TQ), F32)],
        compiler_params=pltpu.CompilerParams(
            dimension_semantics=("parallel", "parallel"), vmem_limit_bytes=VMEM_LIMIT),
        name="diffattn",
    )(qd, kd, kf, qf, vd_t, lam_params)


SWA_ACC_ROWS = SWA_HEAD_DIM + 16


def _swa_features():
    w = WINDOW
    zero = np.float32(0)
    lane = np.arange(LANES, dtype=np.int32)[None, :]
    r = np.arange(2 * w, dtype=np.float32)[:, None]
    kf = np.where(lane == 0, r, zero) + np.where(lane == 1, np.float32(1), zero)
    c = np.arange(w, dtype=np.float32)[:, None]
    qf = np.stack([np.where(lane == 0, np.float32(_alibi_slope(h)), zero)
                   + np.where(lane == 1, np.float32(-_alibi_slope(h)) * (w + c), zero)
                   for h in range(SWA_Q_HEADS)])
    return kf.astype(BF16), qf.astype(BF16)


def _outproj_kernel(x_ref, ys_ref, od_ref, zd_ref, yw_ref, mod_ref, w_ref, sg_ref, fg_ref, o_ref,
                    *, final_norm, lam_init):
    a = SSM_WIDTH
    b = SSM_WIDTH + DIFF_WIDTH
    hw = 2 * DIFF_HEAD_DIM
    gain = sg_ref[...] * (1.0 - lam_init)
    heads = []
    for h in range(DIFF_HEADS):
        o_t = od_ref[0, h * hw:(h + 1) * hw, :].astype(F32)
        o_t = o_t * lax.rsqrt(jnp.mean(o_t * o_t, axis=0, keepdims=True) + RMS_EPS) * gain
        z_t = zd_ref[0, h * hw:(h + 1) * hw, :].astype(F32)
        heads.append((o_t * _silu(z_t)).astype(BF16))
    yd_t = jnp.concatenate(heads, axis=0)
    y = (jnp.dot(ys_ref[0], w_ref[:a, :], preferred_element_type=F32)
         + lax.dot_general(yd_t, w_ref[a:b, :], (((0,), (0,)), ((), ())),
                           preferred_element_type=F32)
         + jnp.dot(yw_ref[0], w_ref[b:, :], preferred_element_type=F32))
    out = x_ref[0] + mod_ref[0][2:3, :] * y
    if final_norm:
        out = out * lax.rsqrt(jnp.mean(out * out, axis=-1, keepdims=True) + RMS_EPS) * fg_ref[...]
    o_ref[0] = out


def _outproj(layer, x, y_ssm, o_diff_t, z_diff_t, y_swa, mod, w_out_bf16, subln_gain, final_gain,
             final_norm, lam_init):
    bsz, seq, _ = x.shape
    hw = 2 * DIFF_HEAD_DIM
    gain_rows = jnp.broadcast_to(subln_gain.astype(F32).reshape(hw, 1), (hw, OUT_TILE))
    tok = lambda n: pl.BlockSpec((1, OUT_TILE, n), lambda b, i: (b, i, 0))
    tok_t = pl.BlockSpec((1, DIFF_WIDTH, OUT_TILE), lambda b, i: (b, 0, i))
    return pl.pallas_call(
        functools.partial(_outproj_kernel, final_norm=final_norm, lam_init=lam_init),
        grid=(bsz, seq // OUT_TILE),
        in_specs=[
            tok(D_MODEL), tok(SSM_WIDTH), tok_t, tok_t, tok(SWA_WIDTH),
            pl.BlockSpec((None, 1, 3, D_MODEL), lambda b, i: (layer, b, 0, 0)),
            pl.BlockSpec((None, D_MODEL, D_MODEL), lambda b, i: (layer, 0, 0)),
            pl.BlockSpec((hw, OUT_TILE), lambda b, i: (0, 0)),
            pl.BlockSpec((1, D_MODEL), lambda b, i: (0, 0)),
        ],
        out_specs=tok(D_MODEL),
        out_shape=jax.ShapeDtypeStruct((bsz, seq, D_MODEL), F32),
        compiler_params=pltpu.CompilerParams(
            dimension_semantics=("parallel", "parallel"), vmem_limit_bytes=OUTPROJ_VMEM_LIMIT),
        name="outproj",
    )(x, y_ssm, o_diff_t, z_diff_t, y_swa, mod, w_out_bf16, gain_rows,
      final_gain.reshape(1, D_MODEL).astype(F32))


def kernel(x, c, norm_gain, ada_w, ada_b, w_in, w_out, ssm_lam_re, ssm_lam_im, ssm_log_step,
           ssm_b_re, ssm_b_im, ssm_c_re, ssm_c_im, ssm_d, glu_w, glu_b,
           diff_lq1, diff_lk1, diff_lq2, diff_lk2, diff_subln, swa_sinks, final_gain):
    bsz = x.shape[0]
    mod = _ada(c, ada_w, ada_b).reshape(DEPTH, bsz, 3, D_MODEL)
    s5_tables = jax.vmap(_s5_tables)(ssm_lam_re, ssm_lam_im, ssm_log_step,
                                     ssm_b_re, ssm_b_im, ssm_c_re, ssm_c_im)
    params = _inproj_params(norm_gain, w_in, s5_tables, ssm_d, glu_w, glu_b, swa_sinks)
    w_out_bf16 = w_out.astype(BF16)
    for l in range(DEPTH):
        proj = _inproj(l, x, mod, params)
        lam_init = 0.8 - 0.6 * math.exp(-0.3 * l)
        lam_params = jnp.stack([diff_lq1[l], diff_lk1[l], diff_lq2[l], diff_lk2[l]]).astype(F32)
        o_diff = _diff_attention(proj["qd"], proj["kd"], proj["vd"], lam_params, lam_init)
        x = _outproj(l, x, proj["y_ssm"], o_diff, proj["z_diff"], proj["y_swa"], mod, w_out_bf16,
                     diff_subln[l], final_gain, final_norm=(l == DEPTH - 1), lam_init=lam_init)
    return x
```
